```python
import math
import jax, jax.numpy as jnp
from jax import lax
import numpy as np

D_MODEL = 1024
BATCH = 8
SEQ = 2048
DEPTH = 2
DEC_BATCH = 32
DEC_SEQ = 32
PAST_LEN = 4096

CHUNK = 64
N_A = DEPTH // 2
N_B = DEPTH - N_A
H_A = 8
DK_A = D_MODEL // H_A
DV_A = 2 * DK_A
A_IN = 2 * H_A * DK_A + 2 * H_A * DV_A
H_B = 8
DH_B = D_MODEL // (2 * H_B)
QK_B = 2 * H_B * DH_B
V_B = H_B * 2 * DH_B
D_FF = 11 * D_MODEL // 4
N_MOD = 9
ROPE_THETA = 10000.0
EPS = 1e-6
Q_BLOCK = 128
HALF = 0.5

kernel_name = "yoco_retention_diffattn_streaming_step"


def rms_norm(x, g):
    xf = x.astype(jnp.float32)
    y = xf * lax.rsqrt(jnp.mean(xf * xf, axis=-1, keepdims=True) + EPS)
    return (y * g.astype(jnp.float32)).astype(x.dtype)


def modulate(x, g, shift, scale):
    return rms_norm(x, g) * (1 + scale[:, None, :]) + shift[:, None, :]


def swiglu(h, w_in, w_out):
    a, b = jnp.split(h @ w_in, 2, axis=-1)
    return (jax.nn.silu(a) * b) @ w_out


def rotary(x, pos):
    d = x.shape[-1]
    inv = jnp.power(ROPE_THETA, -jnp.arange(0, d, 2, dtype=jnp.float32) / d)
    ang = pos[:, None] * inv[None, :]
    ang = ang.reshape((ang.shape[0],) + (1,) * (x.ndim - 3) + (ang.shape[1],))
    cos, sin = jnp.cos(ang), jnp.sin(ang)
    x1, x2 = jnp.split(x.astype(jnp.float32), 2, axis=-1)
    return jnp.concatenate([x1 * cos - x2 * sin, x2 * cos + x1 * sin], axis=-1).astype(x.dtype)


def retention_log_gamma():
    return jnp.log1p(-jnp.exp2(-5.0 - jnp.arange(H_A, dtype=jnp.float32)))


def retention_block(state, q, k, v, log_gamma):
    n = q.shape[2]
    idx = jnp.arange(n, dtype=jnp.float32)
    lg = log_gamma[:, None]
    d_intra = jnp.exp(lg[:, :, None] * jnp.abs(idx[:, None] - idx[None, :])).astype(q.dtype)
    q_dec = jnp.exp(lg * idx).astype(q.dtype)
    k_dec = jnp.exp(lg * (n - idx)).astype(q.dtype)
    c_dec = jnp.exp(log_gamma * n).astype(q.dtype)
    scores = jnp.einsum('bhid,bhjd->bhij', q, k) * d_intra
    out = (jnp.einsum('bhij,bhjv->bhiv', scores, v)
           + jnp.einsum('bhid,bhdv->bhiv', q * q_dec[..., None], state))
    new_state = state * c_dec[:, None, None] + jnp.einsum('bhjd,bhjv->bhdv', k * k_dec[..., None], v)
    return out, new_state


def retention_mixer(h, pos, state, w_in, g_gn, w_out, prompt):
    b, s, _ = h.shape
    q, k, v, g = jnp.split(h @ w_in, [H_A * DK_A, 2 * H_A * DK_A, 2 * H_A * DK_A + H_A * DV_A], axis=-1)
    q = rotary(q.reshape(b, s, H_A, DK_A), pos)
    k = rotary(k.reshape(b, s, H_A, DK_A), pos) * (DK_A ** -0.5)
    v = v.reshape(b, s, H_A, DV_A)
    lg = retention_log_gamma()
    if prompt:
        nc = s // CHUNK

        def to_chunks(t):
            return t.reshape(b, nc, CHUNK, H_A, t.shape[-1]).transpose(1, 0, 3, 2, 4)

        def step(st, qkv):
            out, st = retention_block(st, *qkv, lg)
            return st, out

        state0 = jnp.zeros((b, H_A, DK_A, DV_A), h.dtype)
        new_state, o = lax.scan(step, state0, (to_chunks(q), to_chunks(k), to_chunks(v)))
        o = o.transpose(1, 0, 3, 2, 4).reshape(b, s, H_A, DV_A)
    else:
        o, new_state = retention_block(state, q.transpose(0, 2, 1, 3), k.transpose(0, 2, 1, 3),
                                       v.transpose(0, 2, 1, 3), lg)
        o = o.transpose(0, 2, 1, 3)
    o = rms_norm(o, g_gn.reshape(H_A, DV_A))
    y = (jax.nn.silu(g) * o.reshape(b, s, H_A * DV_A)) @ w_out
    return y, new_state


def shared_kv(x, c, pos, w_ada_kv, b_ada_kv, g_kv, w_kv):
    b, s, _ = x.shape
    shift, scale = jnp.split(jax.nn.silu(c) @ w_ada_kv + b_ada_kv, 2, axis=-1)
    k, v = jnp.split(modulate(x, g_kv, shift, scale) @ w_kv, 2, axis=-1)
    k = rotary(k.reshape(b, s, H_B, 2, DH_B), pos)
    v = v.reshape(b, s, H_B, 2 * DH_B)
    return k, v


def diff_attention_prompt(q, k, v, lam):
    b, s = q.shape[:2]
    nb = s // Q_BLOCK
    key_pos = jnp.arange(s)
    qb = q.reshape(b, nb, Q_BLOCK, H_B, 2, DH_B).transpose(1, 0, 2, 3, 4, 5)

    def block(args):
        qblk, i = args
        qpos = i * Q_BLOCK + jnp.arange(Q_BLOCK)
        visible = key_pos[None, :] < (qpos[:, None] // CHUNK + 1) * CHUNK
        logits = jnp.einsum('bqhtd,bkhtd->bhtqk', qblk, k).astype(jnp.float32) * (DH_B ** -0.5)
        p = jax.nn.softmax(jnp.where(visible, logits, -jnp.inf), axis=-1)
        pd = (p[:, :, 0] - lam * p[:, :, 1]).astype(v.dtype)
        return jnp.einsum('bhqk,bkhv->bqhv', pd, v)

    o = lax.map(block, (qb, jnp.arange(nb)))
    return o.transpose(1, 0, 2, 3, 4).reshape(b, s, H_B, 2 * DH_B)


def diff_attention_sample(q, k_new, v_new, k_past, v_past, lam):
    sc = DH_B ** -0.5
    lp = jnp.einsum('bqhtd,bkhtd->bhtqk', q, k_past).astype(jnp.float32) * sc
    ln = jnp.einsum('bqhtd,bkhtd->bhtqk', q, k_new).astype(jnp.float32) * sc
    p = jax.nn.softmax(jnp.concatenate([lp, ln], axis=-1), axis=-1)
    pd = (p[:, :, 0] - lam * p[:, :, 1]).astype(v_new.dtype)
    past = k_past.shape[1]
    return (jnp.einsum('bhqk,bkhv->bqhv', pd[..., :past], v_past)
            + jnp.einsum('bhqk,bkhv->bqhv', pd[..., past:], v_new))


def diff_mixer(h, pos, k, v, k_past, v_past, w_q, lam_p, g_sub, w_out, lam_init, prompt):
    b, s, _ = h.shape
    q = rotary((h @ w_q).reshape(b, s, H_B, 2, DH_B), pos)
    lp = lam_p.astype(jnp.float32)
    lam = jnp.exp(jnp.sum(lp[0] * lp[1])) - jnp.exp(jnp.sum(lp[2] * lp[3])) + lam_init
    if prompt:
        o = diff_attention_prompt(q, k, v, lam)
    else:
        o = diff_attention_sample(q, k, v, k_past, v_past, lam)
    o = rms_norm(o, g_sub) * (1 - lam_init)
    return o.reshape(b, s, V_B) @ w_out


def run_trunk(x, c, pos, ret_state, k_past, v_past, prompt,
              w_ada, b_ada, g_norm, w_ffn_in, w_ffn_out, w_in_a, g_gn_a, w_out_a,
              w_ada_kv, b_ada_kv, g_kv, w_kv, w_q_b, lam_b, g_subln_b, w_out_b):
    b = x.shape[0]
    new_ret = []
    k = v = None
    for l in range(DEPTH):
        mod = (jax.nn.silu(c) @ w_ada[l] + b_ada[l]).reshape(b, N_MOD, D_MODEL)
        h = modulate(x, g_norm[l, 0], mod[:, 0], mod[:, 1])
        x = x + HALF * mod[:, 2, None] * rms_norm(swiglu(h, w_ffn_in[l, 0], w_ffn_out[l, 0]), g_norm[l, 1])
        h = modulate(x, g_norm[l, 2], mod[:, 3], mod[:, 4])
        if l < N_A:
            y, st = retention_mixer(h, pos, None if prompt else ret_state[l],
                                    w_in_a[l], g_gn_a[l], w_out_a[l], prompt)
            new_ret.append(st)
        else:
            j = l - N_A
            lam_init = 0.8 - 0.6 * math.exp(-0.3 * l)
            y = diff_mixer(h, pos, k, v, k_past, v_past, w_q_b[j], lam_b[j], g_subln_b[j], w_out_b[j],
                           lam_init, prompt)
        x = x + mod[:, 5, None] * rms_norm(y, g_norm[l, 3])
        h = modulate(x, g_norm[l, 4], mod[:, 6], mod[:, 7])
        x = x + HALF * mod[:, 8, None] * rms_norm(swiglu(h, w_ffn_in[l, 1], w_ffn_out[l, 1]), g_norm[l, 5])
        if l == N_A - 1:
            k, v = shared_kv(x, c, pos, w_ada_kv, b_ada_kv, g_kv, w_kv)
    return x, jnp.stack(new_ret), k, v


def _nrm(key, shape, s):
    return jax.random.normal(key, shape, jnp.float32) * s


def setup_inputs(seed: int = 0) -> dict:
    key = jax.random.key(seed)
    ks = jax.random.split(key, 23)
    D = D_MODEL
    return {
        "x_prompt": _nrm(ks[0], (BATCH, SEQ, D), 1.0),
        "x_sample": _nrm(ks[1], (DEC_BATCH, DEC_SEQ, D), 1.0),
        "state_ret": _nrm(ks[2], (N_A, DEC_BATCH, H_A, DK_A, DV_A), 0.5),
        "cache_k": _nrm(ks[3], (DEC_BATCH, PAST_LEN, 2 * H_B, DH_B), 1.0),
        "cache_v": _nrm(ks[4], (DEC_BATCH, PAST_LEN, H_B, 2 * DH_B), 1.0),
        "c_prompt": _nrm(ks[5], (BATCH, D), 1.0),
        "c_sample": _nrm(ks[6], (DEC_BATCH, D), 1.0),
        "w_ada": _nrm(ks[7], (DEPTH, D, N_MOD * D), 0.5 * D ** -0.5),
        "b_ada": _nrm(ks[8], (DEPTH, N_MOD * D), 0.01),
        "g_norm": 1.0 + _nrm(ks[9], (DEPTH, 6, D), 0.02),
        "w_ffn_in": _nrm(ks[10], (DEPTH, 2, D, 2 * D_FF), D ** -0.5),
        "w_ffn_out": _nrm(ks[11], (DEPTH, 2, D_FF, D), D_FF ** -0.5),
        "w_in_a": _nrm(ks[12], (N_A, D, A_IN), D ** -0.5),
        "g_gn_a": 1.0 + _nrm(ks[13], (N_A, H_A * DV_A), 0.02),
        "w_out_a": _nrm(ks[14], (N_A, H_A * DV_A, D), (H_A * DV_A) ** -0.5),
        "w_ada_kv": _nrm(ks[15], (D, 2 * D), 0.5 * D ** -0.5),
        "b_ada_kv": _nrm(ks[16], (2 * D,), 0.01),
        "g_kv": 1.0 + _nrm(ks[17], (D,), 0.02),
        "w_kv": _nrm(ks[18], (D, QK_B + V_B), D ** -0.5),
        "w_q_b": _nrm(ks[19], (N_B, D, QK_B), D ** -0.5),
        "lam_b": _nrm(ks[20], (N_B, 4, DH_B), 0.1),
        "g_subln_b": 1.0 + _nrm(ks[21], (N_B, 2 * DH_B), 0.02),
        "w_out_b": _nrm(ks[22], (N_B, V_B, D), V_B ** -0.5),
    }


def reference(x_prompt, x_sample, state_ret, cache_k, cache_v, c_prompt, c_sample,
              w_ada, b_ada, g_norm, w_ffn_in, w_ffn_out, w_in_a, g_gn_a, w_out_a,
              w_ada_kv, b_ada_kv, g_kv, w_kv, w_q_b, lam_b, g_subln_b, w_out_b):
    weights = (w_ada, b_ada, g_norm, w_ffn_in, w_ffn_out, w_in_a, g_gn_a, w_out_a,
               w_ada_kv, b_ada_kv, g_kv, w_kv, w_q_b, lam_b, g_subln_b, w_out_b)
    bp, sp = x_prompt.shape[:2]
    bs, ss = x_sample.shape[:2]
    pos_p = jnp.arange(sp, dtype=jnp.float32)
    y_prompt, ret_p, k_p, v_p = run_trunk(x_prompt, c_prompt, pos_p, None, None, None, True, *weights)
    pos_s = PAST_LEN + jnp.arange(ss, dtype=jnp.float32)
    k_past = cache_k.reshape(cache_k.shape[0], cache_k.shape[1], H_B, 2, DH_B)
    y_sample, ret_s, k_s, v_s = run_trunk(x_sample, c_sample, pos_s, state_ret, k_past, cache_v, False, *weights)
    return (y_prompt, y_sample,
            ret_p, k_p.reshape(bp, sp, 2 * H_B, DH_B), v_p,
            ret_s, k_s.reshape(bs, ss, 2 * H_B, DH_B), v_s)
```

```python
import functools
import math

import jax
import jax.numpy as jnp
from jax import lax
from jax.experimental import pallas as pl
from jax.experimental.pallas import tpu as pltpu

D_MODEL = 1024
CHUNK = 64
CHUNK_SHIFT = 6
H_A = 8
DK_A = 128
DV_A = 256
QK_A = H_A * DK_A
V_A = H_A * DV_A
H_B = 8
DH_B = 64
HB_W = 2 * DH_B
D_FF = 2816
N_MOD = 9
ROPE_THETA = 10000.0
EPS = 1e-6
HALF = 0.5
PAST_LEN = 4096
NEG = -1e30

LANES = 128
VMEM_LIMIT = 56 * 1024 * 1024

BF = jnp.bfloat16
F32 = jnp.float32

LOG_GAMMA = [math.log1p(-(2.0 ** (-5 - h))) for h in range(H_A)]


def _dot(a, b):
    return jnp.dot(a, b, preferred_element_type=F32)


def _dot_nt(a, b):
    return lax.dot_general(a, b, (((1,), (1,)), ((), ())), preferred_element_type=F32)


def _dot_tn(a, b):
    return lax.dot_general(a, b, (((0,), (0,)), ((), ())), preferred_element_type=F32)


def _rmsn(x):
    return x * lax.rsqrt(jnp.mean(x * x, axis=-1, keepdims=True) + EPS)


def _silu(x):
    return x * jax.nn.sigmoid(x)


def _resident(shape, index_map):
    return pl.BlockSpec(shape, index_map, pipeline_mode=pl.Buffered(1))


def _params(n_axes):
    return pltpu.CompilerParams(dimension_semantics=("arbitrary",) * n_axes,
                                vmem_limit_bytes=VMEM_LIMIT)


def _ada_kernel(c_ref, w_ref, b_ref, o_ref):
    sc = _silu(c_ref[...]).astype(BF)
    o_ref[...] = _dot(sc, w_ref[...].astype(BF)) + b_ref[...]


def _ada_proj(c_all, w, b, tn=1024):
    nl, d, n = w.shape
    r = c_all.shape[0]
    return pl.pallas_call(
        _ada_kernel,
        grid=(nl, n // tn),
        in_specs=[
            pl.BlockSpec((r, d), lambda l, j: (0, 0)),
            pl.BlockSpec((None, d, tn), lambda l, j: (l, 0, j)),
            pl.BlockSpec((None, 1, tn), lambda l, j: (l, 0, j)),
        ],
        out_specs=pl.BlockSpec((None, r, tn), lambda l, j: (l, 0, j)),
        out_shape=jax.ShapeDtypeStruct((nl, r, n), F32),
        compiler_params=_params(2),
        name="ada_proj",
    )(c_all, w, b.reshape(nl, 1, n))


def _ffn_kernel(x_ref, mod_ref, g_ref, win_ref, wout_ref, o_ref, *, m0, g0, fc):
    nb, ts, d = x_ref.shape
    x = x_ref[...]
    shift = mod_ref[:, m0:m0 + 1, :]
    scale = mod_ref[:, m0 + 1:m0 + 2, :]
    gate = mod_ref[:, m0 + 2:m0 + 3, :]
    h = _rmsn(x) * g_ref[g0:g0 + 1, :] * (1.0 + scale) + shift
    hb = h.reshape(nb * ts, d).astype(BF)
    acc = jnp.zeros((nb * ts, d), F32)
    for c in range(D_FF // fc):
        a = _dot(hb, win_ref[:, c * fc:(c + 1) * fc])
        b = _dot(hb, win_ref[:, D_FF + c * fc:D_FF + (c + 1) * fc])
        u = (_silu(a) * b).astype(BF)
        acc = acc + _dot(u, wout_ref[c * fc:(c + 1) * fc, :])
    y = _rmsn(acc) * g_ref[g0 + 1:g0 + 2, :]
    o_ref[...] = x + (HALF * gate) * y.reshape(nb, ts, d)


def _ffn(x, mod, mod_row0, g, w_in, w_out, *, nb, ts, m0, g0):
    b, s, d = x.shape
    nbt, nst = b // nb, s // ts
    mrow = mod_row0 // nb
    kern = functools.partial(_ffn_kernel, m0=m0, g0=g0, fc=256)
    return pl.pallas_call(
        kern,
        grid=(nbt, nst),
        in_specs=[
            pl.BlockSpec((nb, ts, d), lambda i, j: (i, j, 0)),
            pl.BlockSpec((nb, N_MOD, d), lambda i, j: (mrow + i, 0, 0)),
            pl.BlockSpec((6, d), lambda i, j: (0, 0)),
            _resident((d, 2 * D_FF), lambda i, j: (0, 0)),
            _resident((D_FF, d), lambda i, j: (0, 0)),
        ],
        out_specs=pl.BlockSpec((nb, ts, d), lambda i, j: (i, j, 0)),
        out_shape=jax.ShapeDtypeStruct(x.shape, F32),
        compiler_params=_params(2),
        name="ffn",
    )(x, mod, g, w_in, w_out)


def _decay_mask(lg, t):
    i = lax.broadcasted_iota(jnp.int32, (t, t), 0)
    j = lax.broadcasted_iota(jnp.int32, (t, t), 1)
    dist = jnp.abs(i - j).astype(F32)
    vis = (j >> CHUNK_SHIFT) <= (i >> CHUNK_SHIFT)
    return jnp.where(vis, jnp.exp(lg * dist), 0.0)


def _ret_kernel(*refs, nb, t, from_input_state):
    if from_input_state:
        (x_ref, mod_ref, g_ref, win_ref, ggn_ref, wout_ref, cos_ref, sin_ref, sin_st_ref,
         o_ref, so_ref, q_scr, k_scr, v_scr, og_scr, gat_scr, mask_scr, st_scr) = refs
    else:
        (x_ref, mod_ref, g_ref, win_ref, ggn_ref, wout_ref, cos_ref, sin_ref,
         o_ref, so_ref, q_scr, k_scr, v_scr, og_scr, gat_scr, mask_scr, st_scr) = refs
        sin_st_ref = None
    d = D_MODEL
    r = nb * t
    j = pl.program_id(1)

    @pl.when(j == 0)
    def _():
        for hd in range(H_A):
            mask_scr[hd] = _decay_mask(LOG_GAMMA[hd], t)
        if not from_input_state:
            st_scr[...] = jnp.zeros_like(st_scr)

    x = x_ref[...]
    shift = mod_ref[:, 3:4, :]
    scale = mod_ref[:, 4:5, :]
    gate = mod_ref[:, 5:6, :]
    h = _rmsn(x) * g_ref[2:3, :] * (1.0 + scale) + shift
    hb = h.reshape(r, d).astype(BF)

    cos = cos_ref[...]
    sin = sin_ref[...]
    qk = _dot(hb, win_ref[:, 0:2 * QK_A])
    for c in range(H_A):
        qc = qk[:, c * DK_A:(c + 1) * DK_A]
        q_scr[:, c * DK_A:(c + 1) * DK_A] = qc * cos + pltpu.roll(qc, DK_A // 2, 1) * sin
        kc = qk[:, QK_A + c * DK_A:QK_A + (c + 1) * DK_A]
        k_scr[:, c * DK_A:(c + 1) * DK_A] = (kc * cos + pltpu.roll(kc, DK_A // 2, 1) * sin) * (DK_A ** -0.5)
    v_scr[...] = _dot(hb, win_ref[:, 2 * QK_A:2 * QK_A + V_A]).astype(BF)
    gp = _dot(hb, win_ref[:, 2 * QK_A + V_A:2 * QK_A + 2 * V_A])

    row = lax.broadcasted_iota(jnp.int32, (t, DK_A), 0).astype(F32)
    for hd in range(H_A):
        lg = LOG_GAMMA[hd]
        qdec = jnp.exp(lg * row)
        kdec = jnp.exp(lg * (float(t) - row))
        cdec = math.exp(lg * t)
        mask = mask_scr[hd]

        def one_batch(n, carry, hd=hd, qdec=qdec, kdec=kdec, cdec=cdec, mask=mask):
            rows = pl.ds(0, t) if nb == 1 else pl.ds(pl.multiple_of(n * t, t), t)
            q = q_scr[rows, hd * DK_A:(hd + 1) * DK_A]
            k = k_scr[rows, hd * DK_A:(hd + 1) * DK_A]
            v = v_scr[rows, hd * DV_A:(hd + 1) * DV_A]
            if from_input_state:
                st = sin_st_ref[n, hd]
            else:
                st = st_scr[hd]
            s = _dot_nt(q.astype(BF), k.astype(BF)) * mask
            o = _dot(s.astype(BF), v) + _dot((q * qdec).astype(BF), st.astype(BF))
            st_new = st * cdec + _dot_tn((k * kdec).astype(BF), v)
            og_scr[rows, hd * DV_A:(hd + 1) * DV_A] = o
            if from_input_state:
                so_ref[n, hd] = st_new
            else:
                st_scr[hd] = st_new
            return carry

        if nb == 1:
            one_batch(0, 0)
        else:
            lax.fori_loop(0, nb, one_batch, 0)

    for hd in range(H_A):
        cols = slice(hd * DV_A, (hd + 1) * DV_A)
        on = _rmsn(og_scr[:, cols]) * ggn_ref[:, cols]
        gat_scr[:, cols] = (_silu(gp[:, cols]) * on).astype(BF)
    y = _dot(gat_scr[...], wout_ref[...])
    yn = _rmsn(y) * g_ref[3:4, :]
    o_ref[...] = x + gate * yn.reshape(nb, t, d)

    if not from_input_state:
        @pl.when(j == pl.num_programs(1) - 1)
        def _():
            so_ref[0] = st_scr[...]


def _ret_mixer(x, mod, mod_row0, g, w_in, g_gn, w_out, cos_t, sin_t, state_in, *, nb, t):
    b, s, d = x.shape
    nbt, nst = b // nb, s // t
    r = nb * t
    mrow = mod_row0 // nb
    from_input = state_in is not None
    kern = functools.partial(_ret_kernel, nb=nb, t=t, from_input_state=from_input)
    in_specs = [
        pl.BlockSpec((nb, t, d), lambda i, j: (i, j, 0)),
        pl.BlockSpec((nb, N_MOD, d), lambda i, j: (mrow + i, 0, 0)),
        pl.BlockSpec((6, d), lambda i, j: (0, 0)),
        _resident((d, 2 * QK_A + 2 * V_A), lambda i, j: (0, 0)),
        pl.BlockSpec((1, V_A), lambda i, j: (0, 0)),
        _resident((V_A, d), lambda i, j: (0, 0)),
        pl.BlockSpec((r, DK_A), lambda i, j: (j, 0)),
        pl.BlockSpec((r, DK_A), lambda i, j: (j, 0)),
    ]
    args = [x, mod, g, w_in, g_gn.reshape(1, V_A), w_out, cos_t, sin_t]
    if from_input:
        in_specs.append(pl.BlockSpec((nb, H_A, DK_A, DV_A), lambda i, j: (i, 0, 0, 0)))
        args.append(state_in)
    out, st = pl.pallas_call(
        kern,
        grid=(nbt, nst),
        in_specs=in_specs,
        out_specs=[
            pl.BlockSpec((nb, t, d), lambda i, j: (i, j, 0)),
            pl.BlockSpec((nb, H_A, DK_A, DV_A), lambda i, j: (i, 0, 0, 0)),
        ],
        out_shape=[
            jax.ShapeDtypeStruct(x.shape, F32),
            jax.ShapeDtypeStruct((b, H_A, DK_A, DV_A), F32),
        ],
        scratch_shapes=[
            pltpu.VMEM((r, QK_A), F32),
            pltpu.VMEM((r, QK_A), F32),
            pltpu.VMEM((r, V_A), BF),
            pltpu.VMEM((r, V_A), F32),
            pltpu.VMEM((r, V_A), BF),
            pltpu.VMEM((H_A, t, t), F32),
            pltpu.VMEM((H_A, DK_A, DV_A), F32),
        ],
        compiler_params=_params(2),
        name="ret_mixer",
    )(*args)
    return out, st


def _rope64(blk, cos, sa, sb):
    return blk * cos + pltpu.roll(blk, DH_B // 2, 1) * sa + pltpu.roll(blk, LANES - DH_B // 2, 1) * sb


def _kv_kernel(x_ref, mod_ref, g_ref, w_ref, cos_ref, sa_ref, sb_ref, k_ref, v_ref, kb_ref, vb_ref):
    nb, ts, d = x_ref.shape
    x = x_ref[...]
    shift = mod_ref[:, 0:1, :]
    scale = mod_ref[:, 1:2, :]
    h = _rmsn(x) * g_ref[...] * (1.0 + scale) + shift
    hb = h.reshape(nb * ts, d).astype(BF)
    kv = _dot(hb, w_ref[...])
    cos = cos_ref[...]
    sa = sa_ref[...]
    sb = sb_ref[...]
    ks = [_rope64(kv[:, c * LANES:(c + 1) * LANES], cos, sa, sb) for c in range(d // LANES)]
    k = jnp.concatenate(ks, axis=1).reshape(nb, ts, d)
    v = kv[:, d:2 * d].reshape(nb, ts, d)
    k_ref[...] = k
    v_ref[...] = v
    kb_ref[...] = k.astype(BF)
    vb_ref[...] = v.astype(BF)


def _shared_kv(x, modkv, mod_row0, g_kv, w_kv, cos_t, sa_t, sb_t, *, nb, ts):
    b, s, d = x.shape
    nbt, nst = b // nb, s // ts
    r = nb * ts
    mrow = mod_row0 // nb
    xspec = pl.BlockSpec((nb, ts, d), lambda i, j: (i, j, 0))
    tspec = pl.BlockSpec((r, LANES), lambda i, j: (j, 0))
    return pl.pallas_call(
        _kv_kernel,
        grid=(nbt, nst),
        in_specs=[
            xspec,
            pl.BlockSpec((nb, 2, d), lambda i, j: (mrow + i, 0, 0)),
            pl.BlockSpec((1, d), lambda i, j: (0, 0)),
            _resident((d, 2 * d), lambda i, j: (0, 0)),
            tspec, tspec, tspec,
        ],
        out_specs=[xspec, xspec, xspec, xspec],
        out_shape=[
            jax.ShapeDtypeStruct(x.shape, F32),
            jax.ShapeDtypeStruct(x.shape, F32),
            jax.ShapeDtypeStruct(x.shape, BF),
            jax.ShapeDtypeStruct(x.shape, BF),
        ],
        compiler_params=_params(2),
        name="shared_kv",
    )(x, modkv, g_kv.reshape(1, d), w_kv, cos_t, sa_t, sb_t)


def _lambda(lam_ref, lam_init):
    lp = lam_ref[...]
    l1 = jnp.sum(lp[0:1, :] * lp[1:2, :], axis=-1, keepdims=True)
    l2 = jnp.sum(lp[2:3, :] * lp[3:4, :], axis=-1, keepdims=True)
    return jnp.exp(l1) - jnp.exp(l2) + lam_init


def _project_q(hb, wq_ref, cos, sa, sb, q_scr, tq):
    q = _dot(hb, wq_ref[...])
    lane = lax.broadcasted_iota(jnp.int32, (tq, LANES), 1)
    first = lane < DH_B
    for c in range(H_B):
        rq = _rope64(q[:, c * LANES:(c + 1) * LANES], cos, sa, sb) * (DH_B ** -0.5)
        q_scr[0:tq, c * LANES:(c + 1) * LANES] = jnp.where(first, rq, 0.0).astype(BF)
        q_scr[tq:2 * tq, c * LANES:(c + 1) * LANES] = jnp.where(first, 0.0, rq).astype(BF)


def _finish_head(acc, l, lam, gsub, lam_init, tq):
    o = acc[0:tq] / l[0:tq] - lam * (acc[tq:2 * tq] / l[tq:2 * tq])
    return _rmsn(o) * gsub * (1.0 - lam_init)


def _diffp_kernel(x_ref, mod_ref, g_ref, wq_ref, k_ref, v_ref, cos_ref, sa_ref, sb_ref, lam_ref,
                  gsub_ref, wout_ref, o_ref, q_scr, acc_scr, oh_scr, *, tq, lam_init):
    qi = pl.program_id(1)
    x = x_ref[0]
    shift = mod_ref[0, 3:4, :]
    scale = mod_ref[0, 4:5, :]
    gate = mod_ref[0, 5:6, :]
    h = _rmsn(x) * g_ref[2:3, :] * (1.0 + scale) + shift
    _project_q(h.astype(BF), wq_ref, cos_ref[...], sa_ref[...], sb_ref[...], q_scr, tq)
    lam = _lambda(lam_ref, lam_init)

    ri = lax.broadcasted_iota(jnp.int32, (2 * tq, tq), 0)
    ki = lax.broadcasted_iota(jnp.int32, (2 * tq, tq), 1)
    qloc = jnp.where(ri >= tq, ri - tq, ri)
    bias = jnp.where(ki < ((qloc >> CHUNK_SHIFT) + 1) * CHUNK, 0.0, NEG)
    dstart = pl.multiple_of(qi * tq, tq)

    for hd in range(H_B):
        cols = slice(hd * HB_W, (hd + 1) * HB_W)
        qh = q_scr[:, cols]
        s = _dot_nt(qh, k_ref[0, pl.ds(dstart, tq), cols]) + bias
        m0 = jnp.max(s, axis=-1, keepdims=True)
        p = jnp.exp(s - m0)
        l0 = jnp.sum(p, axis=-1, keepdims=True)
        acc_scr[...] = _dot(p.astype(BF), v_ref[0, pl.ds(dstart, tq), cols])

        def body(kb, carry, cols=cols):
            m_prev, l_prev = carry
            rows = pl.ds(pl.multiple_of(kb * tq, tq), tq)
            s = _dot_nt(q_scr[:, cols], k_ref[0, rows, cols])
            m_new = jnp.maximum(m_prev, jnp.max(s, axis=-1, keepdims=True))
            alpha = jnp.exp(m_prev - m_new)
            p = jnp.exp(s - m_new)
            acc_scr[...] = alpha * acc_scr[...] + _dot(p.astype(BF), v_ref[0, rows, cols])
            return m_new, alpha * l_prev + jnp.sum(p, axis=-1, keepdims=True)

        _, l = lax.fori_loop(0, qi, body, (m0, l0))
        oh_scr[:, cols] = _finish_head(acc_scr[...], l, lam, gsub_ref[...], lam_init, tq).astype(BF)

    y = _dot(oh_scr[...], wout_ref[...])
    o_ref[0] = x + gate * (_rmsn(y) * g_ref[3:4, :])


def _diff_prompt(x, mod, mod_row0, g, w_q, kb, vb, cos_t, sa_t, sb_t, lam_p, g_sub, w_out, lam_init, *, tq):
    b, s, d = x.shape
    kern = functools.partial(_diffp_kernel, tq=tq, lam_init=lam_init)
    tspec = pl.BlockSpec((tq, LANES), lambda i, j: (j, 0))
    return pl.pallas_call(
        kern,
        grid=(b, s // tq),
        in_specs=[
            pl.BlockSpec((1, tq, d), lambda i, j: (i, j, 0)),
            pl.BlockSpec((1, N_MOD, d), lambda i, j: (mod_row0 + i, 0, 0)),
            pl.BlockSpec((6, d), lambda i, j: (0, 0)),
            _resident((d, d), lambda i, j: (0, 0)),
            pl.BlockSpec((1, s, d), lambda i, j: (i, 0, 0)),
            pl.BlockSpec((1, s, d), lambda i, j: (i, 0, 0)),
            tspec, tspec, tspec,
            pl.BlockSpec((4, DH_B), lambda i, j: (0, 0)),
            pl.BlockSpec((1, HB_W), lambda i, j: (0, 0)),
            _resident((d, d), lambda i, j: (0, 0)),
        ],
        out_specs=pl.BlockSpec((1, tq, d), lambda i, j: (i, j, 0)),
        out_shape=jax.ShapeDtypeStruct(x.shape, F32),
        scratch_shapes=[
            pltpu.VMEM((2 * tq, d), BF),
            pltpu.VMEM((2 * tq, HB_W), F32),
            pltpu.VMEM((tq, d), BF),
        ],
        compiler_params=_params(2),
        name="diff_prompt",
    )(x, mod, g, w_q, kb, vb, cos_t, sa_t, sb_t, lam_p, g_sub.reshape(1, HB_W), w_out)


def _diffs_kernel(x_ref, mod_ref, g_ref, wq_ref, kn_ref, vn_ref, kc_ref, vc_ref, cos_ref, sa_ref, sb_ref,
                  lam_ref, gsub_ref, wout_ref, o_ref, q_scr, m_scr, l_scr, acc_scr, oh_scr, *, tq, lam_init):
    kc = pl.program_id(1)
    rows2 = 2 * tq

    @pl.when(kc == 0)
    def _():
        x = x_ref[0]
        h = _rmsn(x) * g_ref[2:3, :] * (1.0 + mod_ref[0, 4:5, :]) + mod_ref[0, 3:4, :]
        _project_q(h.astype(BF), wq_ref, cos_ref[...], sa_ref[...], sb_ref[...], q_scr, tq)
        for hd in range(H_B):
            cols = slice(hd * HB_W, (hd + 1) * HB_W)
            s = _dot_nt(q_scr[:, cols], kn_ref[0, :, cols])
            m0 = jnp.max(s, axis=-1, keepdims=True)
            p = jnp.exp(s - m0)
            m_scr[hd] = jnp.broadcast_to(m0, (rows2, LANES))
            l_scr[hd] = jnp.broadcast_to(jnp.sum(p, axis=-1, keepdims=True), (rows2, LANES))
            acc_scr[hd] = _dot(p.astype(BF), vn_ref[0, :, cols])

    for hd in range(H_B):
        cols = slice(hd * HB_W, (hd + 1) * HB_W)
        s = _dot_nt(q_scr[:, cols], kc_ref[0, :, cols].astype(BF))
        m_prev = m_scr[hd][:, 0:1]
        m_new = jnp.maximum(m_prev, jnp.max(s, axis=-1, keepdims=True))
        alpha = jnp.exp(m_prev - m_new)
        p = jnp.exp(s - m_new)
        l_scr[hd] = jnp.broadcast_to(alpha * l_scr[hd][:, 0:1] + jnp.sum(p, axis=-1, keepdims=True),
                                     (rows2, LANES))
        acc_scr[hd] = alpha * acc_scr[hd] + _dot(p.astype(BF), vc_ref[0, :, cols].astype(BF))
        m_scr[hd] = jnp.broadcast_to(m_new, (rows2, LANES))

    @pl.when(kc == pl.num_programs(1) - 1)
    def _():
        lam = _lambda(lam_ref, lam_init)
        for hd in range(H_B):
            cols = slice(hd * HB_W, (hd + 1) * HB_W)
            oh_scr[:, cols] = _finish_head(acc_scr[hd], l_scr[hd][:, 0:1], lam, gsub_ref[...],
                                           lam_init, tq).astype(BF)
        y = _dot(oh_scr[...], wout_ref[...])
        o_ref[0] = x_ref[0] + mod_ref[0, 5:6, :] * (_rmsn(y) * g_ref[3:4, :])


def _diff_sample(x, mod, mod_row0, g, w_q, kn, vn, cache_k, cache_v, cos_t, sa_t, sb_t, lam_p, g_sub, w_out,
                 lam_init, *, tkc):
    b, tq, d = x.shape
    past = cache_k.shape[1]
    kern = functools.partial(_diffs_kernel, tq=tq, lam_init=lam_init)
    tspec = pl.BlockSpec((tq, LANES), lambda i, j: (0, 0))
    xspec = pl.BlockSpec((1, tq, d), lambda i, j: (i, 0, 0))
    cspec = pl.BlockSpec((1, tkc, d), lambda i, j: (i, j, 0))
    return pl.pallas_call(
        kern,
        grid=(b, past // tkc),
        in_specs=[
            xspec,
            pl.BlockSpec((1, N_MOD, d), lambda i, j: (mod_row0 + i, 0, 0)),
            pl.BlockSpec((6, d), lambda i, j: (0, 0)),
            _resident((d, d), lambda i, j: (0, 0)),
            xspec, xspec, cspec, cspec,
            tspec, tspec, tspec,
            pl.BlockSpec((4, DH_B), lambda i, j: (0, 0)),
            pl.BlockSpec((1, HB_W), lambda i, j: (0, 0)),
            _resident((d, d), lambda i, j: (0, 0)),
        ],
        out_specs=xspec,
        out_shape=jax.ShapeDtypeStruct(x.shape, F32),
        scratch_shapes=[
            pltpu.VMEM((2 * tq, d), BF),
            pltpu.VMEM((H_B, 2 * tq, LANES), F32),
            pltpu.VMEM((H_B, 2 * tq, LANES), F32),
            pltpu.VMEM((H_B, 2 * tq, HB_W), F32),
            pltpu.VMEM((tq, d), BF),
        ],
        compiler_params=_params(2),
        name="diff_sample",
    )(x, mod, g, w_q, kn, vn, cache_k, cache_v, cos_t, sa_t, sb_t, lam_p, g_sub.reshape(1, HB_W), w_out)


def _rope_tables_a(pos):
    inv = jnp.power(ROPE_THETA, -jnp.arange(0, DK_A, 2, dtype=jnp.float32) / DK_A)
    ang = pos[:, None] * inv[None, :]
    cos, sin = jnp.cos(ang), jnp.sin(ang)
    return jnp.concatenate([cos, cos], axis=-1), jnp.concatenate([-sin, sin], axis=-1)


def _rope_tables_b(pos):
    inv = jnp.power(ROPE_THETA, -jnp.arange(0, DH_B, 2, dtype=jnp.float32) / DH_B)
    ang = pos[:, None] * inv[None, :]
    cos, sin = jnp.cos(ang), jnp.sin(ang)
    zero = jnp.zeros_like(sin)
    cos_t = jnp.concatenate([cos, cos, cos, cos], axis=-1)
    sa = jnp.concatenate([zero, sin, zero, sin], axis=-1)
    sb = jnp.concatenate([-sin, zero, -sin, zero], axis=-1)
    return cos_t, sa, sb


def kernel(x_prompt, x_sample, state_ret, cache_k, cache_v, c_prompt, c_sample, w_ada, b_ada, g_norm, w_ffn_in,
           w_ffn_out, w_in_a, g_gn_a, w_out_a, w_ada_kv, b_ada_kv, g_kv, w_kv, w_q_b, lam_b, g_subln_b, w_out_b):
    d = D_MODEL
    bp, sp, _ = x_prompt.shape
    bs, ss, _ = x_sample.shape

    n_rows = -(-(bs + bp) // 16) * 16
    c_all = jnp.concatenate([c_sample, c_prompt, jnp.zeros((n_rows - bs - bp, d), F32)], axis=0)
    mod = _ada_proj(c_all, w_ada, b_ada).reshape(2, n_rows, N_MOD, d)
    modkv = _ada_proj(c_all, w_ada_kv[None], b_ada_kv[None]).reshape(n_rows, 2, d)
    row_s, row_p = 0, bs

    wfi = w_ffn_in.astype(BF)
    wfo = w_ffn_out.astype(BF)
    wia = w_in_a.astype(BF)
    woa = w_out_a.astype(BF)
    wkv = w_kv.astype(BF)
    wqb = w_q_b.astype(BF)
    wob = w_out_b.astype(BF)

    pos_p = jnp.arange(sp, dtype=jnp.float32)
    pos_s = PAST_LEN + jnp.arange(ss, dtype=jnp.float32)
    nb_s = 16
    nb_r = 4
    ts_p = 512
    t_ret = 256
    tq = 256

    cos_ap, sin_ap = _rope_tables_a(pos_p)
    cos_as, sin_as = [jnp.tile(a, (nb_r, 1)) for a in _rope_tables_a(pos_s)]
    tab_bp = _rope_tables_b(pos_p)
    tab_bs = _rope_tables_b(pos_s)
    tab_bs_tiled = [jnp.tile(a, (nb_s, 1)) for a in tab_bs]

    def ffn_p(x, l, i):
        return _ffn(x, mod[l], row_p, g_norm[l], wfi[l, i], wfo[l, i], nb=1, ts=ts_p, m0=6 * i, g0=4 * i)

    def ffn_s(x, l, i):
        return _ffn(x, mod[l], row_s, g_norm[l], wfi[l, i], wfo[l, i], nb=nb_s, ts=ss, m0=6 * i, g0=4 * i)

    xp = ffn_p(x_prompt, 0, 0)
    xs = ffn_s(x_sample, 0, 0)
    xp, ret_p = _ret_mixer(xp, mod[0], row_p, g_norm[0], wia[0], g_gn_a[0], woa[0], cos_ap, sin_ap, None,
                           nb=1, t=t_ret)
    xs, ret_s = _ret_mixer(xs, mod[0], row_s, g_norm[0], wia[0], g_gn_a[0], woa[0], cos_as, sin_as, state_ret[0],
                           nb=nb_r, t=ss)
    xp = ffn_p(xp, 0, 1)
    xs = ffn_s(xs, 0, 1)
    k_p, v_p, kb_p, vb_p = _shared_kv(xp, modkv, row_p, g_kv, wkv, *tab_bp, nb=1, ts=ts_p)
    k_s, v_s, kb_s, vb_s = _shared_kv(xs, modkv, row_s, g_kv, wkv, *tab_bs_tiled, nb=nb_s, ts=ss)

    lam_init = 0.8 - 0.6 * math.exp(-0.3 * 1)
    xp = ffn_p(xp, 1, 0)
    xs = ffn_s(xs, 1, 0)
    xp = _diff_prompt(xp, mod[1], row_p, g_norm[1], wqb[0], kb_p, vb_p, *tab_bp, lam_b[0], g_subln_b[0], wob[0],
                      lam_init, tq=tq)
    xs = _diff_sample(xs, mod[1], row_s, g_norm[1], wqb[0], kb_s, vb_s,
                      cache_k.reshape(bs, PAST_LEN, d), cache_v.reshape(bs, PAST_LEN, d),
                      *tab_bs, lam_b[0], g_subln_b[0], wob[0], lam_init, tkc=1024)
    xp = ffn_p(xp, 1, 1)
    xs = ffn_s(xs, 1, 1)

    return (xp, xs,
            ret_p[None], k_p.reshape(bp, sp, 2 * H_B, DH_B), v_p.reshape(bp, sp, H_B, 2 * DH_B),
            ret_s[None], k_s.reshape(bs, ss, 2 * H_B, DH_B), v_s.reshape(bs, ss, H_B, 2 * DH_B))
```

```python
import functools
import math

import jax
import jax.numpy as jnp
from jax import lax
from jax.experimental import pallas as pl
from jax.experimental.pallas import tpu as pltpu

D_MODEL = 1024
CHUNK = 64
CHUNK_SHIFT = 6
H_A = 8
DK_A = 128
DV_A = 256
QK_A = H_A * DK_A
V_A = H_A * DV_A
H_B = 8
DH_B = 64
HB_W = 2 * DH_B
D_FF = 2816
N_MOD = 9
ROPE_THETA = 10000.0
EPS = 1e-6
HALF = 0.5
PAST_LEN = 4096
NEG = -1e30

LANES = 128
VMEM_LIMIT = 56 * 1024 * 1024

BF = jnp.bfloat16
F32 = jnp.float32

LOG_GAMMA = [math.log1p(-(2.0 ** (-5 - h))) for h in range(H_A)]


def _dot(a, b):
    return jnp.dot(a, b, preferred_element_type=F32)


def _dot_nt(a, b):
    return lax.dot_general(a, b, (((1,), (1,)), ((), ())), preferred_element_type=F32)


def _dot_tn(a, b):
    return lax.dot_general(a, b, (((0,), (0,)), ((), ())), preferred_element_type=F32)


def _rmsn(x):
    return x * lax.rsqrt(jnp.mean(x * x, axis=-1, keepdims=True) + EPS)


def _silu(x):
    return x * jax.nn.sigmoid(x)


def _resident(shape, index_map):
    return pl.BlockSpec(shape, index_map, pipeline_mode=pl.Buffered(1))


def _params(n_axes):
    return pltpu.CompilerParams(dimension_semantics=("arbitrary",) * n_axes,
                                vmem_limit_bytes=VMEM_LIMIT)


def _ada_kernel(c_ref, w_ref, b_ref, o_ref):
    sc = _silu(c_ref[...]).astype(BF)
    o_ref[...] = _dot(sc, w_ref[...].astype(BF)) + b_ref[...]


def _ada_proj(c_all, w, b, tn=1024):
    nl, d, n = w.shape
    r = c_all.shape[0]
    return pl.pallas_call(
        _ada_kernel,
        grid=(nl, n // tn),
        in_specs=[
            pl.BlockSpec((r, d), lambda l, j: (0, 0)),
            pl.BlockSpec((None, d, tn), lambda l, j: (l, 0, j)),
            pl.BlockSpec((None, 1, tn), lambda l, j: (l, 0, j)),
        ],
        out_specs=pl.BlockSpec((None, r, tn), lambda l, j: (l, 0, j)),
        out_shape=jax.ShapeDtypeStruct((nl, r, n), F32),
        compiler_params=_params(2),
        name="ada_proj",
    )(c_all, w, b.reshape(nl, 1, n))


def _ffn_kernel(x_ref, mod_ref, g_ref, win_ref, wout_ref, o_ref, *, m0, g0, fc):
    nb, ts, d = x_ref.shape
    x = x_ref[...]
    shift = mod_ref[:, m0:m0 + 1, :]
    scale = mod_ref[:, m0 + 1:m0 + 2, :]
    gate = mod_ref[:, m0 + 2:m0 + 3, :]
    h = _rmsn(x) * g_ref[g0:g0 + 1, :] * (1.0 + scale) + shift
    hb = h.reshape(nb * ts, d).astype(BF)
    acc = jnp.zeros((nb * ts, d), F32)
    for c in range(D_FF // fc):
        a = _dot(hb, win_ref[:, c * fc:(c + 1) * fc])
        b = _dot(hb, win_ref[:, D_FF + c * fc:D_FF + (c + 1) * fc])
        u = (_silu(a) * b).astype(BF)
        acc = acc + _dot(u, wout_ref[c * fc:(c + 1) * fc, :])
    y = _rmsn(acc) * g_ref[g0 + 1:g0 + 2, :]
    o_ref[...] = x + (HALF * gate) * y.reshape(nb, ts, d)


def _ffn(x, mod, mod_row0, g, w_in, w_out, *, nb, ts, m0, g0):
    b, s, d = x.shape
    nbt, nst = b // nb, s // ts
    mrow = mod_row0 // nb
    kern = functools.partial(_ffn_kernel, m0=m0, g0=g0, fc=256)
    return pl.pallas_call(
        kern,
        grid=(nbt, nst),
        in_specs=[
            pl.BlockSpec((nb, ts, d), lambda i, j: (i, j, 0)),
            pl.BlockSpec((nb, N_MOD, d), lambda i, j: (mrow + i, 0, 0)),
            pl.BlockSpec((6, d), lambda i, j: (0, 0)),
            _resident((d, 2 * D_FF), lambda i, j: (0, 0)),
            _resident((D_FF, d), lambda i, j: (0, 0)),
        ],
        out_specs=pl.BlockSpec((nb, ts, d), lambda i, j: (i, j, 0)),
        out_shape=jax.ShapeDtypeStruct(x.shape, F32),
        compiler_params=_params(2),
        name="ffn",
    )(x, mod, g, w_in, w_out)


def _decay_mask(lg, t):
    i = lax.broadcasted_iota(jnp.int32, (t, t), 0)
    j = lax.broadcasted_iota(jnp.int32, (t, t), 1)
    dist = jnp.abs(i - j).astype(F32)
    vis = (j >> CHUNK_SHIFT) <= (i >> CHUNK_SHIFT)
    return jnp.where(vis, jnp.exp(lg * dist), 0.0)


def _ret_kernel(*refs, nb, t, from_input_state):
    if from_input_state:
        (x_ref, mod_ref, g_ref, win_ref, ggn_ref, wout_ref, cos_ref, sin_ref, sin_st_ref,
         o_ref, so_ref, q_scr, k_scr, v_scr, og_scr, gat_scr, mask_scr, st_scr) = refs
    else:
        (x_ref, mod_ref, g_ref, win_ref, ggn_ref, wout_ref, cos_ref, sin_ref,
         o_ref, so_ref, q_scr, k_scr, v_scr, og_scr, gat_scr, mask_scr, st_scr) = refs
        sin_st_ref = None
    d = D_MODEL
    r = nb * t
    j = pl.program_id(1)

    @pl.when(j == 0)
    def _():
        for hd in range(H_A):
            mask_scr[hd] = _decay_mask(LOG_GAMMA[hd], t)
        if not from_input_state:
            st_scr[...] = jnp.zeros_like(st_scr)

    x = x_ref[...]
    shift = mod_ref[:, 3:4, :]
    scale = mod_ref[:, 4:5, :]
    gate = mod_ref[:, 5:6, :]
    h = _rmsn(x) * g_ref[2:3, :] * (1.0 + scale) + shift
    hb = h.reshape(r, d).astype(BF)

    cos = cos_ref[...]
    sin = sin_ref[...]
    qk = _dot(hb, win_ref[:, 0:2 * QK_A])
    for c in range(H_A):
        qc = qk[:, c * DK_A:(c + 1) * DK_A]
        q_scr[:, c * DK_A:(c + 1) * DK_A] = qc * cos + pltpu.roll(qc, DK_A // 2, 1) * sin
        kc = qk[:, QK_A + c * DK_A:QK_A + (c + 1) * DK_A]
        k_scr[:, c * DK_A:(c + 1) * DK_A] = (kc * cos + pltpu.roll(kc, DK_A // 2, 1) * sin) * (DK_A ** -0.5)
    v_scr[...] = _dot(hb, win_ref[:, 2 * QK_A:2 * QK_A + V_A]).astype(BF)
    gp = _dot(hb, win_ref[:, 2 * QK_A + V_A:2 * QK_A + 2 * V_A])

    row = lax.broadcasted_iota(jnp.int32, (t, DK_A), 0).astype(F32)
    for hd in range(H_A):
        lg = LOG_GAMMA[hd]
        qdec = jnp.exp(lg * row)
        kdec = jnp.exp(lg * (float(t) - row))
        cdec = math.exp(lg * t)
        mask = mask_scr[hd]

        def one_batch(n, carry, hd=hd, qdec=qdec, kdec=kdec, cdec=cdec, mask=mask):
            rows = pl.ds(0, t) if nb == 1 else pl.ds(pl.multiple_of(n * t, t), t)
            q = q_scr[rows, hd * DK_A:(hd + 1) * DK_A]
            k = k_scr[rows, hd * DK_A:(hd + 1) * DK_A]
            v = v_scr[rows, hd * DV_A:(hd + 1) * DV_A]
            if from_input_state:
                st = sin_st_ref[n, hd]
            else:
                st = st_scr[hd]
            s = _dot_nt(q.astype(BF), k.astype(BF)) * mask
            o = _dot(s.astype(BF), v) + _dot((q * qdec).astype(BF), st.astype(BF))
            st_new = st * cdec + _dot_tn((k * kdec).astype(BF), v)
            og_scr[rows, hd * DV_A:(hd + 1) * DV_A] = o
            if from_input_state:
                so_ref[n, hd] = st_new
            else:
                st_scr[hd] = st_new
            return carry

        if nb == 1:
            one_batch(0, 0)
        else:
            lax.fori_loop(0, nb, one_batch, 0)

    for hd in range(H_A):
        cols = slice(hd * DV_A, (hd + 1) * DV_A)
        on = _rmsn(og_scr[:, cols]) * ggn_ref[:, cols]
        gat_scr[:, cols] = (_silu(gp[:, cols]) * on).astype(BF)
    y = _dot(gat_scr[...], wout_ref[...])
    yn = _rmsn(y) * g_ref[3:4, :]
    o_ref[...] = x + gate * yn.reshape(nb, t, d)

    if not from_input_state:
        @pl.when(j == pl.num_programs(1) - 1)
        def _():
            so_ref[0] = st_scr[...]


def _ret_mixer(x, mod, mod_row0, g, w_in, g_gn, w_out, cos_t, sin_t, state_in, *, nb, t):
    b, s, d = x.shape
    nbt, nst = b // nb, s // t
    r = nb * t
    mrow = mod_row0 // nb
    from_input = state_in is not None
    kern = functools.partial(_ret_kernel, nb=nb, t=t, from_input_state=from_input)
    in_specs = [
        pl.BlockSpec((nb, t, d), lambda i, j: (i, j, 0)),
        pl.BlockSpec((nb, N_MOD, d), lambda i, j: (mrow + i, 0, 0)),
        pl.BlockSpec((6, d), lambda i, j: (0, 0)),
        _resident((d, 2 * QK_A + 2 * V_A), lambda i, j: (0, 0)),
        pl.BlockSpec((1, V_A), lambda i, j: (0, 0)),
        _resident((V_A, d), lambda i, j: (0, 0)),
        pl.BlockSpec((r, DK_A), lambda i, j: (j, 0)),
        pl.BlockSpec((r, DK_A), lambda i, j: (j, 0)),
    ]
    args = [x, mod, g, w_in, g_gn.reshape(1, V_A), w_out, cos_t, sin_t]
    if from_input:
        in_specs.append(pl.BlockSpec((nb, H_A, DK_A, DV_A), lambda i, j: (i, 0, 0, 0)))
        args.append(state_in)
    out, st = pl.pallas_call(
        kern,
        grid=(nbt, nst),
        in_specs=in_specs,
        out_specs=[
            pl.BlockSpec((nb, t, d), lambda i, j: (i, j, 0)),
            pl.BlockSpec((nb, H_A, DK_A, DV_A), lambda i, j: (i, 0, 0, 0)),
        ],
        out_shape=[
            jax.ShapeDtypeStruct(x.shape, F32),
            jax.ShapeDtypeStruct((b, H_A, DK_A, DV_A), F32),
        ],
        scratch_shapes=[
            pltpu.VMEM((r, QK_A), F32),
            pltpu.VMEM((r, QK_A), F32),
            pltpu.VMEM((r, V_A), BF),
            pltpu.VMEM((r, V_A), F32),
            pltpu.VMEM((r, V_A), BF),
            pltpu.VMEM((H_A, t, t), F32),
            pltpu.VMEM((H_A, DK_A, DV_A), F32),
        ],
        compiler_params=_params(2),
        name="ret_mixer",
    )(*args)
    return out, st


def _rope64(blk, cos, sa, sb):
    return blk * cos + pltpu.roll(blk, DH_B // 2, 1) * sa + pltpu.roll(blk, LANES - DH_B // 2, 1) * sb


def _kv_kernel(x_ref, mod_ref, g_ref, w_ref, cos_ref, sa_ref, sb_ref, k_ref, v_ref, kb_ref, vb_ref):
    nb, ts, d = x_ref.shape
    x = x_ref[...]
    shift = mod_ref[:, 0:1, :]
    scale = mod_ref[:, 1:2, :]
    h = _rmsn(x) * g_ref[...] * (1.0 + scale) + shift
    hb = h.reshape(nb * ts, d).astype(BF)
    kv = _dot(hb, w_ref[...])
    cos = cos_ref[...]
    sa = sa_ref[...]
    sb = sb_ref[...]
    ks = [_rope64(kv[:, c * LANES:(c + 1) * LANES], cos, sa, sb) for c in range(d // LANES)]
    k = jnp.concatenate(ks, axis=1).reshape(nb, ts, d)
    v = kv[:, d:2 * d].reshape(nb, ts, d)
    k_ref[...] = k
    v_ref[...] = v
    kb_ref[...] = k.astype(BF)
    vb_ref[...] = v.astype(BF)


def _shared_kv(x, modkv, mod_row0, g_kv, w_kv, cos_t, sa_t, sb_t, *, nb, ts):
    b, s, d = x.shape
    nbt, nst = b // nb, s // ts
    r = nb * ts
    mrow = mod_row0 // nb
    xspec = pl.BlockSpec((nb, ts, d), lambda i, j: (i, j, 0))
    tspec = pl.BlockSpec((r, LANES), lambda i, j: (j, 0))
    return pl.pallas_call(
        _kv_kernel,
        grid=(nbt, nst),
        in_specs=[
            xspec,
            pl.BlockSpec((nb, 2, d), lambda i, j: (mrow + i, 0, 0)),
            pl.BlockSpec((1, d), lambda i, j: (0, 0)),
            _resident((d, 2 * d), lambda i, j: (0, 0)),
            tspec, tspec, tspec,
        ],
        out_specs=[xspec, xspec, xspec, xspec],
        out_shape=[
            jax.ShapeDtypeStruct(x.shape, F32),
            jax.ShapeDtypeStruct(x.shape, F32),
            jax.ShapeDtypeStruct(x.shape, BF),
            jax.ShapeDtypeStruct(x.shape, BF),
        ],
        compiler_params=_params(2),
        name="shared_kv",
    )(x, modkv, g_kv.reshape(1, d), w_kv, cos_t, sa_t, sb_t)


def _lambda(lam_ref, lam_init):
    lp = lam_ref[...]
    l1 = jnp.sum(lp[0:1, :] * lp[1:2, :], axis=-1, keepdims=True)
    l2 = jnp.sum(lp[2:3, :] * lp[3:4, :], axis=-1, keepdims=True)
    return jnp.exp(l1) - jnp.exp(l2) + lam_init


def _project_q(hb, wq_ref, cos, sa, sb, q_scr, tq):
    q = _dot(hb, wq_ref[...])
    lane = lax.broadcasted_iota(jnp.int32, (tq, LANES), 1)
    first = lane < DH_B
    for c in range(H_B):
        rq = _rope64(q[:, c * LANES:(c + 1) * LANES], cos, sa, sb) * (DH_B ** -0.5)
        q_scr[0:tq, c * LANES:(c + 1) * LANES] = jnp.where(first, rq, 0.0).astype(BF)
        q_scr[tq:2 * tq, c * LANES:(c + 1) * LANES] = jnp.where(first, 0.0, rq).astype(BF)


def _finish_head(acc, l, lam, gsub, lam_init, tq):
    o = acc[0:tq] / l[0:tq] - lam * (acc[tq:2 * tq] / l[tq:2 * tq])
    return _rmsn(o) * gsub * (1.0 - lam_init)


def _diffp_kernel(x_ref, mod_ref, g_ref, wq_ref, k_ref, v_ref, cos_ref, sa_ref, sb_ref, lam_ref,
                  gsub_ref, wout_ref, o_ref, q_scr, acc_scr, oh_scr, *, tq, lam_init):
    qi = pl.program_id(1)
    x = x_ref[0]
    shift = mod_ref[0, 3:4, :]
    scale = mod_ref[0, 4:5, :]
    gate = mod_ref[0, 5:6, :]
    h = _rmsn(x) * g_ref[2:3, :] * (1.0 + scale) + shift
    _project_q(h.astype(BF), wq_ref, cos_ref[...], sa_ref[...], sb_ref[...], q_scr, tq)
    lam = _lambda(lam_ref, lam_init)

    ri = lax.broadcasted_iota(jnp.int32, (2 * tq, tq), 0)
    ki = lax.broadcasted_iota(jnp.int32, (2 * tq, tq), 1)
    qloc = jnp.where(ri >= tq, ri - tq, ri)
    bias = jnp.where(ki < ((qloc >> CHUNK_SHIFT) + 1) * CHUNK, 0.0, NEG)
    dstart = pl.multiple_of(qi * tq, tq)

    for hd in range(H_B):
        cols = slice(hd * HB_W, (hd + 1) * HB_W)
        qh = q_scr[:, cols]
        s = _dot_nt(qh, k_ref[0, pl.ds(dstart, tq), cols]) + bias
        m0 = jnp.max(s, axis=-1, keepdims=True)
        p = jnp.exp(s - m0)
        l0 = jnp.sum(p, axis=-1, keepdims=True)
        acc_scr[...] = _dot(p.astype(BF), v_ref[0, pl.ds(dstart, tq), cols])

        def body(kb, carry, cols=cols):
            m_prev, l_prev = carry
            rows = pl.ds(pl.multiple_of(kb * tq, tq), tq)
            s = _dot_nt(q_scr[:, cols], k_ref[0, rows, cols])
            m_new = jnp.maximum(m_prev, jnp.max(s, axis=-1, keepdims=True))
            alpha = jnp.exp(m_prev - m_new)
            p = jnp.exp(s - m_new)
            acc_scr[...] = alpha * acc_scr[...] + _dot(p.astype(BF), v_ref[0, rows, cols])
            return m_new, alpha * l_prev + jnp.sum(p, axis=-1, keepdims=True)

        _, l = lax.fori_loop(0, qi, body, (m0, l0))
        oh_scr[:, cols] = _finish_head(acc_scr[...], l, lam, gsub_ref[...], lam_init, tq).astype(BF)

    y = _dot(oh_scr[...], wout_ref[...])
    o_ref[0] = x + gate * (_rmsn(y) * g_ref[3:4, :])


def _diff_prompt(x, mod, mod_row0, g, w_q, kb, vb, cos_t, sa_t, sb_t, lam_p, g_sub, w_out, lam_init, *, tq):
    b, s, d = x.shape
    kern = functools.partial(_diffp_kernel, tq=tq, lam_init=lam_init)
    tspec = pl.BlockSpec((tq, LANES), lambda i, j: (j, 0))
    return pl.pallas_call(
        kern,
        grid=(b, s // tq),
        in_specs=[
            pl.BlockSpec((1, tq, d), lambda i, j: (i, j, 0)),
            pl.BlockSpec((1, N_MOD, d), lambda i, j: (mod_row0 + i, 0, 0)),
            pl.BlockSpec((6, d), lambda i, j: (0, 0)),
            _resident((d, d), lambda i, j: (0, 0)),
            pl.BlockSpec((1, s, d), lambda i, j: (i, 0, 0)),
            pl.BlockSpec((1, s, d), lambda i, j: (i, 0, 0)),
            tspec, tspec, tspec,
            pl.BlockSpec((4, DH_B), lambda i, j: (0, 0)),
            pl.BlockSpec((1, HB_W), lambda i, j: (0, 0)),
            _resident((d, d), lambda i, j: (0, 0)),
        ],
        out_specs=pl.BlockSpec((1, tq, d), lambda i, j: (i, j, 0)),
        out_shape=jax.ShapeDtypeStruct(x.shape, F32),
        scratch_shapes=[
            pltpu.VMEM((2 * tq, d), BF),
            pltpu.VMEM((2 * tq, HB_W), F32),
            pltpu.VMEM((tq, d), BF),
        ],
        compiler_params=_params(2),
        name="diff_prompt",
    )(x, mod, g, w_q, kb, vb, cos_t, sa_t, sb_t, lam_p, g_sub.reshape(1, HB_W), w_out)


def _diffs_kernel(x_ref, mod_ref, g_ref, wq_ref, kn_ref, vn_ref, kc_ref, vc_ref, cos_ref, sa_ref, sb_ref,
                  lam_ref, gsub_ref, wout_ref, o_ref, q_scr, m_scr, l_scr, acc_scr, oh_scr, *, tq, lam_init):
    kc = pl.program_id(1)
    rows2 = 2 * tq

    @pl.when(kc == 0)
    def _():
        x = x_ref[0]
        h = _rmsn(x) * g_ref[2:3, :] * (1.0 + mod_ref[0, 4:5, :]) + mod_ref[0, 3:4, :]
        _project_q(h.astype(BF), wq_ref, cos_ref[...], sa_ref[...], sb_ref[...], q_scr, tq)
        for hd in range(H_B):
            cols = slice(hd * HB_W, (hd + 1) * HB_W)
            s = _dot_nt(q_scr[:, cols], kn_ref[0, :, cols])
            m0 = jnp.max(s, axis=-1, keepdims=True)
            p = jnp.exp(s - m0)
            m_scr[hd] = jnp.broadcast_to(m0, (rows2, LANES))
            l_scr[hd] = jnp.broadcast_to(jnp.sum(p, axis=-1, keepdims=True), (rows2, LANES))
            acc_scr[hd] = _dot(p.astype(BF), vn_ref[0, :, cols])

    tkc = kc_ref.shape[3]
    for hd in range(H_B):
        s = jnp.concatenate(
            [_dot(q_scr[t * tq:(t + 1) * tq, (2 * hd + t) * DH_B:(2 * hd + t + 1) * DH_B],
                  kc_ref[0, 2 * hd + t].astype(BF)) for t in range(2)], axis=0)
        v_h = vc_ref[0, pl.ds(hd, tkc, stride=H_B), :].astype(BF)
        m_prev = m_scr[hd][:, 0:1]
        m_new = jnp.maximum(m_prev, jnp.max(s, axis=-1, keepdims=True))
        alpha = jnp.exp(m_prev - m_new)
        p = jnp.exp(s - m_new)
        l_scr[hd] = jnp.broadcast_to(alpha * l_scr[hd][:, 0:1] + jnp.sum(p, axis=-1, keepdims=True),
                                     (rows2, LANES))
        acc_scr[hd] = alpha * acc_scr[hd] + _dot(p.astype(BF), v_h)
        m_scr[hd] = jnp.broadcast_to(m_new, (rows2, LANES))

    @pl.when(kc == pl.num_programs(1) - 1)
    def _():
        lam = _lambda(lam_ref, lam_init)
        for hd in range(H_B):
            cols = slice(hd * HB_W, (hd + 1) * HB_W)
            oh_scr[:, cols] = _finish_head(acc_scr[hd], l_scr[hd][:, 0:1], lam, gsub_ref[...],
                                           lam_init, tq).astype(BF)
        y = _dot(oh_scr[...], wout_ref[...])
        o_ref[0] = x_ref[0] + mod_ref[0, 5:6, :] * (_rmsn(y) * g_ref[3:4, :])


def _diff_sample(x, mod, mod_row0, g, w_q, kn, vn, cache_k, cache_v, cos_t, sa_t, sb_t, lam_p, g_sub, w_out,
                 lam_init, *, tkc):
    b, tq, d = x.shape
    past = cache_k.shape[1]
    cache_kt = jnp.transpose(cache_k, (0, 2, 3, 1))
    cache_vf = cache_v.reshape(b, past * H_B, HB_W)
    kern = functools.partial(_diffs_kernel, tq=tq, lam_init=lam_init)
    tspec = pl.BlockSpec((tq, LANES), lambda i, j: (0, 0))
    xspec = pl.BlockSpec((1, tq, d), lambda i, j: (i, 0, 0))
    kspec = pl.BlockSpec((1, 2 * H_B, DH_B, tkc), lambda i, j: (i, 0, 0, j))
    vspec = pl.BlockSpec((1, tkc * H_B, HB_W), lambda i, j: (i, j, 0))
    return pl.pallas_call(
        kern,
        grid=(b, past // tkc),
        in_specs=[
            xspec,
            pl.BlockSpec((1, N_MOD, d), lambda i, j: (mod_row0 + i, 0, 0)),
            pl.BlockSpec((6, d), lambda i, j: (0, 0)),
            _resident((d, d), lambda i, j: (0, 0)),
            xspec, xspec, kspec, vspec,
            tspec, tspec, tspec,
            pl.BlockSpec((4, DH_B), lambda i, j: (0, 0)),
            pl.BlockSpec((1, HB_W), lambda i, j: (0, 0)),
            _resident((d, d), lambda i, j: (0, 0)),
        ],
        out_specs=xspec,
        out_shape=jax.ShapeDtypeStruct(x.shape, F32),
        scratch_shapes=[
            pltpu.VMEM((2 * tq, d), BF),
            pltpu.VMEM((H_B, 2 * tq, LANES), F32),
            pltpu.VMEM((H_B, 2 * tq, LANES), F32),
            pltpu.VMEM((H_B, 2 * tq, HB_W), F32),
            pltpu.VMEM((tq, d), BF),
        ],
        compiler_params=_params(2),
        name="diff_sample",
    )(x, mod, g, w_q, kn, vn, cache_kt, cache_vf, cos_t, sa_t, sb_t, lam_p, g_sub.reshape(1, HB_W), w_out)


def _rope_tables_a(pos):
    inv = jnp.power(ROPE_THETA, -jnp.arange(0, DK_A, 2, dtype=jnp.float32) / DK_A)
    ang = pos[:, None] * inv[None, :]
    cos, sin = jnp.cos(ang), jnp.sin(ang)
    return jnp.concatenate([cos, cos], axis=-1), jnp.concatenate([-sin, sin], axis=-1)


def _rope_tables_b(pos):
    inv = jnp.power(ROPE_THETA, -jnp.arange(0, DH_B, 2, dtype=jnp.float32) / DH_B)
    ang = pos[:, None] * inv[None, :]
    cos, sin = jnp.cos(ang), jnp.sin(ang)
    zero = jnp.zeros_like(sin)
    cos_t = jnp.concatenate([cos, cos, cos, cos], axis=-1)
    sa = jnp.concatenate([zero, sin, zero, sin], axis=-1)
    sb = jnp.concatenate([-sin, zero, -sin, zero], axis=-1)
    return cos_t, sa, sb


def kernel(x_prompt, x_sample, state_ret, cache_k, cache_v, c_prompt, c_sample, w_ada, b_ada, g_norm, w_ffn_in,
           w_ffn_out, w_in_a, g_gn_a, w_out_a, w_ada_kv, b_ada_kv, g_kv, w_kv, w_q_b, lam_b, g_subln_b, w_out_b):
    d = D_MODEL
    bp, sp, _ = x_prompt.shape
    bs, ss, _ = x_sample.shape

    n_rows = -(-(bs + bp) // 16) * 16
    c_all = jnp.concatenate([c_sample, c_prompt, jnp.zeros((n_rows - bs - bp, d), F32)], axis=0)
    mod = _ada_proj(c_all, w_ada, b_ada).reshape(2, n_rows, N_MOD, d)
    modkv = _ada_proj(c_all, w_ada_kv[None], b_ada_kv[None]).reshape(n_rows, 2, d)
    row_s, row_p = 0, bs

    wfi = w_ffn_in.astype(BF)
    wfo = w_ffn_out.astype(BF)
    wia = w_in_a.astype(BF)
    woa = w_out_a.astype(BF)
    wkv = w_kv.astype(BF)
    wqb = w_q_b.astype(BF)
    wob = w_out_b.astype(BF)

    pos_p = jnp.arange(sp, dtype=jnp.float32)
    pos_s = PAST_LEN + jnp.arange(ss, dtype=jnp.float32)
    nb_s = 16
    nb_r = 4
    ts_p = 512
    t_ret = 256
    tq = 256

    cos_ap, sin_ap = _rope_tables_a(pos_p)
    cos_as, sin_as = [jnp.tile(a, (nb_r, 1)) for a in _rope_tables_a(pos_s)]
    tab_bp = _rope_tables_b(pos_p)
    tab_bs = _rope_tables_b(pos_s)
    tab_bs_tiled = [jnp.tile(a, (nb_s, 1)) for a in tab_bs]

    def ffn_p(x, l, i):
        return _ffn(x, mod[l], row_p, g_norm[l], wfi[l, i], wfo[l, i], nb=1, ts=ts_p, m0=6 * i, g0=4 * i)

    def ffn_s(x, l, i):
        return _ffn(x, mod[l], row_s, g_norm[l], wfi[l, i], wfo[l, i], nb=nb_s, ts=ss, m0=6 * i, g0=4 * i)

    xp = ffn_p(x_prompt, 0, 0)
    xs = ffn_s(x_sample, 0, 0)
    xp, ret_p = _ret_mixer(xp, mod[0], row_p, g_norm[0], wia[0], g_gn_a[0], woa[0], cos_ap, sin_ap, None,
                           nb=1, t=t_ret)
    xs, ret_s = _ret_mixer(xs, mod[0], row_s, g_norm[0], wia[0], g_gn_a[0], woa[0], cos_as, sin_as, state_ret[0],
                           nb=nb_r, t=ss)
    xp = ffn_p(xp, 0, 1)
    xs = ffn_s(xs, 0, 1)
    k_p, v_p, kb_p, vb_p = _shared_kv(xp, modkv, row_p, g_kv, wkv, *tab_bp, nb=1, ts=ts_p)
    k_s, v_s, kb_s, vb_s = _shared_kv(xs, modkv, row_s, g_kv, wkv, *tab_bs_tiled, nb=nb_s, ts=ss)

    lam_init = 0.8 - 0.6 * math.exp(-0.3 * 1)
    xp = ffn_p(xp, 1, 0)
    xs = ffn_s(xs, 1, 0)
    xp = _diff_prompt(xp, mod[1], row_p, g_norm[1], wqb[0], kb_p, vb_p, *tab_bp, lam_b[0], g_subln_b[0], wob[0],
                      lam_init, tq=tq)
    xs = _diff_sample(xs, mod[1], row_s, g_norm[1], wqb[0], kb_s, vb_s,
                      cache_k, cache_v, *tab_bs, lam_b[0], g_subln_b[0], wob[0], lam_init, tkc=1024)
    xp = ffn_p(xp, 1, 1)
    xs = ffn_s(xs, 1, 1)

    return (xp, xs,
            ret_p[None], k_p.reshape(bp, sp, 2 * H_B, DH_B), v_p.reshape(bp, sp, H_B, 2 * DH_B),
            ret_s[None], k_s.reshape(bs, ss, 2 * H_B, DH_B), v_s.reshape(bs, ss, H_B, 2 * DH_B))
```

```python
import functools
import math

import jax
import jax.numpy as jnp
from jax import lax
from jax.experimental import pallas as pl
from jax.experimental.pallas import tpu as pltpu

D_MODEL = 1024
CHUNK = 64
CHUNK_SHIFT = 6
H_A = 8
DK_A = 128
DV_A = 256
QK_A = H_A * DK_A
V_A = H_A * DV_A
H_B = 8
DH_B = 64
HB_W = 2 * DH_B
D_FF = 2816
N_MOD = 9
ROPE_THETA = 10000.0
EPS = 1e-6
HALF = 0.5
PAST_LEN = 4096
NEG = -1e30
LOG2E = 1.4426950408889634
Q_SCALE = DH_B ** -0.5 * LOG2E
VT_ROWS = HB_W + 16

LANES = 128
VMEM_LIMIT = 56 * 1024 * 1024

BF = jnp.bfloat16
F32 = jnp.float32

LOG_GAMMA = [math.log1p(-(2.0 ** (-5 - h))) for h in range(H_A)]


def _dot(a, b):
    return jnp.dot(a, b, preferred_element_type=F32)


def _dot_nt(a, b):
    return lax.dot_general(a, b, (((1,), (1,)), ((), ())), preferred_element_type=F32)


def _dot_tn(a, b):
    return lax.dot_general(a, b, (((0,), (0,)), ((), ())), preferred_element_type=F32)


def _rmsn(x):
    return x * lax.rsqrt(jnp.mean(x * x, axis=-1, keepdims=True) + EPS)


def _silu(x):
    return x * jax.nn.sigmoid(x)


def _resident(shape, index_map):
    return pl.BlockSpec(shape, index_map, pipeline_mode=pl.Buffered(1))


def _params(n_axes):
    return pltpu.CompilerParams(dimension_semantics=("arbitrary",) * n_axes,
                                vmem_limit_bytes=VMEM_LIMIT)


def _ada_kernel(c_ref, w_ref, b_ref, o_ref):
    sc = _silu(c_ref[...]).astype(BF)
    o_ref[...] = _dot(sc, w_ref[...].astype(BF)) + b_ref[...]


def _ada_proj(c_all, w, b, tn=1024):
    nl, d, n = w.shape
    r = c_all.shape[0]
    return pl.pallas_call(
        _ada_kernel,
        grid=(nl, n // tn),
        in_specs=[
            pl.BlockSpec((r, d), lambda l, j: (0, 0)),
            pl.BlockSpec((None, d, tn), lambda l, j: (l, 0, j)),
            pl.BlockSpec((None, 1, tn), lambda l, j: (l, 0, j)),
        ],
        out_specs=pl.BlockSpec((None, r, tn), lambda l, j: (l, 0, j)),
        out_shape=jax.ShapeDtypeStruct((nl, r, n), F32),
        compiler_params=_params(2),
        name="ada_proj",
    )(c_all, w, b.reshape(nl, 1, n))


def _ffn_kernel(x_ref, mod_ref, g_ref, win_ref, wout_ref, o_ref, *, m0, g0, fc):
    nb, ts, d = x_ref.shape
    x = x_ref[...]
    shift = mod_ref[:, m0:m0 + 1, :]
    scale = mod_ref[:, m0 + 1:m0 + 2, :]
    gate = mod_ref[:, m0 + 2:m0 + 3, :]
    h = _rmsn(x) * g_ref[g0:g0 + 1, :] * (1.0 + scale) + shift
    hb = h.reshape(nb * ts, d).astype(BF)
    acc = jnp.zeros((nb * ts, d), F32)
    for c in range(D_FF // fc):
        a = _dot(hb, win_ref[:, c * fc:(c + 1) * fc])
        b = _dot(hb, win_ref[:, D_FF + c * fc:D_FF + (c + 1) * fc])
        u = (_silu(a) * b).astype(BF)
        acc = acc + _dot(u, wout_ref[c * fc:(c + 1) * fc, :])
    y = _rmsn(acc) * g_ref[g0 + 1:g0 + 2, :]
    o_ref[...] = x + (HALF * gate) * y.reshape(nb, ts, d)


def _ffn(x, mod, mod_row0, g, w_in, w_out, *, nb, ts, m0, g0):
    b, s, d = x.shape
    nbt, nst = b // nb, s // ts
    mrow = mod_row0 // nb
    kern = functools.partial(_ffn_kernel, m0=m0, g0=g0, fc=256)
    return pl.pallas_call(
        kern,
        grid=(nbt, nst),
        in_specs=[
            pl.BlockSpec((nb, ts, d), lambda i, j: (i, j, 0)),
            pl.BlockSpec((nb, N_MOD, d), lambda i, j: (mrow + i, 0, 0)),
            pl.BlockSpec((6, d), lambda i, j: (0, 0)),
            _resident((d, 2 * D_FF), lambda i, j: (0, 0)),
            _resident((D_FF, d), lambda i, j: (0, 0)),
        ],
        out_specs=pl.BlockSpec((nb, ts, d), lambda i, j: (i, j, 0)),
        out_shape=jax.ShapeDtypeStruct(x.shape, F32),
        compiler_params=_params(2),
        name="ffn",
    )(x, mod, g, w_in, w_out)


def _decay_mask(lg, t):
    i = lax.broadcasted_iota(jnp.int32, (t, t), 0)
    j = lax.broadcasted_iota(jnp.int32, (t, t), 1)
    dist = jnp.abs(i - j).astype(F32)
    vis = (j >> CHUNK_SHIFT) <= (i >> CHUNK_SHIFT)
    return jnp.where(vis, jnp.exp(lg * dist), 0.0)


def _ret_kernel(*refs, nb, t, from_input_state):
    if from_input_state:
        (x_ref, mod_ref, g_ref, win_ref, ggn_ref, wout_ref, cos_ref, sin_ref, sin_st_ref,
         o_ref, so_ref, q_scr, k_scr, v_scr, og_scr, gat_scr, mask_scr, st_scr) = refs
    else:
        (x_ref, mod_ref, g_ref, win_ref, ggn_ref, wout_ref, cos_ref, sin_ref,
         o_ref, so_ref, q_scr, k_scr, v_scr, og_scr, gat_scr, mask_scr, st_scr) = refs
        sin_st_ref = None
    d = D_MODEL
    r = nb * t
    j = pl.program_id(1)

    @pl.when(j == 0)
    def _():
        for hd in range(H_A):
            mask_scr[hd] = _decay_mask(LOG_GAMMA[hd], t)
        if not from_input_state:
            st_scr[...] = jnp.zeros_like(st_scr)

    x = x_ref[...]
    shift = mod_ref[:, 3:4, :]
    scale = mod_ref[:, 4:5, :]
    gate = mod_ref[:, 5:6, :]
    h = _rmsn(x) * g_ref[2:3, :] * (1.0 + scale) + shift
    hb = h.reshape(r, d).astype(BF)

    cos = cos_ref[...]
    sin = sin_ref[...]
    qk = _dot(hb, win_ref[:, 0:2 * QK_A])
    for c in range(H_A):
        qc = qk[:, c * DK_A:(c + 1) * DK_A]
        q_scr[:, c * DK_A:(c + 1) * DK_A] = qc * cos + pltpu.roll(qc, DK_A // 2, 1) * sin
        kc = qk[:, QK_A + c * DK_A:QK_A + (c + 1) * DK_A]
        k_scr[:, c * DK_A:(c + 1) * DK_A] = (kc * cos + pltpu.roll(kc, DK_A // 2, 1) * sin) * (DK_A ** -0.5)
    v_scr[...] = _dot(hb, win_ref[:, 2 * QK_A:2 * QK_A + V_A]).astype(BF)
    gp = _dot(hb, win_ref[:, 2 * QK_A + V_A:2 * QK_A + 2 * V_A])

    row = lax.broadcasted_iota(jnp.int32, (t, DK_A), 0).astype(F32)
    for hd in range(H_A):
        lg = LOG_GAMMA[hd]
        qdec = jnp.exp(lg * row)
        kdec = jnp.exp(lg * (float(t) - row))
        cdec = math.exp(lg * t)
        mask = mask_scr[hd]

        def one_batch(n, carry, hd=hd, qdec=qdec, kdec=kdec, cdec=cdec, mask=mask):
            rows = pl.ds(0, t) if nb == 1 else pl.ds(pl.multiple_of(n * t, t), t)
            q = q_scr[rows, hd * DK_A:(hd + 1) * DK_A]
            k = k_scr[rows, hd * DK_A:(hd + 1) * DK_A]
            v = v_scr[rows, hd * DV_A:(hd + 1) * DV_A]
            if from_input_state:
                st = sin_st_ref[n, hd]
            else:
                st = st_scr[hd]
            s = _dot_nt(q.astype(BF), k.astype(BF)) * mask
            o = _dot(s.astype(BF), v) + _dot((q * qdec).astype(BF), st.astype(BF))
            st_new = st * cdec + _dot_tn((k * kdec).astype(BF), v)
            og_scr[rows, hd * DV_A:(hd + 1) * DV_A] = o
            if from_input_state:
                so_ref[n, hd] = st_new
            else:
                st_scr[hd] = st_new
            return carry

        if nb == 1:
            one_batch(0, 0)
        else:
            lax.fori_loop(0, nb, one_batch, 0)

    for hd in range(H_A):
        cols = slice(hd * DV_A, (hd + 1) * DV_A)
        on = _rmsn(og_scr[:, cols]) * ggn_ref[:, cols]
        gat_scr[:, cols] = (_silu(gp[:, cols]) * on).astype(BF)
    y = _dot(gat_scr[...], wout_ref[...])
    yn = _rmsn(y) * g_ref[3:4, :]
    o_ref[...] = x + gate * yn.reshape(nb, t, d)

    if not from_input_state:
        @pl.when(j == pl.num_programs(1) - 1)
        def _():
            so_ref[0] = st_scr[...]


def _ret_mixer(x, mod, mod_row0, g, w_in, g_gn, w_out, cos_t, sin_t, state_in, *, nb, t):
    b, s, d = x.shape
    nbt, nst = b // nb, s // t
    r = nb * t
    mrow = mod_row0 // nb
    from_input = state_in is not None
    kern = functools.partial(_ret_kernel, nb=nb, t=t, from_input_state=from_input)
    in_specs = [
        pl.BlockSpec((nb, t, d), lambda i, j: (i, j, 0)),
        pl.BlockSpec((nb, N_MOD, d), lambda i, j: (mrow + i, 0, 0)),
        pl.BlockSpec((6, d), lambda i, j: (0, 0)),
        _resident((d, 2 * QK_A + 2 * V_A), lambda i, j: (0, 0)),
        pl.BlockSpec((1, V_A), lambda i, j: (0, 0)),
        _resident((V_A, d), lambda i, j: (0, 0)),
        pl.BlockSpec((r, DK_A), lambda i, j: (j, 0)),
        pl.BlockSpec((r, DK_A), lambda i, j: (j, 0)),
    ]
    args = [x, mod, g, w_in, g_gn.reshape(1, V_A), w_out, cos_t, sin_t]
    if from_input:
        in_specs.append(pl.BlockSpec((nb, H_A, DK_A, DV_A), lambda i, j: (i, 0, 0, 0)))
        args.append(state_in)
    out, st = pl.pallas_call(
        kern,
        grid=(nbt, nst),
        in_specs=in_specs,
        out_specs=[
            pl.BlockSpec((nb, t, d), lambda i, j: (i, j, 0)),
            pl.BlockSpec((nb, H_A, DK_A, DV_A), lambda i, j: (i, 0, 0, 0)),
        ],
        out_shape=[
            jax.ShapeDtypeStruct(x.shape, F32),
            jax.ShapeDtypeStruct((b, H_A, DK_A, DV_A), F32),
        ],
        scratch_shapes=[
            pltpu.VMEM((r, QK_A), F32),
            pltpu.VMEM((r, QK_A), F32),
            pltpu.VMEM((r, V_A), BF),
            pltpu.VMEM((r, V_A), F32),
            pltpu.VMEM((r, V_A), BF),
            pltpu.VMEM((H_A, t, t), F32),
            pltpu.VMEM((H_A, DK_A, DV_A), F32),
        ],
        compiler_params=_params(2),
        name="ret_mixer",
    )(*args)
    return out, st


def _rope64(blk, cos, sa, sb):
    return blk * cos + pltpu.roll(blk, DH_B // 2, 1) * sa + pltpu.roll(blk, LANES - DH_B // 2, 1) * sb


def _kv_kernel(x_ref, mod_ref, g_ref, w_ref, cos_ref, sa_ref, sb_ref, k_ref, v_ref, kb_ref, vb_ref, *, vt_block):
    nb, ts, d = x_ref.shape
    x = x_ref[...]
    shift = mod_ref[:, 0:1, :]
    scale = mod_ref[:, 1:2, :]
    h = _rmsn(x) * g_ref[...] * (1.0 + scale) + shift
    hb = h.reshape(nb * ts, d).astype(BF)
    kv = _dot(hb, w_ref[...])
    cos = cos_ref[...]
    sa = sa_ref[...]
    sb = sb_ref[...]
    ks = [_rope64(kv[:, c * LANES:(c + 1) * LANES], cos, sa, sb) for c in range(d // LANES)]
    k = jnp.concatenate(ks, axis=1).reshape(nb, ts, d)
    v = kv[:, d:2 * d].reshape(nb, ts, d)
    k_ref[...] = k
    v_ref[...] = v
    kb_ref[...] = k.astype(BF)
    if vt_block is None:
        vb_ref[...] = v.astype(BF)
    else:
        vt = kv[:, d:2 * d].T
        ones = jnp.ones((VT_ROWS - HB_W, vt_block), BF)
        for c in range(ts // vt_block):
            for hd in range(H_B):
                vb_ref[0, c, hd, 0:HB_W, :] = vt[hd * HB_W:(hd + 1) * HB_W,
                                                 c * vt_block:(c + 1) * vt_block].astype(BF)
                vb_ref[0, c, hd, HB_W:VT_ROWS, :] = ones


def _shared_kv(x, modkv, mod_row0, g_kv, w_kv, cos_t, sa_t, sb_t, *, nb, ts, vt_block=None):
    b, s, d = x.shape
    nbt, nst = b // nb, s // ts
    r = nb * ts
    mrow = mod_row0 // nb
    xspec = pl.BlockSpec((nb, ts, d), lambda i, j: (i, j, 0))
    tspec = pl.BlockSpec((r, LANES), lambda i, j: (j, 0))
    if vt_block is None:
        vb_spec, vb_shape = xspec, jax.ShapeDtypeStruct(x.shape, BF)
    else:
        assert nb == 1
        vb_spec = pl.BlockSpec((1, ts // vt_block, H_B, VT_ROWS, vt_block), lambda i, j: (i, j, 0, 0, 0))
        vb_shape = jax.ShapeDtypeStruct((b, s // vt_block, H_B, VT_ROWS, vt_block), BF)
    return pl.pallas_call(
        functools.partial(_kv_kernel, vt_block=vt_block),
        grid=(nbt, nst),
        in_specs=[
            xspec,
            pl.BlockSpec((nb, 2, d), lambda i, j: (mrow + i, 0, 0)),
            pl.BlockSpec((1, d), lambda i, j: (0, 0)),
            _resident((d, 2 * d), lambda i, j: (0, 0)),
            tspec, tspec, tspec,
        ],
        out_specs=[xspec, xspec, xspec, vb_spec],
        out_shape=[
            jax.ShapeDtypeStruct(x.shape, F32),
            jax.ShapeDtypeStruct(x.shape, F32),
            jax.ShapeDtypeStruct(x.shape, BF),
            vb_shape,
        ],
        compiler_params=_params(2),
        name="shared_kv",
    )(x, modkv, g_kv.reshape(1, d), w_kv, cos_t, sa_t, sb_t)


def _lambda(lam_ref, lam_init):
    lp = lam_ref[...]
    l1 = jnp.sum(lp[0:1, :] * lp[1:2, :], axis=-1, keepdims=True)
    l2 = jnp.sum(lp[2:3, :] * lp[3:4, :], axis=-1, keepdims=True)
    return jnp.exp(l1) - jnp.exp(l2) + lam_init


def _project_q(hb, wq_ref, cos, sa, sb, q_scr, tq):
    q = _dot(hb, wq_ref[...])
    lane = lax.broadcasted_iota(jnp.int32, (tq, LANES), 1)
    first = lane < DH_B
    for c in range(H_B):
        rq = _rope64(q[:, c * LANES:(c + 1) * LANES], cos, sa, sb) * (DH_B ** -0.5)
        q_scr[0:tq, c * LANES:(c + 1) * LANES] = jnp.where(first, rq, 0.0).astype(BF)
        q_scr[tq:2 * tq, c * LANES:(c + 1) * LANES] = jnp.where(first, 0.0, rq).astype(BF)


def _finish_head(acc, l, lam, gsub, lam_init, tq):
    o = acc[0:tq] / l[0:tq] - lam * (acc[tq:2 * tq] / l[tq:2 * tq])
    return _rmsn(o) * gsub * (1.0 - lam_init)


def _roll_rows(x, shift):
    n = x.shape[0]
    return jnp.concatenate([x[n - shift:], x[:n - shift]], axis=0)


def _diffp_kernel(x_ref, mod_ref, g_ref, wqt_ref, k_ref, vt_ref, cos_ref, sa_ref, sb_ref, lam_ref,
                  gsub_ref, wout_ref, o_ref, qt_scr, m_scr, acc_scr, oh_scr, s_scr, p_scr, a_scr,
                  *, tq, lam_init):
    qi = pl.program_id(1)
    x = x_ref[0]
    shift = mod_ref[0, 3:4, :]
    scale = mod_ref[0, 4:5, :]
    gate = mod_ref[0, 5:6, :]
    h = _rmsn(x) * g_ref[2:3, :] * (1.0 + scale) + shift
    hb = h.astype(BF)
    cos = cos_ref[...]
    sa = sa_ref[...]
    sb = sb_ref[...]
    first = lax.broadcasted_iota(jnp.int32, (HB_W, tq), 0) < DH_B
    for hd in range(H_B):
        blk = _dot_nt(wqt_ref[hd * HB_W:(hd + 1) * HB_W, :], hb)
        rq = (blk * cos + _roll_rows(blk, DH_B // 2) * sa + _roll_rows(blk, HB_W - DH_B // 2) * sb) * Q_SCALE
        qt_scr[hd] = jnp.concatenate([jnp.where(first, rq, 0.0), jnp.where(first, 0.0, rq)], axis=1).astype(BF)
        m_scr[hd] = jnp.full((1, 2 * tq), NEG, F32)
        acc_scr[hd] = jnp.zeros((VT_ROWS, 2 * tq), F32)

    def block(kb, masked):
        rows = pl.ds(pl.multiple_of(kb * tq, tq), tq)
        for hd in range(H_B):
            cols = slice(hd * HB_W, (hd + 1) * HB_W)
            s_scr[hd] = _dot(k_ref[0, rows, cols], qt_scr[hd])
        if masked:
            ki = lax.broadcasted_iota(jnp.int32, (tq, 2 * tq), 0)
            ci = lax.broadcasted_iota(jnp.int32, (tq, 2 * tq), 1)
            qloc = jnp.where(ci >= tq, ci - tq, ci)
            bias = jnp.where(ki < ((qloc >> CHUNK_SHIFT) + 1) * CHUNK, 0.0, NEG)
        for hd in range(H_B):
            s = s_scr[hd] + bias if masked else s_scr[hd]
            m_prev = m_scr[hd]
            m_new = jnp.maximum(m_prev, jnp.max(s, axis=0, keepdims=True))
            p = jnp.exp2(s - m_new).astype(BF)
            m_scr[hd] = m_new
            acc_scr[hd] = jnp.exp2(m_prev - m_new) * acc_scr[hd] + _dot(vt_ref[0, kb, hd], p)

    def body(kb, carry):
        block(kb, False)
        return carry

    lax.fori_loop(0, qi, body, 0)
    block(qi, True)

    lam = _lambda(lam_ref, lam_init)
    for hd in range(H_B):
        acc = acc_scr[hd, 0:HB_W, :]
        inv_l = 1.0 / acc_scr[hd, HB_W:HB_W + 1, :]
        o = acc[:, 0:tq] * inv_l[:, 0:tq] - lam * (acc[:, tq:2 * tq] * inv_l[:, tq:2 * tq])
        on = o * lax.rsqrt(jnp.mean(o * o, axis=0, keepdims=True) + EPS)
        oh_scr[hd * HB_W:(hd + 1) * HB_W, :] = (on * gsub_ref[...] * (1.0 - lam_init)).astype(BF)

    y = _dot_tn(oh_scr[...], wout_ref[...])
    o_ref[0] = x + gate * (_rmsn(y) * g_ref[3:4, :])


def _diff_prompt(x, mod, mod_row0, g, w_qt, kb, vtb, cos_t, sa_t, sb_t, lam_p, g_sub, w_out, lam_init, *, tq):
    b, s, d = x.shape
    kern = functools.partial(_diffp_kernel, tq=tq, lam_init=lam_init)
    tspec = pl.BlockSpec((HB_W, tq), lambda i, j: (0, j))
    return pl.pallas_call(
        kern,
        grid=(b, s // tq),
        in_specs=[
            pl.BlockSpec((1, tq, d), lambda i, j: (i, j, 0)),
            pl.BlockSpec((1, N_MOD, d), lambda i, j: (mod_row0 + i, 0, 0)),
            pl.BlockSpec((6, d), lambda i, j: (0, 0)),
            _resident((d, d), lambda i, j: (0, 0)),
            pl.BlockSpec((1, s, d), lambda i, j: (i, 0, 0)),
            pl.BlockSpec((1, s // tq, H_B, VT_ROWS, tq), lambda i, j: (i, 0, 0, 0, 0)),
            tspec, tspec, tspec,
            pl.BlockSpec((4, DH_B), lambda i, j: (0, 0)),
            pl.BlockSpec((HB_W, 1), lambda i, j: (0, 0)),
            _resident((d, d), lambda i, j: (0, 0)),
        ],
        out_specs=pl.BlockSpec((1, tq, d), lambda i, j: (i, j, 0)),
        out_shape=jax.ShapeDtypeStruct(x.shape, F32),
        scratch_shapes=[
            pltpu.VMEM((H_B, HB_W, 2 * tq), BF),
            pltpu.VMEM((H_B, 1, 2 * tq), F32),
            pltpu.VMEM((H_B, VT_ROWS, 2 * tq), F32),
            pltpu.VMEM((d, tq), BF),
            pltpu.VMEM((H_B, tq, 2 * tq), F32),
            pltpu.VMEM((H_B, tq, 2 * tq), BF),
            pltpu.VMEM((H_B, 1, 2 * tq), F32),
        ],
        compiler_params=_params(2),
        name="diff_prompt",
    )(x, mod, g, w_qt, kb, vtb, cos_t, sa_t, sb_t, lam_p, g_sub.reshape(HB_W, 1), w_out)


def _diffs_kernel(x_ref, mod_ref, g_ref, wq_ref, kn_ref, vn_ref, kc_ref, vc_ref, cos_ref, sa_ref, sb_ref,
                  lam_ref, gsub_ref, wout_ref, o_ref, q_scr, m_scr, l_scr, acc_scr, oh_scr, *, tq, lam_init):
    kc = pl.program_id(1)
    rows2 = 2 * tq

    @pl.when(kc == 0)
    def _():
        x = x_ref[0]
        h = _rmsn(x) * g_ref[2:3, :] * (1.0 + mod_ref[0, 4:5, :]) + mod_ref[0, 3:4, :]
        _project_q(h.astype(BF), wq_ref, cos_ref[...], sa_ref[...], sb_ref[...], q_scr, tq)
        for hd in range(H_B):
            cols = slice(hd * HB_W, (hd + 1) * HB_W)
            s = _dot_nt(q_scr[:, cols], kn_ref[0, :, cols])
            m0 = jnp.max(s, axis=-1, keepdims=True)
            p = jnp.exp(s - m0)
            m_scr[hd] = jnp.broadcast_to(m0, (rows2, LANES))
            l_scr[hd] = jnp.broadcast_to(jnp.sum(p, axis=-1, keepdims=True), (rows2, LANES))
            acc_scr[hd] = _dot(p.astype(BF), vn_ref[0, :, cols])

    tkc = kc_ref.shape[3]
    for hd in range(H_B):
        s = jnp.concatenate(
            [_dot(q_scr[t * tq:(t + 1) * tq, (2 * hd + t) * DH_B:(2 * hd + t + 1) * DH_B],
                  kc_ref[0, 2 * hd + t].astype(BF)) for t in range(2)], axis=0)
        v_h = vc_ref[0, pl.ds(hd, tkc, stride=H_B), :].astype(BF)
        m_prev = m_scr[hd][:, 0:1]
        m_new = jnp.maximum(m_prev, jnp.max(s, axis=-1, keepdims=True))
        alpha = jnp.exp(m_prev - m_new)
        p = jnp.exp(s - m_new)
        l_scr[hd] = jnp.broadcast_to(alpha * l_scr[hd][:, 0:1] + jnp.sum(p, axis=-1, keepdims=True),
                                     (rows2, LANES))
        acc_scr[hd] = alpha * acc_scr[hd] + _dot(p.astype(BF), v_h)
        m_scr[hd] = jnp.broadcast_to(m_new, (rows2, LANES))

    @pl.when(kc == pl.num_programs(1) - 1)
    def _():
        lam = _lambda(lam_ref, lam_init)
        for hd in range(H_B):
            cols = slice(hd * HB_W, (hd + 1) * HB_W)
            oh_scr[:, cols] = _finish_head(acc_scr[hd], l_scr[hd][:, 0:1], lam, gsub_ref[...],
                                           lam_init, tq).astype(BF)
        y = _dot(oh_scr[...], wout_ref[...])
        o_ref[0] = x_ref[0] + mod_ref[0, 5:6, :] * (_rmsn(y) * g_ref[3:4, :])


def _diff_sample(x, mod, mod_row0, g, w_q, kn, vn, cache_k, cache_v, cos_t, sa_t, sb_t, lam_p, g_sub, w_out,
                 lam_init, *, tkc):
    b, tq, d = x.shape
    past = cache_k.shape[1]
    cache_kt = jnp.transpose(cache_k, (0, 2, 3, 1))
    cache_vf = cache_v.reshape(b, past * H_B, HB_W)
    kern = functools.partial(_diffs_kernel, tq=tq, lam_init=lam_init)
    tspec = pl.BlockSpec((tq, LANES), lambda i, j: (0, 0))
    xspec = pl.BlockSpec((1, tq, d), lambda i, j: (i, 0, 0))
    kspec = pl.BlockSpec((1, 2 * H_B, DH_B, tkc), lambda i, j: (i, 0, 0, j))
    vspec = pl.BlockSpec((1, tkc * H_B, HB_W), lambda i, j: (i, j, 0))
    return pl.pallas_call(
        kern,
        grid=(b, past // tkc),
        in_specs=[
            xspec,
            pl.BlockSpec((1, N_MOD, d), lambda i, j: (mod_row0 + i, 0, 0)),
            pl.BlockSpec((6, d), lambda i, j: (0, 0)),
            _resident((d, d), lambda i, j: (0, 0)),
            xspec, xspec, kspec, vspec,
            tspec, tspec, tspec,
            pl.BlockSpec((4, DH_B), lambda i, j: (0, 0)),
            pl.BlockSpec((1, HB_W), lambda i, j: (0, 0)),
            _resident((d, d), lambda i, j: (0, 0)),
        ],
        out_specs=xspec,
        out_shape=jax.ShapeDtypeStruct(x.shape, F32),
        scratch_shapes=[
            pltpu.VMEM((2 * tq, d), BF),
            pltpu.VMEM((H_B, 2 * tq, LANES), F32),
            pltpu.VMEM((H_B, 2 * tq, LANES), F32),
            pltpu.VMEM((H_B, 2 * tq, HB_W), F32),
            pltpu.VMEM((tq, d), BF),
        ],
        compiler_params=_params(2),
        name="diff_sample",
    )(x, mod, g, w_q, kn, vn, cache_kt, cache_vf, cos_t, sa_t, sb_t, lam_p, g_sub.reshape(1, HB_W), w_out)


def _rope_tables_a(pos):
    inv = jnp.power(ROPE_THETA, -jnp.arange(0, DK_A, 2, dtype=jnp.float32) / DK_A)
    ang = pos[:, None] * inv[None, :]
    cos, sin = jnp.cos(ang), jnp.sin(ang)
    return jnp.concatenate([cos, cos], axis=-1), jnp.concatenate([-sin, sin], axis=-1)


def _rope_tables_b(pos):
    inv = jnp.power(ROPE_THETA, -jnp.arange(0, DH_B, 2, dtype=jnp.float32) / DH_B)
    ang = pos[:, None] * inv[None, :]
    cos, sin = jnp.cos(ang), jnp.sin(ang)
    zero = jnp.zeros_like(sin)
    cos_t = jnp.concatenate([cos, cos, cos, cos], axis=-1)
    sa = jnp.concatenate([zero, sin, zero, sin], axis=-1)
    sb = jnp.concatenate([-sin, zero, -sin, zero], axis=-1)
    return cos_t, sa, sb


def kernel(x_prompt, x_sample, state_ret, cache_k, cache_v, c_prompt, c_sample, w_ada, b_ada, g_norm, w_ffn_in,
           w_ffn_out, w_in_a, g_gn_a, w_out_a, w_ada_kv, b_ada_kv, g_kv, w_kv, w_q_b, lam_b, g_subln_b, w_out_b):
    d = D_MODEL
    bp, sp, _ = x_prompt.shape
    bs, ss, _ = x_sample.shape

    n_rows = -(-(bs + bp) // 16) * 16
    c_all = jnp.concatenate([c_sample, c_prompt, jnp.zeros((n_rows - bs - bp, d), F32)], axis=0)
    mod = _ada_proj(c_all, w_ada, b_ada).reshape(2, n_rows, N_MOD, d)
    modkv = _ada_proj(c_all, w_ada_kv[None], b_ada_kv[None]).reshape(n_rows, 2, d)
    row_s, row_p = 0, bs

    wfi = w_ffn_in.astype(BF)
    wfo = w_ffn_out.astype(BF)
    wia = w_in_a.astype(BF)
    woa = w_out_a.astype(BF)
    wkv = w_kv.astype(BF)
    wqb = w_q_b.astype(BF)
    wob = w_out_b.astype(BF)

    pos_p = jnp.arange(sp, dtype=jnp.float32)
    pos_s = PAST_LEN + jnp.arange(ss, dtype=jnp.float32)
    nb_s = 16
    nb_r = 4
    ts_p = 512
    t_ret = 256
    tq = 256

    cos_ap, sin_ap = _rope_tables_a(pos_p)
    cos_as, sin_as = [jnp.tile(a, (nb_r, 1)) for a in _rope_tables_a(pos_s)]
    tab_bp = _rope_tables_b(pos_p)
    tab_bs = _rope_tables_b(pos_s)
    tab_bs_tiled = [jnp.tile(a, (nb_s, 1)) for a in tab_bs]

    def ffn_p(x, l, i):
        return _ffn(x, mod[l], row_p, g_norm[l], wfi[l, i], wfo[l, i], nb=1, ts=ts_p, m0=6 * i, g0=4 * i)

    def ffn_s(x, l, i):
        return _ffn(x, mod[l], row_s, g_norm[l], wfi[l, i], wfo[l, i], nb=nb_s, ts=ss, m0=6 * i, g0=4 * i)

    xp = ffn_p(x_prompt, 0, 0)
    xs = ffn_s(x_sample, 0, 0)
    xp, ret_p = _ret_mixer(xp, mod[0], row_p, g_norm[0], wia[0], g_gn_a[0], woa[0], cos_ap, sin_ap, None,
                           nb=1, t=t_ret)
    xs, ret_s = _ret_mixer(xs, mod[0], row_s, g_norm[0], wia[0], g_gn_a[0], woa[0], cos_as, sin_as, state_ret[0],
                           nb=nb_r, t=ss)
    xp = ffn_p(xp, 0, 1)
    xs = ffn_s(xs, 0, 1)
    k_p, v_p, kb_p, vtb_p = _shared_kv(xp, modkv, row_p, g_kv, wkv, *tab_bp, nb=1, ts=ts_p, vt_block=tq)
    k_s, v_s, kb_s, vb_s = _shared_kv(xs, modkv, row_s, g_kv, wkv, *tab_bs_tiled, nb=nb_s, ts=ss)

    lam_init = 0.8 - 0.6 * math.exp(-0.3 * 1)
    xp = ffn_p(xp, 1, 0)
    xs = ffn_s(xs, 1, 0)
    xp = _diff_prompt(xp, mod[1], row_p, g_norm[1], w_q_b[0].T.astype(BF), kb_p, vtb_p,
                      *[a.T for a in tab_bp], lam_b[0], g_subln_b[0], wob[0],
                      lam_init, tq=tq)
    xs = _diff_sample(xs, mod[1], row_s, g_norm[1], wqb[0], kb_s, vb_s,
                      cache_k, cache_v, *tab_bs, lam_b[0], g_subln_b[0], wob[0], lam_init, tkc=1024)
    xp = ffn_p(xp, 1, 1)
    xs = ffn_s(xs, 1, 1)

    return (xp, xs,
            ret_p[None], k_p.reshape(bp, sp, 2 * H_B, DH_B), v_p.reshape(bp, sp, H_B, 2 * DH_B),
            ret_s[None], k_s.reshape(bs, ss, 2 * H_B, DH_B), v_s.reshape(bs, ss, H_B, 2 * DH_B))
```

```python
import functools
import math

import jax
import jax.numpy as jnp
from jax import lax
from jax.experimental import pallas as pl
from jax.experimental.pallas import tpu as pltpu

D_MODEL = 1024
CHUNK = 64
CHUNK_SHIFT = 6
H_A = 8
DK_A = 128
DV_A = 256
QK_A = H_A * DK_A
V_A = H_A * DV_A
H_B = 8
DH_B = 64
HB_W = 2 * DH_B
D_FF = 2816
N_MOD = 9
ROPE_THETA = 10000.0
EPS = 1e-6
HALF = 0.5
PAST_LEN = 4096
NEG = -1e30
LOG2E = 1.4426950408889634
Q_SCALE = DH_B ** -0.5 * LOG2E
VT_ROWS = HB_W + 16

LANES = 128
VMEM_LIMIT = 56 * 1024 * 1024

BF = jnp.bfloat16
F32 = jnp.float32

LOG_GAMMA = [math.log1p(-(2.0 ** (-5 - h))) for h in range(H_A)]


def _dot(a, b):
    return jnp.dot(a, b, preferred_element_type=F32)


def _dot_nt(a, b):
    return lax.dot_general(a, b, (((1,), (1,)), ((), ())), preferred_element_type=F32)


def _dot_tn(a, b):
    return lax.dot_general(a, b, (((0,), (0,)), ((), ())), preferred_element_type=F32)


def _rmsn(x):
    return x * lax.rsqrt(jnp.mean(x * x, axis=-1, keepdims=True) + EPS)


def _silu(x):
    return x * jax.nn.sigmoid(x)


def _resident(shape, index_map):
    return pl.BlockSpec(shape, index_map, pipeline_mode=pl.Buffered(1))


def _params(n_axes):
    return pltpu.CompilerParams(dimension_semantics=("arbitrary",) * n_axes,
                                vmem_limit_bytes=VMEM_LIMIT)


def _ada_kernel(c_ref, w_ref, b_ref, o_ref):
    sc = _silu(c_ref[...]).astype(BF)
    o_ref[...] = _dot(sc, w_ref[...].astype(BF)) + b_ref[...]


def _ada_proj(c_all, w, b, tn=1024):
    nl, d, n = w.shape
    r = c_all.shape[0]
    return pl.pallas_call(
        _ada_kernel,
        grid=(nl, n // tn),
        in_specs=[
            pl.BlockSpec((r, d), lambda l, j: (0, 0)),
            pl.BlockSpec((None, d, tn), lambda l, j: (l, 0, j)),
            pl.BlockSpec((None, 1, tn), lambda l, j: (l, 0, j)),
        ],
        out_specs=pl.BlockSpec((None, r, tn), lambda l, j: (l, 0, j)),
        out_shape=jax.ShapeDtypeStruct((nl, r, n), F32),
        compiler_params=_params(2),
        name="ada_proj",
    )(c_all, w, b.reshape(nl, 1, n))


def _ffn_kernel(x_ref, mod_ref, g_ref, win_ref, wout_ref, o_ref, *, m0, g0, fc):
    nb, ts, d = x_ref.shape
    x = x_ref[...]
    shift = mod_ref[:, m0:m0 + 1, :]
    scale = mod_ref[:, m0 + 1:m0 + 2, :]
    gate = mod_ref[:, m0 + 2:m0 + 3, :]
    h = _rmsn(x) * g_ref[g0:g0 + 1, :] * (1.0 + scale) + shift
    hb = h.reshape(nb * ts, d).astype(BF)
    acc = jnp.zeros((nb * ts, d), F32)
    for c in range(D_FF // fc):
        a = _dot(hb, win_ref[:, c * fc:(c + 1) * fc])
        b = _dot(hb, win_ref[:, D_FF + c * fc:D_FF + (c + 1) * fc])
        u = (_silu(a) * b).astype(BF)
        acc = acc + _dot(u, wout_ref[c * fc:(c + 1) * fc, :])
    y = _rmsn(acc) * g_ref[g0 + 1:g0 + 2, :]
    o_ref[...] = x + (HALF * gate) * y.reshape(nb, ts, d)


def _ffn(x, mod, mod_row0, g, w_in, w_out, *, layer, idx, nb, ts):
    b, s, d = x.shape
    nbt, nst = b // nb, s // ts
    mrow = mod_row0 // nb
    kern = functools.partial(_ffn_kernel, m0=6 * idx, g0=4 * idx, fc=256)
    return pl.pallas_call(
        kern,
        grid=(nbt, nst),
        in_specs=[
            pl.BlockSpec((nb, ts, d), lambda i, j: (i, j, 0)),
            pl.BlockSpec((None, nb, N_MOD, d), lambda i, j: (layer, mrow + i, 0, 0)),
            pl.BlockSpec((None, 6, d), lambda i, j: (layer, 0, 0)),
            _resident((None, None, d, 2 * D_FF), lambda i, j: (layer, idx, 0, 0)),
            _resident((None, None, D_FF, d), lambda i, j: (layer, idx, 0, 0)),
        ],
        out_specs=pl.BlockSpec((nb, ts, d), lambda i, j: (i, j, 0)),
        out_shape=jax.ShapeDtypeStruct(x.shape, F32),
        compiler_params=_params(2),
        name="ffn",
    )(x, mod, g, w_in, w_out)


def _decay_mask(lg, t):
    i = lax.broadcasted_iota(jnp.int32, (t, t), 0)
    j = lax.broadcasted_iota(jnp.int32, (t, t), 1)
    dist = jnp.abs(i - j).astype(F32)
    vis = (j >> CHUNK_SHIFT) <= (i >> CHUNK_SHIFT)
    return jnp.where(vis, jnp.exp(lg * dist), 0.0)


def _ret_kernel(*refs, nb, t, from_input_state):
    if from_input_state:
        (x_ref, mod_ref, g_ref, win_ref, ggn_ref, wout_ref, cos_ref, sin_ref, sin_st_ref,
         o_ref, so_ref, q_scr, k_scr, v_scr, og_scr, gat_scr, mask_scr, st_scr) = refs
    else:
        (x_ref, mod_ref, g_ref, win_ref, ggn_ref, wout_ref, cos_ref, sin_ref,
         o_ref, so_ref, q_scr, k_scr, v_scr, og_scr, gat_scr, mask_scr, st_scr) = refs
        sin_st_ref = None
    d = D_MODEL
    r = nb * t
    j = pl.program_id(1)

    @pl.when(j == 0)
    def _():
        for hd in range(H_A):
            mask_scr[hd] = _decay_mask(LOG_GAMMA[hd], t)
        if not from_input_state:
            st_scr[...] = jnp.zeros_like(st_scr)

    x = x_ref[...]
    shift = mod_ref[:, 3:4, :]
    scale = mod_ref[:, 4:5, :]
    gate = mod_ref[:, 5:6, :]
    h = _rmsn(x) * g_ref[2:3, :] * (1.0 + scale) + shift
    hb = h.reshape(r, d).astype(BF)

    cos = cos_ref[...]
    sin = sin_ref[...]
    qk = _dot(hb, win_ref[:, 0:2 * QK_A])
    for c in range(H_A):
        qc = qk[:, c * DK_A:(c + 1) * DK_A]
        q_scr[:, c * DK_A:(c + 1) * DK_A] = qc * cos + pltpu.roll(qc, DK_A // 2, 1) * sin
        kc = qk[:, QK_A + c * DK_A:QK_A + (c + 1) * DK_A]
        k_scr[:, c * DK_A:(c + 1) * DK_A] = (kc * cos + pltpu.roll(kc, DK_A // 2, 1) * sin) * (DK_A ** -0.5)
    v_scr[...] = _dot(hb, win_ref[:, 2 * QK_A:2 * QK_A + V_A]).astype(BF)
    gp = _dot(hb, win_ref[:, 2 * QK_A + V_A:2 * QK_A + 2 * V_A])

    row = lax.broadcasted_iota(jnp.int32, (t, DK_A), 0).astype(F32)
    for hd in range(H_A):
        lg = LOG_GAMMA[hd]
        qdec = jnp.exp(lg * row)
        kdec = jnp.exp(lg * (float(t) - row))
        cdec = math.exp(lg * t)
        mask = mask_scr[hd]

        def one_batch(n, carry, hd=hd, qdec=qdec, kdec=kdec, cdec=cdec, mask=mask):
            rows = pl.ds(n * t, t)
            q = q_scr[rows, hd * DK_A:(hd + 1) * DK_A]
            k = k_scr[rows, hd * DK_A:(hd + 1) * DK_A]
            v = v_scr[rows, hd * DV_A:(hd + 1) * DV_A]
            if from_input_state:
                st = sin_st_ref[n, hd]
            else:
                st = st_scr[hd]
            s = _dot_nt(q.astype(BF), k.astype(BF)) * mask
            o = _dot(s.astype(BF), v) + _dot((q * qdec).astype(BF), st.astype(BF))
            st_new = st * cdec + _dot_tn((k * kdec).astype(BF), v)
            og_scr[rows, hd * DV_A:(hd + 1) * DV_A] = o
            if from_input_state:
                so_ref[n, hd] = st_new
            else:
                st_scr[hd] = st_new
            return carry

        for n in range(nb):
            one_batch(n, 0)

    for hd in range(H_A):
        cols = slice(hd * DV_A, (hd + 1) * DV_A)
        on = _rmsn(og_scr[:, cols]) * ggn_ref[:, cols]
        gat_scr[:, cols] = (_silu(gp[:, cols]) * on).astype(BF)
    y = _dot(gat_scr[...], wout_ref[...])
    yn = _rmsn(y) * g_ref[3:4, :]
    o_ref[...] = x + gate * yn.reshape(nb, t, d)

    if not from_input_state:
        @pl.when(j == pl.num_programs(1) - 1)
        def _():
            so_ref[0] = st_scr[...]


def _ret_mixer(x, mod, mod_row0, g, w_in, g_gn, w_out, cos_t, sin_t, state_in, *, nb, t):
    b, s, d = x.shape
    nbt, nst = b // nb, s // t
    r = nb * t
    mrow = mod_row0 // nb
    from_input = state_in is not None
    kern = functools.partial(_ret_kernel, nb=nb, t=t, from_input_state=from_input)
    in_specs = [
        pl.BlockSpec((nb, t, d), lambda i, j: (i, j, 0)),
        pl.BlockSpec((nb, N_MOD, d), lambda i, j: (mrow + i, 0, 0)),
        pl.BlockSpec((6, d), lambda i, j: (0, 0)),
        _resident((d, 2 * QK_A + 2 * V_A), lambda i, j: (0, 0)),
        pl.BlockSpec((1, V_A), lambda i, j: (0, 0)),
        _resident((V_A, d), lambda i, j: (0, 0)),
        pl.BlockSpec((r, DK_A), lambda i, j: (j, 0)),
        pl.BlockSpec((r, DK_A), lambda i, j: (j, 0)),
    ]
    args = [x, mod, g, w_in, g_gn.reshape(1, V_A), w_out, cos_t, sin_t]
    if from_input:
        in_specs.append(pl.BlockSpec((nb, H_A, DK_A, DV_A), lambda i, j: (i, 0, 0, 0)))
        args.append(state_in)
    out, st = pl.pallas_call(
        kern,
        grid=(nbt, nst),
        in_specs=in_specs,
        out_specs=[
            pl.BlockSpec((nb, t, d), lambda i, j: (i, j, 0)),
            pl.BlockSpec((nb, H_A, DK_A, DV_A), lambda i, j: (i, 0, 0, 0)),
        ],
        out_shape=[
            jax.ShapeDtypeStruct(x.shape, F32),
            jax.ShapeDtypeStruct((b, H_A, DK_A, DV_A), F32),
        ],
        scratch_shapes=[
            pltpu.VMEM((r, QK_A), F32),
            pltpu.VMEM((r, QK_A), F32),
            pltpu.VMEM((r, V_A), BF),
            pltpu.VMEM((r, V_A), F32),
            pltpu.VMEM((r, V_A), BF),
            pltpu.VMEM((H_A, t, t), F32),
            pltpu.VMEM((H_A, DK_A, DV_A), F32),
        ],
        compiler_params=_params(2),
        name="ret_mixer",
    )(*args)
    return out, st


def _rope64(blk, cos, sa, sb):
    return blk * cos + pltpu.roll(blk, DH_B // 2, 1) * sa + pltpu.roll(blk, LANES - DH_B // 2, 1) * sb


def _kv_kernel(x_ref, mod_ref, g_ref, w_ref, cos_ref, sa_ref, sb_ref, k_ref, v_ref, kb_ref, vb_ref, *, vt_block):
    nb, ts, d = x_ref.shape
    x = x_ref[...]
    shift = mod_ref[:, 0:1, :]
    scale = mod_ref[:, 1:2, :]
    h = _rmsn(x) * g_ref[...] * (1.0 + scale) + shift
    hb = h.reshape(nb * ts, d).astype(BF)
    kv = _dot(hb, w_ref[...])
    cos = cos_ref[...]
    sa = sa_ref[...]
    sb = sb_ref[...]
    ks = [_rope64(kv[:, c * LANES:(c + 1) * LANES], cos, sa, sb) for c in range(d // LANES)]
    k = jnp.concatenate(ks, axis=1).reshape(nb, ts, d)
    v = kv[:, d:2 * d].reshape(nb, ts, d)
    k_ref[...] = k
    v_ref[...] = v
    kb_ref[...] = k.astype(BF)
    if vt_block is None:
        vb_ref[...] = v.astype(BF)
    else:
        vt = kv[:, d:2 * d].T
        ones = jnp.ones((VT_ROWS - HB_W, vt_block), BF)
        for c in range(ts // vt_block):
            for hd in range(H_B):
                vb_ref[0, c, hd, 0:HB_W, :] = vt[hd * HB_W:(hd + 1) * HB_W,
                                                 c * vt_block:(c + 1) * vt_block].astype(BF)
                vb_ref[0, c, hd, HB_W:VT_ROWS, :] = ones


def _shared_kv(x, modkv, mod_row0, g_kv, w_kv, cos_t, sa_t, sb_t, *, nb, ts, vt_block=None):
    b, s, d = x.shape
    nbt, nst = b // nb, s // ts
    r = nb * ts
    mrow = mod_row0 // nb
    xspec = pl.BlockSpec((nb, ts, d), lambda i, j: (i, j, 0))
    tspec = pl.BlockSpec((r, LANES), lambda i, j: (j, 0))
    if vt_block is None:
        vb_spec, vb_shape = xspec, jax.ShapeDtypeStruct(x.shape, BF)
    else:
        assert nb == 1
        vb_spec = pl.BlockSpec((1, ts // vt_block, H_B, VT_ROWS, vt_block), lambda i, j: (i, j, 0, 0, 0))
        vb_shape = jax.ShapeDtypeStruct((b, s // vt_block, H_B, VT_ROWS, vt_block), BF)
    return pl.pallas_call(
        functools.partial(_kv_kernel, vt_block=vt_block),
        grid=(nbt, nst),
        in_specs=[
            xspec,
            pl.BlockSpec((nb, 2, d), lambda i, j: (mrow + i, 0, 0)),
            pl.BlockSpec((1, d), lambda i, j: (0, 0)),
            _resident((d, 2 * d), lambda i, j: (0, 0)),
            tspec, tspec, tspec,
        ],
        out_specs=[xspec, xspec, xspec, vb_spec],
        out_shape=[
            jax.ShapeDtypeStruct(x.shape, F32),
            jax.ShapeDtypeStruct(x.shape, F32),
            jax.ShapeDtypeStruct(x.shape, BF),
            vb_shape,
        ],
        compiler_params=_params(2),
        name="shared_kv",
    )(x, modkv, g_kv.reshape(1, d), w_kv, cos_t, sa_t, sb_t)


def _lambda(lam_ref, lam_init):
    lp = lam_ref[...]
    l1 = jnp.sum(lp[0:1, :] * lp[1:2, :], axis=-1, keepdims=True)
    l2 = jnp.sum(lp[2:3, :] * lp[3:4, :], axis=-1, keepdims=True)
    return jnp.exp(l1) - jnp.exp(l2) + lam_init


def _project_q(hb, wq_ref, cos, sa, sb, q_scr, tq):
    q = _dot(hb, wq_ref[...])
    lane = lax.broadcasted_iota(jnp.int32, (tq, LANES), 1)
    first = lane < DH_B
    for c in range(H_B):
        rq = _rope64(q[:, c * LANES:(c + 1) * LANES], cos, sa, sb) * Q_SCALE
        q_scr[0:tq, c * LANES:(c + 1) * LANES] = jnp.where(first, rq, 0.0).astype(BF)
        q_scr[tq:2 * tq, c * LANES:(c + 1) * LANES] = jnp.where(first, 0.0, rq).astype(BF)


def _finish_head(acc, l, lam, gsub, lam_init, tq):
    o = acc[0:tq] / l[0:tq] - lam * (acc[tq:2 * tq] / l[tq:2 * tq])
    return _rmsn(o) * gsub * (1.0 - lam_init)


def _roll_rows(x, shift):
    n = x.shape[0]
    return jnp.concatenate([x[n - shift:], x[:n - shift]], axis=0)


def _diffp_kernel(x_ref, mod_ref, g_ref, wqt_ref, k_ref, vt_ref, cos_ref, sa_ref, sb_ref, lam_ref,
                  gsub_ref, wout_ref, o_ref, qt_scr, m_scr, acc_scr, oh_scr, s_scr, bias_scr, *, tq, lam_init):
    qi = pl.program_id(1)

    @pl.when(qi == 0)
    def _():
        ki = lax.broadcasted_iota(jnp.int32, (tq, 2 * tq), 0)
        ci = lax.broadcasted_iota(jnp.int32, (tq, 2 * tq), 1)
        qloc = jnp.where(ci >= tq, ci - tq, ci)
        bias_scr[...] = jnp.where(ki < ((qloc >> CHUNK_SHIFT) + 1) * CHUNK, 0.0, NEG)

    x = x_ref[0]
    shift = mod_ref[0, 3:4, :]
    scale = mod_ref[0, 4:5, :]
    gate = mod_ref[0, 5:6, :]
    h = _rmsn(x) * g_ref[2:3, :] * (1.0 + scale) + shift
    ht = h.T.astype(BF)
    cos = cos_ref[...]
    sa = sa_ref[...]
    sb = sb_ref[...]
    first = lax.broadcasted_iota(jnp.int32, (HB_W, tq), 0) < DH_B
    half = H_B // 2
    for hd in range(H_B):
        if hd % half == 0:
            qt4 = _dot(wqt_ref[hd * HB_W:(hd + half) * HB_W, :], ht)
        blk = qt4[(hd % half) * HB_W:(hd % half + 1) * HB_W, :]
        rq = (blk * cos + _roll_rows(blk, DH_B // 2) * sa + _roll_rows(blk, HB_W - DH_B // 2) * sb) * Q_SCALE
        qt_scr[hd] = jnp.concatenate([jnp.where(first, rq, 0.0), jnp.where(first, 0.0, rq)], axis=1).astype(BF)
        m_scr[hd] = jnp.full((1, 2 * tq), NEG, F32)
        acc_scr[hd] = jnp.zeros((VT_ROWS, 2 * tq), F32)

    def block(kb, masked):
        rows = pl.ds(pl.multiple_of(kb * tq, tq), tq)
        for hd in range(H_B):
            cols = slice(hd * HB_W, (hd + 1) * HB_W)
            s_scr[hd] = _dot(k_ref[0, rows, cols], qt_scr[hd])
        for hd in range(H_B):
            s = s_scr[hd] + bias_scr[...] if masked else s_scr[hd]
            m_prev = m_scr[hd]
            m_new = jnp.maximum(m_prev, jnp.max(s, axis=0, keepdims=True))
            p = jnp.exp2(s - m_new).astype(BF)
            m_scr[hd] = m_new
            acc_scr[hd] = jnp.exp2(m_prev - m_new) * acc_scr[hd] + _dot(vt_ref[0, kb, hd], p)

    def body(kb, carry):
        block(kb, False)
        return carry

    lax.fori_loop(0, qi, body, 0)
    block(qi, True)

    lam = _lambda(lam_ref, lam_init)
    for hd in range(H_B):
        acc = acc_scr[hd, 0:HB_W, :]
        inv_l = 1.0 / acc_scr[hd, HB_W:HB_W + 1, :]
        o = acc[:, 0:tq] * inv_l[:, 0:tq] - lam * (acc[:, tq:2 * tq] * inv_l[:, tq:2 * tq])
        on = o * lax.rsqrt(jnp.mean(o * o, axis=0, keepdims=True) + EPS)
        oh_scr[hd * HB_W:(hd + 1) * HB_W, :] = (on * gsub_ref[...] * (1.0 - lam_init)).astype(BF)

    y = _dot_tn(oh_scr[...], wout_ref[...])
    o_ref[0] = x + gate * (_rmsn(y) * g_ref[3:4, :])


def _diff_prompt(x, mod, mod_row0, g, w_qt, kb, vtb, cos_t, sa_t, sb_t, lam_p, g_sub, w_out, lam_init, *, tq):
    b, s, d = x.shape
    kern = functools.partial(_diffp_kernel, tq=tq, lam_init=lam_init)
    tspec = pl.BlockSpec((HB_W, tq), lambda i, j: (0, j))
    return pl.pallas_call(
        kern,
        grid=(b, s // tq),
        in_specs=[
            pl.BlockSpec((1, tq, d), lambda i, j: (i, j, 0)),
            pl.BlockSpec((1, N_MOD, d), lambda i, j: (mod_row0 + i, 0, 0)),
            pl.BlockSpec((6, d), lambda i, j: (0, 0)),
            _resident((d, d), lambda i, j: (0, 0)),
            pl.BlockSpec((1, s, d), lambda i, j: (i, 0, 0)),
            pl.BlockSpec((1, s // tq, H_B, VT_ROWS, tq), lambda i, j: (i, 0, 0, 0, 0)),
            tspec, tspec, tspec,
            pl.BlockSpec((4, DH_B), lambda i, j: (0, 0)),
            pl.BlockSpec((HB_W, 1), lambda i, j: (0, 0)),
            _resident((d, d), lambda i, j: (0, 0)),
        ],
        out_specs=pl.BlockSpec((1, tq, d), lambda i, j: (i, j, 0)),
        out_shape=jax.ShapeDtypeStruct(x.shape, F32),
        scratch_shapes=[
            pltpu.VMEM((H_B, HB_W, 2 * tq), BF),
            pltpu.VMEM((H_B, 1, 2 * tq), F32),
            pltpu.VMEM((H_B, VT_ROWS, 2 * tq), F32),
            pltpu.VMEM((d, tq), BF),
            pltpu.VMEM((H_B, tq, 2 * tq), F32),
            pltpu.VMEM((tq, 2 * tq), F32),
        ],
        compiler_params=_params(2),
        name="diff_prompt",
    )(x, mod, g, w_qt, kb, vtb, cos_t, sa_t, sb_t, lam_p, g_sub.reshape(HB_W, 1), w_out)


def _diffs_kernel(x_ref, mod_ref, g_ref, wq_ref, kn_ref, vn_ref, kc_ref, vc_ref, cos_ref, sa_ref, sb_ref,
                  lam_ref, gsub_ref, wout_ref, o_ref, q_scr, m_scr, acc_scr, oh_scr, s_scr, *, tq, lam_init):
    kc = pl.program_id(1)
    rows2 = 2 * tq

    def with_ones(v):
        return jnp.concatenate([v, jnp.ones(v.shape, BF)], axis=1)

    @pl.when(kc == 0)
    def _():
        x = x_ref[0]
        h = _rmsn(x) * g_ref[2:3, :] * (1.0 + mod_ref[0, 4:5, :]) + mod_ref[0, 3:4, :]
        _project_q(h.astype(BF), wq_ref, cos_ref[...], sa_ref[...], sb_ref[...], q_scr, tq)
        for hd in range(H_B):
            cols = slice(hd * HB_W, (hd + 1) * HB_W)
            s = _dot_nt(q_scr[:, cols], kn_ref[0, :, cols])
            m0 = jnp.max(s, axis=-1, keepdims=True)
            m_scr[hd] = jnp.broadcast_to(m0, (rows2, LANES))
            acc_scr[hd] = _dot(jnp.exp2(s - m0).astype(BF), with_ones(vn_ref[0, :, cols]))

    tkc = kc_ref.shape[3]
    for hd in range(H_B):
        s_scr[hd] = jnp.concatenate(
            [_dot(q_scr[t * tq:(t + 1) * tq, (2 * hd + t) * DH_B:(2 * hd + t + 1) * DH_B],
                  kc_ref[0, 2 * hd + t].astype(BF)) for t in range(2)], axis=0)
    for hd in range(H_B):
        s = s_scr[hd]
        v_h = with_ones(vc_ref[0, pl.ds(hd, tkc, stride=H_B), :].astype(BF))
        m_prev = m_scr[hd][:, 0:1]
        m_new = jnp.maximum(m_prev, jnp.max(s, axis=-1, keepdims=True))
        acc_scr[hd] = jnp.exp2(m_prev - m_new) * acc_scr[hd] + _dot(jnp.exp2(s - m_new).astype(BF), v_h)
        m_scr[hd] = jnp.broadcast_to(m_new, (rows2, LANES))

    @pl.when(kc == pl.num_programs(1) - 1)
    def _():
        lam = _lambda(lam_ref, lam_init)
        for hd in range(H_B):
            cols = slice(hd * HB_W, (hd + 1) * HB_W)
            oh_scr[:, cols] = _finish_head(acc_scr[hd, :, 0:HB_W], acc_scr[hd, :, HB_W:HB_W + 1], lam,
                                           gsub_ref[...], lam_init, tq).astype(BF)
        y = _dot(oh_scr[...], wout_ref[...])
        o_ref[0] = x_ref[0] + mod_ref[0, 5:6, :] * (_rmsn(y) * g_ref[3:4, :])


def _diff_sample(x, mod, mod_row0, g, w_q, kn, vn, cache_k, cache_v, cos_t, sa_t, sb_t, lam_p, g_sub, w_out,
                 lam_init, *, tkc):
    b, tq, d = x.shape
    past = cache_k.shape[1]
    cache_kt = jnp.transpose(cache_k, (0, 2, 3, 1))
    cache_vf = cache_v.reshape(b, past * H_B, HB_W)
    kern = functools.partial(_diffs_kernel, tq=tq, lam_init=lam_init)
    tspec = pl.BlockSpec((tq, LANES), lambda i, j: (0, 0))
    xspec = pl.BlockSpec((1, tq, d), lambda i, j: (i, 0, 0))
    kspec = pl.BlockSpec((1, 2 * H_B, DH_B, tkc), lambda i, j: (i, 0, 0, j))
    vspec = pl.BlockSpec((1, tkc * H_B, HB_W), lambda i, j: (i, j, 0))
    return pl.pallas_call(
        kern,
        grid=(b, past // tkc),
        in_specs=[
            xspec,
            pl.BlockSpec((1, N_MOD, d), lambda i, j: (mod_row0 + i, 0, 0)),
            pl.BlockSpec((6, d), lambda i, j: (0, 0)),
            _resident((d, d), lambda i, j: (0, 0)),
            xspec, xspec, kspec, vspec,
            tspec, tspec, tspec,
            pl.BlockSpec((4, DH_B), lambda i, j: (0, 0)),
            pl.BlockSpec((1, HB_W), lambda i, j: (0, 0)),
            _resident((d, d), lambda i, j: (0, 0)),
        ],
        out_specs=xspec,
        out_shape=jax.ShapeDtypeStruct(x.shape, F32),
        scratch_shapes=[
            pltpu.VMEM((2 * tq, d), BF),
            pltpu.VMEM((H_B, 2 * tq, LANES), F32),
            pltpu.VMEM((H_B, 2 * tq, 2 * HB_W), F32),
            pltpu.VMEM((tq, d), BF),
            pltpu.VMEM((H_B, 2 * tq, tkc), F32),
        ],
        compiler_params=_params(2),
        name="diff_sample",
    )(x, mod, g, w_q, kn, vn, cache_kt, cache_vf, cos_t, sa_t, sb_t, lam_p, g_sub.reshape(1, HB_W), w_out)


def _rope_tables_a(pos):
    inv = jnp.power(ROPE_THETA, -jnp.arange(0, DK_A, 2, dtype=jnp.float32) / DK_A)
    ang = pos[:, None] * inv[None, :]
    cos, sin = jnp.cos(ang), jnp.sin(ang)
    return jnp.concatenate([cos, cos], axis=-1), jnp.concatenate([-sin, sin], axis=-1)


def _rope_tables_b(pos):
    inv = jnp.power(ROPE_THETA, -jnp.arange(0, DH_B, 2, dtype=jnp.float32) / DH_B)
    ang = pos[:, None] * inv[None, :]
    cos, sin = jnp.cos(ang), jnp.sin(ang)
    zero = jnp.zeros_like(sin)
    cos_t = jnp.concatenate([cos, cos, cos, cos], axis=-1)
    sa = jnp.concatenate([zero, sin, zero, sin], axis=-1)
    sb = jnp.concatenate([-sin, zero, -sin, zero], axis=-1)
    return cos_t, sa, sb


def kernel(x_prompt, x_sample, state_ret, cache_k, cache_v, c_prompt, c_sample, w_ada, b_ada, g_norm, w_ffn_in,
           w_ffn_out, w_in_a, g_gn_a, w_out_a, w_ada_kv, b_ada_kv, g_kv, w_kv, w_q_b, lam_b, g_subln_b, w_out_b):
    d = D_MODEL
    bp, sp, _ = x_prompt.shape
    bs, ss, _ = x_sample.shape

    n_rows = -(-(bs + bp) // 16) * 16
    c_all = jnp.concatenate([c_sample, c_prompt, jnp.zeros((n_rows - bs - bp, d), F32)], axis=0)
    mod = _ada_proj(c_all, w_ada, b_ada).reshape(2, n_rows, N_MOD, d)
    modkv = _ada_proj(c_all, w_ada_kv[None], b_ada_kv[None]).reshape(n_rows, 2, d)
    row_s, row_p = 0, bs

    wfi = w_ffn_in.astype(BF)
    wfo = w_ffn_out.astype(BF)
    wia = w_in_a.astype(BF)
    woa = w_out_a.astype(BF)
    wkv = w_kv.astype(BF)
    wqb = w_q_b.astype(BF)
    wob = w_out_b.astype(BF)

    pos_p = jnp.arange(sp, dtype=jnp.float32)
    pos_s = PAST_LEN + jnp.arange(ss, dtype=jnp.float32)
    nb_s = 16
    nb_r = 4
    ts_p = 512
    t_ret = 256
    tq = 256

    cos_ap, sin_ap = _rope_tables_a(pos_p)
    cos_as, sin_as = [jnp.tile(a, (nb_r, 1)) for a in _rope_tables_a(pos_s)]
    tab_bp = _rope_tables_b(pos_p)
    tab_bs = _rope_tables_b(pos_s)
    tab_bs_tiled = [jnp.tile(a, (nb_s, 1)) for a in tab_bs]

    def ffn_p(x, l, i):
        return _ffn(x, mod, row_p, g_norm, wfi, wfo, layer=l, idx=i, nb=1, ts=ts_p)

    def ffn_s(x, l, i):
        return _ffn(x, mod, row_s, g_norm, wfi, wfo, layer=l, idx=i, nb=nb_s, ts=ss)

    xp = ffn_p(x_prompt, 0, 0)
    xs = ffn_s(x_sample, 0, 0)
    xp, ret_p = _ret_mixer(xp, mod[0], row_p, g_norm[0], wia[0], g_gn_a[0], woa[0], cos_ap, sin_ap, None,
                           nb=1, t=t_ret)
    xs, ret_s = _ret_mixer(xs, mod[0], row_s, g_norm[0], wia[0], g_gn_a[0], woa[0], cos_as, sin_as, state_ret[0],
                           nb=nb_r, t=ss)
    xp = ffn_p(xp, 0, 1)
    xs = ffn_s(xs, 0, 1)
    k_p, v_p, kb_p, vtb_p = _shared_kv(xp, modkv, row_p, g_kv, wkv, *tab_bp, nb=1, ts=ts_p, vt_block=tq)
    k_s, v_s, kb_s, vb_s = _shared_kv(xs, modkv, row_s, g_kv, wkv, *tab_bs_tiled, nb=nb_s, ts=ss)

    lam_init = 0.8 - 0.6 * math.exp(-0.3 * 1)
    xp = ffn_p(xp, 1, 0)
    xs = ffn_s(xs, 1, 0)
    xp = _diff_prompt(xp, mod[1], row_p, g_norm[1], w_q_b[0].T.astype(BF), kb_p, vtb_p,
                      *[a.T for a in tab_bp], lam_b[0], g_subln_b[0], wob[0],
                      lam_init, tq=tq)
    xs = _diff_sample(xs, mod[1], row_s, g_norm[1], wqb[0], kb_s, vb_s,
                      cache_k, cache_v, *tab_bs, lam_b[0], g_subln_b[0], wob[0], lam_init, tkc=1024)
    xp = ffn_p(xp, 1, 1)
    xs = ffn_s(xs, 1, 1)

    return (xp, xs,
            ret_p[None], k_p.reshape(bp, sp, 2 * H_B, DH_B), v_p.reshape(bp, sp, H_B, 2 * DH_B),
            ret_s[None], k_s.reshape(bs, ss, 2 * H_B, DH_B), v_s.reshape(bs, ss, H_B, 2 * DH_B))
```

```python
import functools
import math

import jax
import jax.numpy as jnp
from jax import lax
from jax.experimental import pallas as pl
from jax.experimental.pallas import tpu as pltpu

D_MODEL = 1024
CHUNK = 64
CHUNK_SHIFT = 6
H_A = 8
DK_A = 128
DV_A = 256
QK_A = H_A * DK_A
V_A = H_A * DV_A
H_B = 8
DH_B = 64
HB_W = 2 * DH_B
D_FF = 2816
N_MOD = 9
ROPE_THETA = 10000.0
EPS = 1e-6
HALF = 0.5
PAST_LEN = 4096
NEG = -1e30
LOG2E = 1.4426950408889634
Q_SCALE = DH_B ** -0.5 * LOG2E
VT_ROWS = HB_W + 16

LANES = 128
VMEM_LIMIT = 56 * 1024 * 1024

BF = jnp.bfloat16
F32 = jnp.float32

LOG_GAMMA = [math.log1p(-(2.0 ** (-5 - h))) for h in range(H_A)]


def _dot(a, b):
    return jnp.dot(a, b, preferred_element_type=F32)


def _dot_nt(a, b):
    return lax.dot_general(a, b, (((1,), (1,)), ((), ())), preferred_element_type=F32)


def _dot_tn(a, b):
    return lax.dot_general(a, b, (((0,), (0,)), ((), ())), preferred_element_type=F32)


def _rmsn(x):
    return x * lax.rsqrt(jnp.mean(x * x, axis=-1, keepdims=True) + EPS)


def _silu(x):
    return x * jax.nn.sigmoid(x)


def _resident(shape, index_map):
    return pl.BlockSpec(shape, index_map, pipeline_mode=pl.Buffered(1))


def _params(n_axes):
    return pltpu.CompilerParams(dimension_semantics=("arbitrary",) * n_axes,
                                vmem_limit_bytes=VMEM_LIMIT)


def _ada_kernel(c_ref, w_ref, b_ref, o_ref):
    sc = _silu(c_ref[...]).astype(BF)
    o_ref[...] = _dot(sc, w_ref[...].astype(BF)) + b_ref[...]


def _ada_proj(c_all, w, b, tn=1024):
    nl, d, n = w.shape
    r = c_all.shape[0]
    return pl.pallas_call(
        _ada_kernel,
        grid=(nl, n // tn),
        in_specs=[
            pl.BlockSpec((r, d), lambda l, j: (0, 0)),
            pl.BlockSpec((None, d, tn), lambda l, j: (l, 0, j)),
            pl.BlockSpec((None, 1, tn), lambda l, j: (l, 0, j)),
        ],
        out_specs=pl.BlockSpec((None, r, tn), lambda l, j: (l, 0, j)),
        out_shape=jax.ShapeDtypeStruct((nl, r, n), F32),
        compiler_params=_params(2),
        name="ada_proj",
    )(c_all, w, b.reshape(nl, 1, n))


def _ffn_kernel(x_ref, mod_ref, g_ref, win_ref, wout_ref, o_ref, *, m0, g0, fc):
    nb, ts, d = x_ref.shape
    x = x_ref[...]
    shift = mod_ref[:, m0:m0 + 1, :]
    gain_in = g_ref[g0:g0 + 1, :] * (1.0 + mod_ref[:, m0 + 1:m0 + 2, :])
    gain_out = g_ref[g0 + 1:g0 + 2, :] * (HALF * mod_ref[:, m0 + 2:m0 + 3, :])
    h = _rmsn(x) * gain_in + shift
    hb = h.reshape(nb * ts, d).astype(BF)
    acc = jnp.zeros((nb * ts, d), F32)
    for c in range(D_FF // fc):
        a = _dot(hb, win_ref[:, c * fc:(c + 1) * fc])
        b = _dot(hb, win_ref[:, D_FF + c * fc:D_FF + (c + 1) * fc])
        u = (_silu(a) * b).astype(BF)
        acc = acc + _dot(u, wout_ref[c * fc:(c + 1) * fc, :])
    o_ref[...] = x + _rmsn(acc).reshape(nb, ts, d) * gain_out


def _ffn(x, mod, mod_row0, g, w_in, w_out, *, layer, idx, nb, ts):
    b, s, d = x.shape
    nbt, nst = b // nb, s // ts
    mrow = mod_row0 // nb
    kern = functools.partial(_ffn_kernel, m0=6 * idx, g0=4 * idx, fc=256)
    return pl.pallas_call(
        kern,
        grid=(nbt, nst),
        in_specs=[
            pl.BlockSpec((nb, ts, d), lambda i, j: (i, j, 0)),
            pl.BlockSpec((None, nb, N_MOD, d), lambda i, j: (layer, mrow + i, 0, 0)),
            pl.BlockSpec((None, 6, d), lambda i, j: (layer, 0, 0)),
            _resident((None, None, d, 2 * D_FF), lambda i, j: (layer, idx, 0, 0)),
            _resident((None, None, D_FF, d), lambda i, j: (layer, idx, 0, 0)),
        ],
        out_specs=pl.BlockSpec((nb, ts, d), lambda i, j: (i, j, 0)),
        out_shape=jax.ShapeDtypeStruct(x.shape, F32),
        compiler_params=_params(2),
        name="ffn",
    )(x, mod, g, w_in, w_out)


def _decay_mask(lg, t):
    i = lax.broadcasted_iota(jnp.int32, (t, t), 0)
    j = lax.broadcasted_iota(jnp.int32, (t, t), 1)
    dist = jnp.abs(i - j).astype(F32)
    vis = (j >> CHUNK_SHIFT) <= (i >> CHUNK_SHIFT)
    return jnp.where(vis, jnp.exp(lg * dist), 0.0)


def _ret_kernel(*refs, nb, t, from_input_state):
    if from_input_state:
        (x_ref, mod_ref, g_ref, win_ref, ggn_ref, wout_ref, cos_ref, sin_ref, sin_st_ref,
         o_ref, so_ref, q_scr, k_scr, v_scr, og_scr, gat_scr, mask_scr, st_scr) = refs
    else:
        (x_ref, mod_ref, g_ref, win_ref, ggn_ref, wout_ref, cos_ref, sin_ref,
         o_ref, so_ref, q_scr, k_scr, v_scr, og_scr, gat_scr, mask_scr, st_scr) = refs
        sin_st_ref = None
    d = D_MODEL
    r = nb * t
    j = pl.program_id(1)

    @pl.when(j == 0)
    def _():
        for hd in range(H_A):
            mask_scr[hd] = _decay_mask(LOG_GAMMA[hd], t)
        if not from_input_state:
            st_scr[...] = jnp.zeros_like(st_scr)

    x = x_ref[...]
    shift = mod_ref[:, 3:4, :]
    scale = mod_ref[:, 4:5, :]
    gate = mod_ref[:, 5:6, :]
    h = _rmsn(x) * g_ref[2:3, :] * (1.0 + scale) + shift
    hb = h.reshape(r, d).astype(BF)

    cos = cos_ref[...]
    sin = sin_ref[...]
    qk = _dot(hb, win_ref[:, 0:2 * QK_A])
    for c in range(H_A):
        qc = qk[:, c * DK_A:(c + 1) * DK_A]
        q_scr[:, c * DK_A:(c + 1) * DK_A] = qc * cos + pltpu.roll(qc, DK_A // 2, 1) * sin
        kc = qk[:, QK_A + c * DK_A:QK_A + (c + 1) * DK_A]
        k_scr[:, c * DK_A:(c + 1) * DK_A] = (kc * cos + pltpu.roll(kc, DK_A // 2, 1) * sin) * (DK_A ** -0.5)
    v_scr[...] = _dot(hb, win_ref[:, 2 * QK_A:2 * QK_A + V_A]).astype(BF)
    gp = _dot(hb, win_ref[:, 2 * QK_A + V_A:2 * QK_A + 2 * V_A])

    row = lax.broadcasted_iota(jnp.int32, (t, DK_A), 0).astype(F32)
    for hd in range(H_A):
        lg = LOG_GAMMA[hd]
        qdec = jnp.exp(lg * row)
        kdec = jnp.exp(lg * (float(t) - row))
        cdec = math.exp(lg * t)
        mask = mask_scr[hd]

        def one_batch(n, carry, hd=hd, qdec=qdec, kdec=kdec, cdec=cdec, mask=mask):
            rows = pl.ds(n * t, t)
            q = q_scr[rows, hd * DK_A:(hd + 1) * DK_A]
            k = k_scr[rows, hd * DK_A:(hd + 1) * DK_A]
            v = v_scr[rows, hd * DV_A:(hd + 1) * DV_A]
            if from_input_state:
                st = sin_st_ref[n, hd]
            else:
                st = st_scr[hd]
            s = _dot_nt(q.astype(BF), k.astype(BF)) * mask
            o = _dot(s.astype(BF), v) + _dot((q * qdec).astype(BF), st.astype(BF))
            st_new = st * cdec + _dot_tn((k * kdec).astype(BF), v)
            og_scr[rows, hd * DV_A:(hd + 1) * DV_A] = o
            if from_input_state:
                so_ref[n, hd] = st_new
            else:
                st_scr[hd] = st_new
            return carry

        for n in range(nb):
            one_batch(n, 0)

    for hd in range(H_A):
        cols = slice(hd * DV_A, (hd + 1) * DV_A)
        on = _rmsn(og_scr[:, cols]) * ggn_ref[:, cols]
        gat_scr[:, cols] = (_silu(gp[:, cols]) * on).astype(BF)
    y = _dot(gat_scr[...], wout_ref[...])
    yn = _rmsn(y) * g_ref[3:4, :]
    o_ref[...] = x + gate * yn.reshape(nb, t, d)

    if not from_input_state:
        @pl.when(j == pl.num_programs(1) - 1)
        def _():
            so_ref[0] = st_scr[...]


def _ret_mixer(x, mod, mod_row0, g, w_in, g_gn, w_out, cos_t, sin_t, state_in, *, nb, t):
    b, s, d = x.shape
    nbt, nst = b // nb, s // t
    r = nb * t
    mrow = mod_row0 // nb
    from_input = state_in is not None
    kern = functools.partial(_ret_kernel, nb=nb, t=t, from_input_state=from_input)
    in_specs = [
        pl.BlockSpec((nb, t, d), lambda i, j: (i, j, 0)),
        pl.BlockSpec((nb, N_MOD, d), lambda i, j: (mrow + i, 0, 0)),
        pl.BlockSpec((6, d), lambda i, j: (0, 0)),
        _resident((d, 2 * QK_A + 2 * V_A), lambda i, j: (0, 0)),
        pl.BlockSpec((1, V_A), lambda i, j: (0, 0)),
        _resident((V_A, d), lambda i, j: (0, 0)),
        pl.BlockSpec((r, DK_A), lambda i, j: (j, 0)),
        pl.BlockSpec((r, DK_A), lambda i, j: (j, 0)),
    ]
    args = [x, mod, g, w_in, g_gn.reshape(1, V_A), w_out, cos_t, sin_t]
    if from_input:
        in_specs.append(pl.BlockSpec((nb, H_A, DK_A, DV_A), lambda i, j: (i, 0, 0, 0)))
        args.append(state_in)
    out, st = pl.pallas_call(
        kern,
        grid=(nbt, nst),
        in_specs=in_specs,
        out_specs=[
            pl.BlockSpec((nb, t, d), lambda i, j: (i, j, 0)),
            pl.BlockSpec((nb, H_A, DK_A, DV_A), lambda i, j: (i, 0, 0, 0)),
        ],
        out_shape=[
            jax.ShapeDtypeStruct(x.shape, F32),
            jax.ShapeDtypeStruct((b, H_A, DK_A, DV_A), F32),
        ],
        scratch_shapes=[
            pltpu.VMEM((r, QK_A), F32),
            pltpu.VMEM((r, QK_A), F32),
            pltpu.VMEM((r, V_A), BF),
            pltpu.VMEM((r, V_A), F32),
            pltpu.VMEM((r, V_A), BF),
            pltpu.VMEM((H_A, t, t), F32),
            pltpu.VMEM((H_A, DK_A, DV_A), F32),
        ],
        compiler_params=_params(2),
        name="ret_mixer",
    )(*args)
    return out, st


def _rope64(blk, cos, sa, sb):
    return blk * cos + pltpu.roll(blk, DH_B // 2, 1) * sa + pltpu.roll(blk, LANES - DH_B // 2, 1) * sb


def _kv_kernel(x_ref, mod_ref, g_ref, w_ref, cos_ref, sa_ref, sb_ref, k_ref, v_ref, kb_ref, vb_ref, *, vt_block):
    nb, ts, d = x_ref.shape
    x = x_ref[...]
    shift = mod_ref[:, 0:1, :]
    scale = mod_ref[:, 1:2, :]
    h = _rmsn(x) * g_ref[...] * (1.0 + scale) + shift
    hb = h.reshape(nb * ts, d).astype(BF)
    kv = _dot(hb, w_ref[...])
    cos = cos_ref[...]
    sa = sa_ref[...]
    sb = sb_ref[...]
    ks = [_rope64(kv[:, c * LANES:(c + 1) * LANES], cos, sa, sb) for c in range(d // LANES)]
    k = jnp.concatenate(ks, axis=1).reshape(nb, ts, d)
    v = kv[:, d:2 * d].reshape(nb, ts, d)
    k_ref[...] = k
    v_ref[...] = v
    kb_ref[...] = k.astype(BF)
    if vt_block is None:
        vb_ref[...] = v.astype(BF)
    else:
        vt = kv[:, d:2 * d].T
        ones = jnp.ones((VT_ROWS - HB_W, vt_block), BF)
        for c in range(ts // vt_block):
            for hd in range(H_B):
                vb_ref[0, c, hd, 0:HB_W, :] = vt[hd * HB_W:(hd + 1) * HB_W,
                                                 c * vt_block:(c + 1) * vt_block].astype(BF)
                vb_ref[0, c, hd, HB_W:VT_ROWS, :] = ones


def _shared_kv(x, modkv, mod_row0, g_kv, w_kv, cos_t, sa_t, sb_t, *, nb, ts, vt_block=None):
    b, s, d = x.shape
    nbt, nst = b // nb, s // ts
    r = nb * ts
    mrow = mod_row0 // nb
    xspec = pl.BlockSpec((nb, ts, d), lambda i, j: (i, j, 0))
    tspec = pl.BlockSpec((r, LANES), lambda i, j: (j, 0))
    if vt_block is None:
        vb_spec, vb_shape = xspec, jax.ShapeDtypeStruct(x.shape, BF)
    else:
        assert nb == 1
        vb_spec = pl.BlockSpec((1, ts // vt_block, H_B, VT_ROWS, vt_block), lambda i, j: (i, j, 0, 0, 0))
        vb_shape = jax.ShapeDtypeStruct((b, s // vt_block, H_B, VT_ROWS, vt_block), BF)
    return pl.pallas_call(
        functools.partial(_kv_kernel, vt_block=vt_block),
        grid=(nbt, nst),
        in_specs=[
            xspec,
            pl.BlockSpec((nb, 2, d), lambda i, j: (mrow + i, 0, 0)),
            pl.BlockSpec((1, d), lambda i, j: (0, 0)),
            _resident((d, 2 * d), lambda i, j: (0, 0)),
            tspec, tspec, tspec,
        ],
        out_specs=[xspec, xspec, xspec, vb_spec],
        out_shape=[
            jax.ShapeDtypeStruct(x.shape, F32),
            jax.ShapeDtypeStruct(x.shape, F32),
            jax.ShapeDtypeStruct(x.shape, BF),
            vb_shape,
        ],
        compiler_params=_params(2),
        name="shared_kv",
    )(x, modkv, g_kv.reshape(1, d), w_kv, cos_t, sa_t, sb_t)


def _lambda(lam_ref, lam_init):
    lp = lam_ref[...]
    l1 = jnp.sum(lp[0:1, :] * lp[1:2, :], axis=-1, keepdims=True)
    l2 = jnp.sum(lp[2:3, :] * lp[3:4, :], axis=-1, keepdims=True)
    return jnp.exp(l1) - jnp.exp(l2) + lam_init


def _project_q(hb, wq_ref, cos, sa, sb, q_scr, tq):
    q = _dot(hb, wq_ref[...])
    lane = lax.broadcasted_iota(jnp.int32, (tq, LANES), 1)
    first = lane < DH_B
    for c in range(H_B):
        rq = _rope64(q[:, c * LANES:(c + 1) * LANES], cos, sa, sb) * Q_SCALE
        q_scr[0:tq, c * LANES:(c + 1) * LANES] = jnp.where(first, rq, 0.0).astype(BF)
        q_scr[tq:2 * tq, c * LANES:(c + 1) * LANES] = jnp.where(first, 0.0, rq).astype(BF)


def _finish_head(acc, l, lam, gsub, lam_init, tq):
    o = acc[0:tq] / l[0:tq] - lam * (acc[tq:2 * tq] / l[tq:2 * tq])
    return _rmsn(o) * gsub * (1.0 - lam_init)


def _roll_rows(x, shift):
    n = x.shape[0]
    return jnp.concatenate([x[n - shift:], x[:n - shift]], axis=0)


def _diffp_kernel(x_ref, mod_ref, g_ref, wqt_ref, k_ref, vt_ref, cos_ref, sa_ref, sb_ref, lam_ref,
                  gsub_ref, wout_ref, o_ref, qt_scr, m_scr, acc_scr, oh_scr, s_scr, e_scr, b_scr, *, tq, lam_init):
    qi = pl.program_id(1)

    @pl.when(qi == 0)
    def _():
        kr = lax.broadcasted_iota(jnp.int32, (tq, HB_W), 0)
        kc = lax.broadcasted_iota(jnp.int32, (tq, HB_W), 1)
        e_scr[...] = jnp.where((kr >> CHUNK_SHIFT) == kc, 1.0, 0.0).astype(BF)
        ra = lax.broadcasted_iota(jnp.int32, (HB_W, 2 * tq), 0)
        ci = lax.broadcasted_iota(jnp.int32, (HB_W, 2 * tq), 1)
        qchunk = jnp.where(ci >= tq, ci - tq, ci) >> CHUNK_SHIFT
        hidden = jnp.where(ra < tq // CHUNK, jnp.where(ra > qchunk, NEG, 0.0), 0.0)
        b_scr[...] = hidden.astype(BF)

    x = x_ref[0]
    shift = mod_ref[0, 3:4, :]
    scale = mod_ref[0, 4:5, :]
    gate = mod_ref[0, 5:6, :]
    h = _rmsn(x) * g_ref[2:3, :] * (1.0 + scale) + shift
    ht = h.T.astype(BF)
    cos = cos_ref[...]
    sa = sa_ref[...]
    sb = sb_ref[...]
    first = lax.broadcasted_iota(jnp.int32, (HB_W, tq), 0) < DH_B
    half = H_B // 2
    for hd in range(H_B):
        if hd % half == 0:
            qt4 = _dot(wqt_ref[hd * HB_W:(hd + half) * HB_W, :], ht)
        blk = qt4[(hd % half) * HB_W:(hd % half + 1) * HB_W, :]
        rq = (blk * cos + _roll_rows(blk, DH_B // 2) * sa + _roll_rows(blk, HB_W - DH_B // 2) * sb) * Q_SCALE
        qt_scr[hd] = jnp.concatenate([jnp.where(first, rq, 0.0), jnp.where(first, 0.0, rq)], axis=1).astype(BF)

    def block(kb, diagonal):
        rows = pl.ds(pl.multiple_of(kb * tq, tq), tq)
        for hd in range(H_B):
            cols = slice(hd * HB_W, (hd + 1) * HB_W)
            if diagonal:
                k_aug = jnp.concatenate([k_ref[0, rows, cols], e_scr[...]], axis=1)
                q_aug = jnp.concatenate([qt_scr[hd], b_scr[...]], axis=0)
                s_scr[hd] = _dot(k_aug, q_aug)
            else:
                s_scr[hd] = _dot(k_ref[0, rows, cols], qt_scr[hd])
        for hd in range(H_B):
            s = s_scr[hd]
            if diagonal:
                m_new = jnp.max(s, axis=0, keepdims=True)
                acc_scr[hd] = _dot(vt_ref[0, kb, hd], jnp.exp2(s - m_new).astype(BF))
            else:
                m_prev = m_scr[hd]
                m_new = jnp.maximum(m_prev, jnp.max(s, axis=0, keepdims=True))
                p = jnp.exp2(s - m_new).astype(BF)
                acc_scr[hd] = jnp.exp2(m_prev - m_new) * acc_scr[hd] + _dot(vt_ref[0, kb, hd], p)
            m_scr[hd] = m_new

    block(qi, True)

    def body(kb, carry):
        block(kb, False)
        return carry

    lax.fori_loop(0, qi, body, 0)

    lam = _lambda(lam_ref, lam_init)
    for hd in range(H_B):
        acc = acc_scr[hd, 0:HB_W, :]
        inv_l = 1.0 / acc_scr[hd, HB_W:HB_W + 1, :]
        o = acc[:, 0:tq] * inv_l[:, 0:tq] - lam * (acc[:, tq:2 * tq] * inv_l[:, tq:2 * tq])
        on = o * lax.rsqrt(jnp.mean(o * o, axis=0, keepdims=True) + EPS)
        oh_scr[hd * HB_W:(hd + 1) * HB_W, :] = (on * gsub_ref[...] * (1.0 - lam_init)).astype(BF)

    y = _dot_tn(oh_scr[...], wout_ref[...])
    o_ref[0] = x + gate * (_rmsn(y) * g_ref[3:4, :])


def _diff_prompt(x, mod, mod_row0, g, w_qt, kb, vtb, cos_t, sa_t, sb_t, lam_p, g_sub, w_out, lam_init, *, tq):
    b, s, d = x.shape
    kern = functools.partial(_diffp_kernel, tq=tq, lam_init=lam_init)
    tspec = pl.BlockSpec((HB_W, tq), lambda i, j: (0, j))
    return pl.pallas_call(
        kern,
        grid=(b, s // tq),
        in_specs=[
            pl.BlockSpec((1, tq, d), lambda i, j: (i, j, 0)),
            pl.BlockSpec((1, N_MOD, d), lambda i, j: (mod_row0 + i, 0, 0)),
            pl.BlockSpec((6, d), lambda i, j: (0, 0)),
            _resident((d, d), lambda i, j: (0, 0)),
            pl.BlockSpec((1, s, d), lambda i, j: (i, 0, 0)),
            pl.BlockSpec((1, s // tq, H_B, VT_ROWS, tq), lambda i, j: (i, 0, 0, 0, 0)),
            tspec, tspec, tspec,
            pl.BlockSpec((4, DH_B), lambda i, j: (0, 0)),
            pl.BlockSpec((HB_W, 1), lambda i, j: (0, 0)),
            _resident((d, d), lambda i, j: (0, 0)),
        ],
        out_specs=pl.BlockSpec((1, tq, d), lambda i, j: (i, j, 0)),
        out_shape=jax.ShapeDtypeStruct(x.shape, F32),
        scratch_shapes=[
            pltpu.VMEM((H_B, HB_W, 2 * tq), BF),
            pltpu.VMEM((H_B, 1, 2 * tq), F32),
            pltpu.VMEM((H_B, VT_ROWS, 2 * tq), F32),
            pltpu.VMEM((d, tq), BF),
            pltpu.VMEM((H_B, tq, 2 * tq), F32),
            pltpu.VMEM((tq, HB_W), BF),
            pltpu.VMEM((HB_W, 2 * tq), BF),
        ],
        compiler_params=_params(2),
        name="diff_prompt",
    )(x, mod, g, w_qt, kb, vtb, cos_t, sa_t, sb_t, lam_p, g_sub.reshape(HB_W, 1), w_out)


def _diffs_kernel(x_ref, mod_ref, g_ref, wq_ref, kn_ref, vn_ref, kc_ref, vc_ref, cos_ref, sa_ref, sb_ref,
                  lam_ref, gsub_ref, wout_ref, o_ref, q_scr, m_scr, acc_scr, oh_scr, s_scr, *, tq, lam_init):
    kc = pl.program_id(1)
    rows2 = 2 * tq

    def with_ones(v):
        return jnp.concatenate([v, jnp.ones(v.shape, BF)], axis=1)

    @pl.when(kc == 0)
    def _():
        x = x_ref[0]
        h = _rmsn(x) * g_ref[2:3, :] * (1.0 + mod_ref[0, 4:5, :]) + mod_ref[0, 3:4, :]
        _project_q(h.astype(BF), wq_ref, cos_ref[...], sa_ref[...], sb_ref[...], q_scr, tq)
        for hd in range(H_B):
            cols = slice(hd * HB_W, (hd + 1) * HB_W)
            s = _dot_nt(q_scr[:, cols], kn_ref[0, :, cols])
            m0 = jnp.max(s, axis=-1, keepdims=True)
            m_scr[hd] = jnp.broadcast_to(m0, (rows2, LANES))
            acc_scr[hd] = _dot(jnp.exp2(s - m0).astype(BF), with_ones(vn_ref[0, :, cols]))

    tkc = kc_ref.shape[3]
    for hd in range(H_B):
        s_scr[hd] = jnp.concatenate(
            [_dot(q_scr[t * tq:(t + 1) * tq, (2 * hd + t) * DH_B:(2 * hd + t + 1) * DH_B],
                  kc_ref[0, 2 * hd + t].astype(BF)) for t in range(2)], axis=0)
    for hd in range(H_B):
        s = s_scr[hd]
        v_h = with_ones(vc_ref[0, pl.ds(hd, tkc, stride=H_B), :].astype(BF))
        m_prev = m_scr[hd][:, 0:1]
        m_new = jnp.maximum(m_prev, jnp.max(s, axis=-1, keepdims=True))
        acc_scr[hd] = jnp.exp2(m_prev - m_new) * acc_scr[hd] + _dot(jnp.exp2(s - m_new).astype(BF), v_h)
        m_scr[hd] = jnp.broadcast_to(m_new, (rows2, LANES))

    @pl.when(kc == pl.num_programs(1) - 1)
    def _():
        lam = _lambda(lam_ref, lam_init)
        for hd in range(H_B):
            cols = slice(hd * HB_W, (hd + 1) * HB_W)
            oh_scr[:, cols] = _finish_head(acc_scr[hd, :, 0:HB_W], acc_scr[hd, :, HB_W:HB_W + 1], lam,
                                           gsub_ref[...], lam_init, tq).astype(BF)
        y = _dot(oh_scr[...], wout_ref[...])
        o_ref[0] = x_ref[0] + mod_ref[0, 5:6, :] * (_rmsn(y) * g_ref[3:4, :])


def _diff_sample(x, mod, mod_row0, g, w_q, kn, vn, cache_k, cache_v, cos_t, sa_t, sb_t, lam_p, g_sub, w_out,
                 lam_init, *, tkc):
    b, tq, d = x.shape
    past = cache_k.shape[1]
    cache_kt = jnp.transpose(cache_k, (0, 2, 3, 1))
    cache_vf = cache_v.reshape(b, past * H_B, HB_W)
    kern = functools.partial(_diffs_kernel, tq=tq, lam_init=lam_init)
    tspec = pl.BlockSpec((tq, LANES), lambda i, j: (0, 0))
    xspec = pl.BlockSpec((1, tq, d), lambda i, j: (i, 0, 0))
    kspec = pl.BlockSpec((1, 2 * H_B, DH_B, tkc), lambda i, j: (i, 0, 0, j))
    vspec = pl.BlockSpec((1, tkc * H_B, HB_W), lambda i, j: (i, j, 0))
    return pl.pallas_call(
        kern,
        grid=(b, past // tkc),
        in_specs=[
            xspec,
            pl.BlockSpec((1, N_MOD, d), lambda i, j: (mod_row0 + i, 0, 0)),
            pl.BlockSpec((6, d), lambda i, j: (0, 0)),
            _resident((d, d), lambda i, j: (0, 0)),
            xspec, xspec, kspec, vspec,
            tspec, tspec, tspec,
            pl.BlockSpec((4, DH_B), lambda i, j: (0, 0)),
            pl.BlockSpec((1, HB_W), lambda i, j: (0, 0)),
            _resident((d, d), lambda i, j: (0, 0)),
        ],
        out_specs=xspec,
        out_shape=jax.ShapeDtypeStruct(x.shape, F32),
        scratch_shapes=[
            pltpu.VMEM((2 * tq, d), BF),
            pltpu.VMEM((H_B, 2 * tq, LANES), F32),
            pltpu.VMEM((H_B, 2 * tq, 2 * HB_W), F32),
            pltpu.VMEM((tq, d), BF),
            pltpu.VMEM((H_B, 2 * tq, tkc), F32),
        ],
        compiler_params=_params(2),
        name="diff_sample",
    )(x, mod, g, w_q, kn, vn, cache_kt, cache_vf, cos_t, sa_t, sb_t, lam_p, g_sub.reshape(1, HB_W), w_out)


def _rope_tables_a(pos):
    inv = jnp.power(ROPE_THETA, -jnp.arange(0, DK_A, 2, dtype=jnp.float32) / DK_A)
    ang = pos[:, None] * inv[None, :]
    cos, sin = jnp.cos(ang), jnp.sin(ang)
    return jnp.concatenate([cos, cos], axis=-1), jnp.concatenate([-sin, sin], axis=-1)


def _rope_tables_b(pos):
    inv = jnp.power(ROPE_THETA, -jnp.arange(0, DH_B, 2, dtype=jnp.float32) / DH_B)
    ang = pos[:, None] * inv[None, :]
    cos, sin = jnp.cos(ang), jnp.sin(ang)
    zero = jnp.zeros_like(sin)
    cos_t = jnp.concatenate([cos, cos, cos, cos], axis=-1)
    sa = jnp.concatenate([zero, sin, zero, sin], axis=-1)
    sb = jnp.concatenate([-sin, zero, -sin, zero], axis=-1)
    return cos_t, sa, sb


def kernel(x_prompt, x_sample, state_ret, cache_k, cache_v, c_prompt, c_sample, w_ada, b_ada, g_norm, w_ffn_in,
           w_ffn_out, w_in_a, g_gn_a, w_out_a, w_ada_kv, b_ada_kv, g_kv, w_kv, w_q_b, lam_b, g_subln_b, w_out_b):
    d = D_MODEL
    bp, sp, _ = x_prompt.shape
    bs, ss, _ = x_sample.shape

    n_rows = -(-(bs + bp) // 16) * 16
    c_all = jnp.concatenate([c_sample, c_prompt, jnp.zeros((n_rows - bs - bp, d), F32)], axis=0)
    mod = _ada_proj(c_all, w_ada, b_ada).reshape(2, n_rows, N_MOD, d)
    modkv = _ada_proj(c_all, w_ada_kv[None], b_ada_kv[None]).reshape(n_rows, 2, d)
    row_s, row_p = 0, bs

    wfi = w_ffn_in.astype(BF)
    wfo = w_ffn_out.astype(BF)
    wia = w_in_a.astype(BF)
    woa = w_out_a.astype(BF)
    wkv = w_kv.astype(BF)
    wqb = w_q_b.astype(BF)
    wob = w_out_b.astype(BF)

    pos_p = jnp.arange(sp, dtype=jnp.float32)
    pos_s = PAST_LEN + jnp.arange(ss, dtype=jnp.float32)
    nb_s = 16
    nb_r = 4
    ts_p = 512
    t_ret = 256
    tq = 256

    cos_ap, sin_ap = _rope_tables_a(pos_p)
    cos_as, sin_as = [jnp.tile(a, (nb_r, 1)) for a in _rope_tables_a(pos_s)]
    tab_bp = _rope_tables_b(pos_p)
    tab_bs = _rope_tables_b(pos_s)
    tab_bs_tiled = [jnp.tile(a, (nb_s, 1)) for a in tab_bs]

    def ffn_p(x, l, i):
        return _ffn(x, mod, row_p, g_norm, wfi, wfo, layer=l, idx=i, nb=1, ts=ts_p)

    def ffn_s(x, l, i):
        return _ffn(x, mod, row_s, g_norm, wfi, wfo, layer=l, idx=i, nb=nb_s, ts=ss)

    xp = ffn_p(x_prompt, 0, 0)
    xs = ffn_s(x_sample, 0, 0)
    xp, ret_p = _ret_mixer(xp, mod[0], row_p, g_norm[0], wia[0], g_gn_a[0], woa[0], cos_ap, sin_ap, None,
                           nb=1, t=t_ret)
    xs, ret_s = _ret_mixer(xs, mod[0], row_s, g_norm[0], wia[0], g_gn_a[0], woa[0], cos_as, sin_as, state_ret[0],
                           nb=nb_r, t=ss)
    xp = ffn_p(xp, 0, 1)
    xs = ffn_s(xs, 0, 1)
    k_p, v_p, kb_p, vtb_p = _shared_kv(xp, modkv, row_p, g_kv, wkv, *tab_bp, nb=1, ts=ts_p, vt_block=tq)
    k_s, v_s, kb_s, vb_s = _shared_kv(xs, modkv, row_s, g_kv, wkv, *tab_bs_tiled, nb=nb_s, ts=ss)

    lam_init = 0.8 - 0.6 * math.exp(-0.3 * 1)
    xp = ffn_p(xp, 1, 0)
    xs = ffn_s(xs, 1, 0)
    xp = _diff_prompt(xp, mod[1], row_p, g_norm[1], w_q_b[0].T.astype(BF), kb_p, vtb_p,
                      *[a.T for a in tab_bp], lam_b[0], g_subln_b[0], wob[0],
                      lam_init, tq=tq)
    xs = _diff_sample(xs, mod[1], row_s, g_norm[1], wqb[0], kb_s, vb_s,
                      cache_k, cache_v, *tab_bs, lam_b[0], g_subln_b[0], wob[0], lam_init, tkc=2048)
    xp = ffn_p(xp, 1, 1)
    xs = ffn_s(xs, 1, 1)

    return (xp, xs,
            ret_p[None], k_p.reshape(bp, sp, 2 * H_B, DH_B), v_p.reshape(bp, sp, H_B, 2 * DH_B),
            ret_s[None], k_s.reshape(bs, ss, 2 * H_B, DH_B), v_s.reshape(bs, ss, H_B, 2 * DH_B))
```

```python
import functools
import math

import jax
import jax.numpy as jnp
from jax import lax
from jax.experimental import pallas as pl
from jax.experimental.pallas import tpu as pltpu

D_MODEL = 1024
CHUNK = 64
CHUNK_SHIFT = 6
H_A = 8
DK_A = 128
DV_A = 256
QK_A = H_A * DK_A
V_A = H_A * DV_A
H_B = 8
DH_B = 64
HB_W = 2 * DH_B
D_FF = 2816
N_MOD = 9
ROPE_THETA = 10000.0
EPS = 1e-6
HALF = 0.5
PAST_LEN = 4096
NEG = -1e30
LOG2E = 1.4426950408889634
Q_SCALE = DH_B ** -0.5 * LOG2E
VT_ROWS = HB_W + 16

LANES = 128
VMEM_LIMIT = 56 * 1024 * 1024

BF = jnp.bfloat16
F32 = jnp.float32

LOG_GAMMA = [math.log1p(-(2.0 ** (-5 - h))) for h in range(H_A)]


def _dot(a, b):
    return jnp.dot(a, b, preferred_element_type=F32)


def _dot_nt(a, b):
    return lax.dot_general(a, b, (((1,), (1,)), ((), ())), preferred_element_type=F32)


def _dot_tn(a, b):
    return lax.dot_general(a, b, (((0,), (0,)), ((), ())), preferred_element_type=F32)


def _rmsn(x):
    return x * lax.rsqrt(jnp.mean(x * x, axis=-1, keepdims=True) + EPS)


def _silu(x):
    return x * jax.nn.sigmoid(x)


def _resident(shape, index_map):
    return pl.BlockSpec(shape, index_map, pipeline_mode=pl.Buffered(1))


def _params(n_axes):
    return pltpu.CompilerParams(dimension_semantics=("arbitrary",) * n_axes,
                                vmem_limit_bytes=VMEM_LIMIT)


def _ada_kernel(c_ref, w_ref, b_ref, o_ref):
    sc = _silu(c_ref[...]).astype(BF)
    o_ref[...] = _dot(sc, w_ref[...].astype(BF)) + b_ref[...]


def _ada_proj(c_all, w, b, tn=1024):
    nl, d, n = w.shape
    r = c_all.shape[0]
    return pl.pallas_call(
        _ada_kernel,
        grid=(nl, n // tn),
        in_specs=[
            pl.BlockSpec((r, d), lambda l, j: (0, 0)),
            pl.BlockSpec((None, d, tn), lambda l, j: (l, 0, j)),
            pl.BlockSpec((None, 1, tn), lambda l, j: (l, 0, j)),
        ],
        out_specs=pl.BlockSpec((None, r, tn), lambda l, j: (l, 0, j)),
        out_shape=jax.ShapeDtypeStruct((nl, r, n), F32),
        compiler_params=_params(2),
        name="ada_proj",
    )(c_all, w, b.reshape(nl, 1, n))


def _ffn_kernel(x_ref, mod_ref, g_ref, win_ref, wout_ref, *rest, m0, g0, fc, n_cast):
    cast_src, o_ref, cast_dst = rest[:n_cast], rest[n_cast], rest[n_cast + 1:]
    for src, dst in zip(cast_src, cast_dst):
        dst[...] = src[...].astype(BF)
    nb, ts, d = x_ref.shape
    x = x_ref[...]
    shift = mod_ref[:, m0:m0 + 1, :]
    gain_in = g_ref[g0:g0 + 1, :] * (1.0 + mod_ref[:, m0 + 1:m0 + 2, :])
    gain_out = g_ref[g0 + 1:g0 + 2, :] * (HALF * mod_ref[:, m0 + 2:m0 + 3, :])
    h = _rmsn(x) * gain_in + shift
    hb = h.reshape(nb * ts, d).astype(BF)
    acc = jnp.zeros((nb * ts, d), F32)
    for c in range(D_FF // fc):
        a = _dot(hb, win_ref[:, c * fc:(c + 1) * fc])
        b = _dot(hb, win_ref[:, D_FF + c * fc:D_FF + (c + 1) * fc])
        u = (_silu(a) * b).astype(BF)
        acc = acc + _dot(u, wout_ref[c * fc:(c + 1) * fc, :])
    o_ref[...] = x + _rmsn(acc).reshape(nb, ts, d) * gain_out


def _ffn(x, mod, mod_row0, g, w_in, w_out, wsel, *, layer, idx, nb, ts, casts=()):
    b, s, d = x.shape
    nbt, nst = b // nb, s // ts
    mrow = mod_row0 // nb
    cast_args, cast_specs, cast_out_specs, cast_shapes = [], [], [], []
    for w, row0, rows, step in casts:
        assert row0 % step == 0 and rows % step == 0 and rows // step <= nbt * nst
        first, last = row0 // step, rows // step - 1
        cast_args.append(w)
        cast_specs.append(pl.BlockSpec((step, w.shape[1]),
                                       lambda i, j, first=first, last=last: (first + jnp.minimum(i * nst + j, last), 0)))
        cast_out_specs.append(pl.BlockSpec((step, w.shape[1]),
                                           lambda i, j, last=last: (jnp.minimum(i * nst + j, last), 0)))
        cast_shapes.append(jax.ShapeDtypeStruct((rows, w.shape[1]), BF))
    kern = functools.partial(_ffn_kernel, m0=6 * idx, g0=4 * idx, fc=256, n_cast=len(casts))
    outs = pl.pallas_call(
        kern,
        grid=(nbt, nst),
        in_specs=[
            pl.BlockSpec((nb, ts, d), lambda i, j: (i, j, 0)),
            pl.BlockSpec((None, nb, N_MOD, d), lambda i, j: (layer, mrow + i, 0, 0)),
            pl.BlockSpec((None, 6, d), lambda i, j: (layer, 0, 0)),
            _resident((None, d, 2 * D_FF), lambda i, j: (wsel, 0, 0)),
            _resident((None, D_FF, d), lambda i, j: (wsel, 0, 0)),
        ] + cast_specs,
        out_specs=[pl.BlockSpec((nb, ts, d), lambda i, j: (i, j, 0))] + cast_out_specs,
        out_shape=[jax.ShapeDtypeStruct(x.shape, F32)] + cast_shapes,
        compiler_params=_params(2),
        name="ffn",
    )(x, mod, g, w_in, w_out, *cast_args)
    return outs if casts else outs[0]


def _decay_mask(lg, t):
    i = lax.broadcasted_iota(jnp.int32, (t, t), 0)
    j = lax.broadcasted_iota(jnp.int32, (t, t), 1)
    dist = jnp.abs(i - j).astype(F32)
    vis = (j >> CHUNK_SHIFT) <= (i >> CHUNK_SHIFT)
    return jnp.where(vis, jnp.exp(lg * dist), 0.0)


def _ret_kernel(*refs, nb, t, from_input_state):
    if from_input_state:
        (x_ref, mod_ref, g_ref, win_ref, ggn_ref, wout_ref, cos_ref, sin_ref, sin_st_ref,
         o_ref, so_ref, q_scr, k_scr, v_scr, og_scr, gat_scr, mask_scr, st_scr) = refs
    else:
        (x_ref, mod_ref, g_ref, win_ref, ggn_ref, wout_ref, cos_ref, sin_ref,
         o_ref, so_ref, q_scr, k_scr, v_scr, og_scr, gat_scr, mask_scr, st_scr) = refs
        sin_st_ref = None
    d = D_MODEL
    r = nb * t
    j = pl.program_id(1)

    @pl.when(j == 0)
    def _():
        for hd in range(H_A):
            mask_scr[hd] = _decay_mask(LOG_GAMMA[hd], t)
        if not from_input_state:
            st_scr[...] = jnp.zeros_like(st_scr)

    x = x_ref[...]
    shift = mod_ref[:, 3:4, :]
    scale = mod_ref[:, 4:5, :]
    gate = mod_ref[:, 5:6, :]
    h = _rmsn(x) * g_ref[2:3, :] * (1.0 + scale) + shift
    hb = h.reshape(r, d).astype(BF)

    cos = cos_ref[...]
    sin = sin_ref[...]
    qk = _dot(hb, win_ref[:, 0:2 * QK_A])
    for c in range(H_A):
        qc = qk[:, c * DK_A:(c + 1) * DK_A]
        q_scr[:, c * DK_A:(c + 1) * DK_A] = qc * cos + pltpu.roll(qc, DK_A // 2, 1) * sin
        kc = qk[:, QK_A + c * DK_A:QK_A + (c + 1) * DK_A]
        k_scr[:, c * DK_A:(c + 1) * DK_A] = (kc * cos + pltpu.roll(kc, DK_A // 2, 1) * sin) * (DK_A ** -0.5)
    v_scr[...] = _dot(hb, win_ref[:, 2 * QK_A:2 * QK_A + V_A]).astype(BF)
    gp = _dot(hb, win_ref[:, 2 * QK_A + V_A:2 * QK_A + 2 * V_A])

    row = lax.broadcasted_iota(jnp.int32, (t, DK_A), 0).astype(F32)
    for hd in range(H_A):
        lg = LOG_GAMMA[hd]
        qdec = jnp.exp(lg * row)
        kdec = jnp.exp(lg * (float(t) - row))
        cdec = math.exp(lg * t)
        mask = mask_scr[hd]

        def one_batch(n, carry, hd=hd, qdec=qdec, kdec=kdec, cdec=cdec, mask=mask):
            rows = pl.ds(n * t, t)
            q = q_scr[rows, hd * DK_A:(hd + 1) * DK_A]
            k = k_scr[rows, hd * DK_A:(hd + 1) * DK_A]
            v = v_scr[rows, hd * DV_A:(hd + 1) * DV_A]
            if from_input_state:
                st = sin_st_ref[n, hd]
            else:
                st = st_scr[hd]
            s = _dot_nt(q.astype(BF), k.astype(BF)) * mask
            o = _dot(s.astype(BF), v) + _dot((q * qdec).astype(BF), st.astype(BF))
            st_new = st * cdec + _dot_tn((k * kdec).astype(BF), v)
            og_scr[rows, hd * DV_A:(hd + 1) * DV_A] = o
            if from_input_state:
                so_ref[n, hd] = st_new
            else:
                st_scr[hd] = st_new
            return carry

        for n in range(nb):
            one_batch(n, 0)

    for hd in range(H_A):
        cols = slice(hd * DV_A, (hd + 1) * DV_A)
        on = _rmsn(og_scr[:, cols]) * ggn_ref[:, cols]
        gat_scr[:, cols] = (_silu(gp[:, cols]) * on).astype(BF)
    y = _dot(gat_scr[...], wout_ref[...])
    yn = _rmsn(y) * g_ref[3:4, :]
    o_ref[...] = x + gate * yn.reshape(nb, t, d)

    if not from_input_state:
        @pl.when(j == pl.num_programs(1) - 1)
        def _():
            so_ref[0] = st_scr[...]


def _ret_mixer(x, mod, mod_row0, g, w_in, g_gn, w_out, cos_t, sin_t, state_in, *, nb, t):
    b, s, d = x.shape
    nbt, nst = b // nb, s // t
    r = nb * t
    mrow = mod_row0 // nb
    from_input = state_in is not None
    kern = functools.partial(_ret_kernel, nb=nb, t=t, from_input_state=from_input)
    in_specs = [
        pl.BlockSpec((nb, t, d), lambda i, j: (i, j, 0)),
        pl.BlockSpec((nb, N_MOD, d), lambda i, j: (mrow + i, 0, 0)),
        pl.BlockSpec((6, d), lambda i, j: (0, 0)),
        _resident((d, 2 * QK_A + 2 * V_A), lambda i, j: (0, 0)),
        pl.BlockSpec((1, V_A), lambda i, j: (0, 0)),
        _resident((V_A, d), lambda i, j: (0, 0)),
        pl.BlockSpec((r, DK_A), lambda i, j: (j, 0)),
        pl.BlockSpec((r, DK_A), lambda i, j: (j, 0)),
    ]
    args = [x, mod, g, w_in, g_gn.reshape(1, V_A), w_out, cos_t, sin_t]
    if from_input:
        in_specs.append(pl.BlockSpec((nb, H_A, DK_A, DV_A), lambda i, j: (i, 0, 0, 0)))
        args.append(state_in)
    out, st = pl.pallas_call(
        kern,
        grid=(nbt, nst),
        in_specs=in_specs,
        out_specs=[
            pl.BlockSpec((nb, t, d), lambda i, j: (i, j, 0)),
            pl.BlockSpec((nb, H_A, DK_A, DV_A), lambda i, j: (i, 0, 0, 0)),
        ],
        out_shape=[
            jax.ShapeDtypeStruct(x.shape, F32),
            jax.ShapeDtypeStruct((b, H_A, DK_A, DV_A), F32),
        ],
        scratch_shapes=[
            pltpu.VMEM((r, QK_A), F32),
            pltpu.VMEM((r, QK_A), F32),
            pltpu.VMEM((r, V_A), BF),
            pltpu.VMEM((r, V_A), F32),
            pltpu.VMEM((r, V_A), BF),
            pltpu.VMEM((H_A, t, t), F32),
            pltpu.VMEM((H_A, DK_A, DV_A), F32),
        ],
        compiler_params=_params(2),
        name="ret_mixer",
    )(*args)
    return out, st


def _rope64(blk, cos, sa, sb):
    return blk * cos + pltpu.roll(blk, DH_B // 2, 1) * sa + pltpu.roll(blk, LANES - DH_B // 2, 1) * sb


def _kv_kernel(x_ref, mod_ref, g_ref, w_ref, cos_ref, sa_ref, sb_ref, k_ref, v_ref, kb_ref, vb_ref, *, vt_block):
    nb, ts, d = x_ref.shape
    x = x_ref[...]
    shift = mod_ref[:, 0:1, :]
    scale = mod_ref[:, 1:2, :]
    h = _rmsn(x) * g_ref[...] * (1.0 + scale) + shift
    hb = h.reshape(nb * ts, d).astype(BF)
    kv = _dot(hb, w_ref[...])
    cos = cos_ref[...]
    sa = sa_ref[...]
    sb = sb_ref[...]
    ks = [_rope64(kv[:, c * LANES:(c + 1) * LANES], cos, sa, sb) for c in range(d // LANES)]
    k = jnp.concatenate(ks, axis=1).reshape(nb, ts, d)
    v = kv[:, d:2 * d].reshape(nb, ts, d)
    k_ref[...] = k
    v_ref[...] = v
    kb_ref[...] = k.astype(BF)
    if vt_block is None:
        vb_ref[...] = v.astype(BF)
    else:
        vt = kv[:, d:2 * d].T
        ones = jnp.ones((VT_ROWS - HB_W, vt_block), BF)
        for c in range(ts // vt_block):
            for hd in range(H_B):
                vb_ref[0, c, hd, 0:HB_W, :] = vt[hd * HB_W:(hd + 1) * HB_W,
                                                 c * vt_block:(c + 1) * vt_block].astype(BF)
                vb_ref[0, c, hd, HB_W:VT_ROWS, :] = ones


def _shared_kv(x, modkv, mod_row0, g_kv, w_kv, cos_t, sa_t, sb_t, *, nb, ts, vt_block=None):
    b, s, d = x.shape
    nbt, nst = b // nb, s // ts
    r = nb * ts
    mrow = mod_row0 // nb
    xspec = pl.BlockSpec((nb, ts, d), lambda i, j: (i, j, 0))
    tspec = pl.BlockSpec((r, LANES), lambda i, j: (j, 0))
    if vt_block is None:
        vb_spec, vb_shape = xspec, jax.ShapeDtypeStruct(x.shape, BF)
    else:
        assert nb == 1
        vb_spec = pl.BlockSpec((1, ts // vt_block, H_B, VT_ROWS, vt_block), lambda i, j: (i, j, 0, 0, 0))
        vb_shape = jax.ShapeDtypeStruct((b, s // vt_block, H_B, VT_ROWS, vt_block), BF)
    return pl.pallas_call(
        functools.partial(_kv_kernel, vt_block=vt_block),
        grid=(nbt, nst),
        in_specs=[
            xspec,
            pl.BlockSpec((nb, 2, d), lambda i, j: (mrow + i, 0, 0)),
            pl.BlockSpec((1, d), lambda i, j: (0, 0)),
            _resident((d, 2 * d), lambda i, j: (0, 0)),
            tspec, tspec, tspec,
        ],
        out_specs=[xspec, xspec, xspec, vb_spec],
        out_shape=[
            jax.ShapeDtypeStruct(x.shape, F32),
            jax.ShapeDtypeStruct(x.shape, F32),
            jax.ShapeDtypeStruct(x.shape, BF),
            vb_shape,
        ],
        compiler_params=_params(2),
        name="shared_kv",
    )(x, modkv, g_kv.reshape(1, d), w_kv, cos_t, sa_t, sb_t)


def _lambda(lam_ref, lam_init):
    lp = lam_ref[...]
    l1 = jnp.sum(lp[0:1, :] * lp[1:2, :], axis=-1, keepdims=True)
    l2 = jnp.sum(lp[2:3, :] * lp[3:4, :], axis=-1, keepdims=True)
    return jnp.exp(l1) - jnp.exp(l2) + lam_init


def _project_q(hb, wq_ref, cos, sa, sb, q_scr, tq):
    q = _dot(hb, wq_ref[...])
    lane = lax.broadcasted_iota(jnp.int32, (tq, LANES), 1)
    first = lane < DH_B
    for c in range(H_B):
        rq = _rope64(q[:, c * LANES:(c + 1) * LANES], cos, sa, sb) * Q_SCALE
        q_scr[0:tq, c * LANES:(c + 1) * LANES] = jnp.where(first, rq, 0.0).astype(BF)
        q_scr[tq:2 * tq, c * LANES:(c + 1) * LANES] = jnp.where(first, 0.0, rq).astype(BF)


def _finish_head(acc, l, lam, gsub, lam_init, tq):
    o = acc[0:tq] / l[0:tq] - lam * (acc[tq:2 * tq] / l[tq:2 * tq])
    return _rmsn(o) * gsub * (1.0 - lam_init)


def _roll_rows(x, shift):
    n = x.shape[0]
    return jnp.concatenate([x[n - shift:], x[:n - shift]], axis=0)


def _diffp_kernel(x_ref, mod_ref, g_ref, wqt_ref, k_ref, vt_ref, cos_ref, sa_ref, sb_ref, lam_ref,
                  gsub_ref, wout_ref, o_ref, qt_scr, m_scr, acc_scr, oh_scr, s_scr, e_scr, b_scr, *, tq, lam_init):
    qi = pl.program_id(1)

    @pl.when(qi == 0)
    def _():
        kr = lax.broadcasted_iota(jnp.int32, (tq, HB_W), 0)
        kc = lax.broadcasted_iota(jnp.int32, (tq, HB_W), 1)
        e_scr[...] = jnp.where((kr >> CHUNK_SHIFT) == kc, 1.0, 0.0).astype(BF)
        ra = lax.broadcasted_iota(jnp.int32, (HB_W, 2 * tq), 0)
        ci = lax.broadcasted_iota(jnp.int32, (HB_W, 2 * tq), 1)
        qchunk = jnp.where(ci >= tq, ci - tq, ci) >> CHUNK_SHIFT
        hidden = jnp.where(ra < tq // CHUNK, jnp.where(ra > qchunk, NEG, 0.0), 0.0)
        b_scr[...] = hidden.astype(BF)

    x = x_ref[0]
    shift = mod_ref[0, 3:4, :]
    scale = mod_ref[0, 4:5, :]
    gate = mod_ref[0, 5:6, :]
    h = _rmsn(x) * g_ref[2:3, :] * (1.0 + scale) + shift
    ht = h.T.astype(BF)
    cos = cos_ref[...]
    sa = sa_ref[...]
    sb = sb_ref[...]
    first = lax.broadcasted_iota(jnp.int32, (HB_W, tq), 0) < DH_B
    half = H_B // 2
    for hd in range(H_B):
        if hd % half == 0:
            qt4 = _dot(wqt_ref[hd * HB_W:(hd + half) * HB_W, :], ht)
        blk = qt4[(hd % half) * HB_W:(hd % half + 1) * HB_W, :]
        rq = (blk * cos + _roll_rows(blk, DH_B // 2) * sa + _roll_rows(blk, HB_W - DH_B // 2) * sb) * Q_SCALE
        qt_scr[hd] = jnp.concatenate([jnp.where(first, rq, 0.0), jnp.where(first, 0.0, rq)], axis=1).astype(BF)

    def block(kb, diagonal):
        rows = pl.ds(pl.multiple_of(kb * tq, tq), tq)
        for hd in range(H_B):
            cols = slice(hd * HB_W, (hd + 1) * HB_W)
            if diagonal:
                k_aug = jnp.concatenate([k_ref[0, rows, cols], e_scr[...]], axis=1)
                q_aug = jnp.concatenate([qt_scr[hd], b_scr[...]], axis=0)
                s_scr[hd] = _dot(k_aug, q_aug)
            else:
                s_scr[hd] = _dot(k_ref[0, rows, cols], qt_scr[hd])
        for hd in range(H_B):
            s = s_scr[hd]
            if diagonal:
                m_new = jnp.max(s, axis=0, keepdims=True)
                acc_scr[hd] = _dot(vt_ref[0, kb, hd], jnp.exp2(s - m_new).astype(BF))
            else:
                m_prev = m_scr[hd]
                m_new = jnp.maximum(m_prev, jnp.max(s, axis=0, keepdims=True))
                p = jnp.exp2(s - m_new).astype(BF)
                acc_scr[hd] = jnp.exp2(m_prev - m_new) * acc_scr[hd] + _dot(vt_ref[0, kb, hd], p)
            m_scr[hd] = m_new

    block(qi, True)

    def body(kb, carry):
        block(kb, False)
        return carry

    lax.fori_loop(0, qi, body, 0)

    lam = _lambda(lam_ref, lam_init)
    for hd in range(H_B):
        acc = acc_scr[hd, 0:HB_W, :]
        inv_l = 1.0 / acc_scr[hd, HB_W:HB_W + 1, :]
        o = acc[:, 0:tq] * inv_l[:, 0:tq] - lam * (acc[:, tq:2 * tq] * inv_l[:, tq:2 * tq])
        on = o * lax.rsqrt(jnp.mean(o * o, axis=0, keepdims=True) + EPS)
        oh_scr[hd * HB_W:(hd + 1) * HB_W, :] = (on * gsub_ref[...] * (1.0 - lam_init)).astype(BF)

    y = _dot_tn(oh_scr[...], wout_ref[...])
    o_ref[0] = x + gate * (_rmsn(y) * g_ref[3:4, :])


def _diff_prompt(x, mod, mod_row0, g, w_qt, kb, vtb, cos_t, sa_t, sb_t, lam_p, g_sub, w_out, lam_init, *, tq):
    b, s, d = x.shape
    kern = functools.partial(_diffp_kernel, tq=tq, lam_init=lam_init)
    tspec = pl.BlockSpec((HB_W, tq), lambda i, j: (0, j))
    return pl.pallas_call(
        kern,
        grid=(b, s // tq),
        in_specs=[
            pl.BlockSpec((1, tq, d), lambda i, j: (i, j, 0)),
            pl.BlockSpec((1, N_MOD, d), lambda i, j: (mod_row0 + i, 0, 0)),
            pl.BlockSpec((6, d), lambda i, j: (0, 0)),
            _resident((d, d), lambda i, j: (0, 0)),
            pl.BlockSpec((1, s, d), lambda i, j: (i, 0, 0)),
            pl.BlockSpec((1, s // tq, H_B, VT_ROWS, tq), lambda i, j: (i, 0, 0, 0, 0)),
            tspec, tspec, tspec,
            pl.BlockSpec((4, DH_B), lambda i, j: (0, 0)),
            pl.BlockSpec((HB_W, 1), lambda i, j: (0, 0)),
            _resident((d, d), lambda i, j: (0, 0)),
        ],
        out_specs=pl.BlockSpec((1, tq, d), lambda i, j: (i, j, 0)),
        out_shape=jax.ShapeDtypeStruct(x.shape, F32),
        scratch_shapes=[
            pltpu.VMEM((H_B, HB_W, 2 * tq), BF),
            pltpu.VMEM((H_B, 1, 2 * tq), F32),
            pltpu.VMEM((H_B, VT_ROWS, 2 * tq), F32),
            pltpu.VMEM((d, tq), BF),
            pltpu.VMEM((H_B, tq, 2 * tq), F32),
            pltpu.VMEM((tq, HB_W), BF),
            pltpu.VMEM((HB_W, 2 * tq), BF),
        ],
        compiler_params=_params(2),
        name="diff_prompt",
    )(x, mod, g, w_qt, kb, vtb, cos_t, sa_t, sb_t, lam_p, g_sub.reshape(HB_W, 1), w_out)


def _diffs_kernel(x_ref, mod_ref, g_ref, wq_ref, kn_ref, vn_ref, kc_ref, vc_ref, cos_ref, sa_ref, sb_ref,
                  lam_ref, gsub_ref, wout_ref, o_ref, q_scr, m_scr, acc_scr, oh_scr, s_scr, *, tq, lam_init):
    kc = pl.program_id(1)
    rows2 = 2 * tq

    def with_ones(v):
        return jnp.concatenate([v, jnp.ones(v.shape, BF)], axis=1)

    @pl.when(kc == 0)
    def _():
        x = x_ref[0]
        h = _rmsn(x) * g_ref[2:3, :] * (1.0 + mod_ref[0, 4:5, :]) + mod_ref[0, 3:4, :]
        _project_q(h.astype(BF), wq_ref, cos_ref[...], sa_ref[...], sb_ref[...], q_scr, tq)
        for hd in range(H_B):
            cols = slice(hd * HB_W, (hd + 1) * HB_W)
            s = _dot_nt(q_scr[:, cols], kn_ref[0, :, cols])
            m0 = jnp.max(s, axis=-1, keepdims=True)
            m_scr[hd] = jnp.broadcast_to(m0, (rows2, LANES))
            acc_scr[hd] = _dot(jnp.exp2(s - m0).astype(BF), with_ones(vn_ref[0, :, cols]))

    tkc = kc_ref.shape[3]
    for hd in range(H_B):
        s_scr[hd] = jnp.concatenate(
            [_dot(q_scr[t * tq:(t + 1) * tq, (2 * hd + t) * DH_B:(2 * hd + t + 1) * DH_B],
                  kc_ref[0, 2 * hd + t].astype(BF)) for t in range(2)], axis=0)
    for hd in range(H_B):
        s = s_scr[hd]
        v_h = with_ones(vc_ref[0, pl.ds(hd, tkc, stride=H_B), :].astype(BF))
        m_prev = m_scr[hd][:, 0:1]
        m_new = jnp.maximum(m_prev, jnp.max(s, axis=-1, keepdims=True))
        acc_scr[hd] = jnp.exp2(m_prev - m_new) * acc_scr[hd] + _dot(jnp.exp2(s - m_new).astype(BF), v_h)
        m_scr[hd] = jnp.broadcast_to(m_new, (rows2, LANES))

    @pl.when(kc == pl.num_programs(1) - 1)
    def _():
        lam = _lambda(lam_ref, lam_init)
        for hd in range(H_B):
            cols = slice(hd * HB_W, (hd + 1) * HB_W)
            oh_scr[:, cols] = _finish_head(acc_scr[hd, :, 0:HB_W], acc_scr[hd, :, HB_W:HB_W + 1], lam,
                                           gsub_ref[...], lam_init, tq).astype(BF)
        y = _dot(oh_scr[...], wout_ref[...])
        o_ref[0] = x_ref[0] + mod_ref[0, 5:6, :] * (_rmsn(y) * g_ref[3:4, :])


def _diff_sample(x, mod, mod_row0, g, w_q, kn, vn, cache_k, cache_v, cos_t, sa_t, sb_t, lam_p, g_sub, w_out,
                 lam_init, *, tkc):
    b, tq, d = x.shape
    past = cache_k.shape[1]
    cache_kt = jnp.transpose(cache_k, (0, 2, 3, 1))
    cache_vf = cache_v.reshape(b, past * H_B, HB_W)
    kern = functools.partial(_diffs_kernel, tq=tq, lam_init=lam_init)
    tspec = pl.BlockSpec((tq, LANES), lambda i, j: (0, 0))
    xspec = pl.BlockSpec((1, tq, d), lambda i, j: (i, 0, 0))
    kspec = pl.BlockSpec((1, 2 * H_B, DH_B, tkc), lambda i, j: (i, 0, 0, j))
    vspec = pl.BlockSpec((1, tkc * H_B, HB_W), lambda i, j: (i, j, 0))
    return pl.pallas_call(
        kern,
        grid=(b, past // tkc),
        in_specs=[
            xspec,
            pl.BlockSpec((1, N_MOD, d), lambda i, j: (mod_row0 + i, 0, 0)),
            pl.BlockSpec((6, d), lambda i, j: (0, 0)),
            _resident((d, d), lambda i, j: (0, 0)),
            xspec, xspec, kspec, vspec,
            tspec, tspec, tspec,
            pl.BlockSpec((4, DH_B), lambda i, j: (0, 0)),
            pl.BlockSpec((1, HB_W), lambda i, j: (0, 0)),
            _resident((d, d), lambda i, j: (0, 0)),
        ],
        out_specs=xspec,
        out_shape=jax.ShapeDtypeStruct(x.shape, F32),
        scratch_shapes=[
            pltpu.VMEM((2 * tq, d), BF),
            pltpu.VMEM((H_B, 2 * tq, LANES), F32),
            pltpu.VMEM((H_B, 2 * tq, 2 * HB_W), F32),
            pltpu.VMEM((tq, d), BF),
            pltpu.VMEM((H_B, 2 * tq, tkc), F32),
        ],
        compiler_params=_params(2),
        name="diff_sample",
    )(x, mod, g, w_q, kn, vn, cache_kt, cache_vf, cos_t, sa_t, sb_t, lam_p, g_sub.reshape(1, HB_W), w_out)


def _rope_tables_a(pos):
    inv = jnp.power(ROPE_THETA, -jnp.arange(0, DK_A, 2, dtype=jnp.float32) / DK_A)
    ang = pos[:, None] * inv[None, :]
    cos, sin = jnp.cos(ang), jnp.sin(ang)
    return jnp.concatenate([cos, cos], axis=-1), jnp.concatenate([-sin, sin], axis=-1)


def _rope_tables_b(pos):
    inv = jnp.power(ROPE_THETA, -jnp.arange(0, DH_B, 2, dtype=jnp.float32) / DH_B)
    ang = pos[:, None] * inv[None, :]
    cos, sin = jnp.cos(ang), jnp.sin(ang)
    zero = jnp.zeros_like(sin)
    cos_t = jnp.concatenate([cos, cos, cos, cos], axis=-1)
    sa = jnp.concatenate([zero, sin, zero, sin], axis=-1)
    sb = jnp.concatenate([-sin, zero, -sin, zero], axis=-1)
    return cos_t, sa, sb


def kernel(x_prompt, x_sample, state_ret, cache_k, cache_v, c_prompt, c_sample, w_ada, b_ada, g_norm, w_ffn_in,
           w_ffn_out, w_in_a, g_gn_a, w_out_a, w_ada_kv, b_ada_kv, g_kv, w_kv, w_q_b, lam_b, g_subln_b, w_out_b):
    d = D_MODEL
    bp, sp, _ = x_prompt.shape
    bs, ss, _ = x_sample.shape

    n_rows = -(-(bs + bp) // 16) * 16
    c_all = jnp.concatenate([c_sample, c_prompt, jnp.zeros((n_rows - bs - bp, d), F32)], axis=0)
    mod = _ada_proj(c_all, w_ada, b_ada).reshape(2, n_rows, N_MOD, d)
    modkv = _ada_proj(c_all, w_ada_kv[None], b_ada_kv[None]).reshape(n_rows, 2, d)
    row_s, row_p = 0, bs

    wfi0 = w_ffn_in[0, 0].astype(BF)[None]
    wfo0 = w_ffn_out[0, 0].astype(BF)[None]
    wqb = w_q_b.astype(BF)
    wob = w_out_b.astype(BF)
    n_ffn = w_ffn_in.shape[0] * w_ffn_in.shape[1]
    ffn_casts = (
        (w_ffn_in.reshape(n_ffn * d, 2 * D_FF), d, (n_ffn - 1) * d, 128),
        (w_ffn_out.reshape(n_ffn * D_FF, d), D_FF, (n_ffn - 1) * D_FF, 352),
        (w_in_a.reshape(d, 2 * QK_A + 2 * V_A), 0, d, 32),
        (w_out_a.reshape(V_A, d), 0, V_A, 64),
        (w_kv, 0, d, 32),
    )

    pos_p = jnp.arange(sp, dtype=jnp.float32)
    pos_s = PAST_LEN + jnp.arange(ss, dtype=jnp.float32)
    nb_s = 16
    nb_r = 4
    ts_p = 512
    t_ret = 256
    tq = 256

    cos_ap, sin_ap = _rope_tables_a(pos_p)
    cos_as, sin_as = [jnp.tile(a, (nb_r, 1)) for a in _rope_tables_a(pos_s)]
    tab_bp = _rope_tables_b(pos_p)
    tab_bs = _rope_tables_b(pos_s)
    tab_bs_tiled = [jnp.tile(a, (nb_s, 1)) for a in tab_bs]

    xp, wfi_rest, wfo_rest, wia, woa, wkv = _ffn(x_prompt, mod, row_p, g_norm, wfi0, wfo0, 0, layer=0, idx=0,
                                                 nb=1, ts=ts_p, casts=ffn_casts)
    wfi_rest = wfi_rest.reshape(n_ffn - 1, d, 2 * D_FF)
    wfo_rest = wfo_rest.reshape(n_ffn - 1, D_FF, d)

    def ffn_p(x, l, i):
        return _ffn(x, mod, row_p, g_norm, wfi_rest, wfo_rest, 2 * l + i - 1, layer=l, idx=i, nb=1, ts=ts_p)

    def ffn_s(x, l, i):
        if (l, i) == (0, 0):
            return _ffn(x, mod, row_s, g_norm, wfi0, wfo0, 0, layer=l, idx=i, nb=nb_s, ts=ss)
        return _ffn(x, mod, row_s, g_norm, wfi_rest, wfo_rest, 2 * l + i - 1, layer=l, idx=i, nb=nb_s, ts=ss)

    xs = ffn_s(x_sample, 0, 0)
    xp, ret_p = _ret_mixer(xp, mod[0], row_p, g_norm[0], wia, g_gn_a[0], woa, cos_ap, sin_ap, None,
                           nb=1, t=t_ret)
    xs, ret_s = _ret_mixer(xs, mod[0], row_s, g_norm[0], wia, g_gn_a[0], woa, cos_as, sin_as, state_ret[0],
                           nb=nb_r, t=ss)
    xp = ffn_p(xp, 0, 1)
    xs = ffn_s(xs, 0, 1)
    k_p, v_p, kb_p, vtb_p = _shared_kv(xp, modkv, row_p, g_kv, wkv, *tab_bp, nb=1, ts=ts_p, vt_block=tq)
    k_s, v_s, kb_s, vb_s = _shared_kv(xs, modkv, row_s, g_kv, wkv, *tab_bs_tiled, nb=nb_s, ts=ss)

    lam_init = 0.8 - 0.6 * math.exp(-0.3 * 1)
    xp = ffn_p(xp, 1, 0)
    xs = ffn_s(xs, 1, 0)
    xp = _diff_prompt(xp, mod[1], row_p, g_norm[1], w_q_b[0].T.astype(BF), kb_p, vtb_p,
                      *[a.T for a in tab_bp], lam_b[0], g_subln_b[0], wob[0],
                      lam_init, tq=tq)
    xs = _diff_sample(xs, mod[1], row_s, g_norm[1], wqb[0], kb_s, vb_s,
                      cache_k, cache_v, *tab_bs, lam_b[0], g_subln_b[0], wob[0], lam_init, tkc=2048)
    xp = ffn_p(xp, 1, 1)
    xs = ffn_s(xs, 1, 1)

    return (xp, xs,
            ret_p[None], k_p.reshape(bp, sp, 2 * H_B, DH_B), v_p.reshape(bp, sp, H_B, 2 * DH_B),
            ret_s[None], k_s.reshape(bs, ss, 2 * H_B, DH_B), v_s.reshape(bs, ss, H_B, 2 * DH_B))
```

```python
import functools
import math

import jax
import jax.numpy as jnp
from jax import lax
from jax.experimental import pallas as pl
from jax.experimental.pallas import tpu as pltpu

D_MODEL = 1024
CHUNK = 64
CHUNK_SHIFT = 6
H_A = 8
DK_A = 128
DV_A = 256
QK_A = H_A * DK_A
V_A = H_A * DV_A
H_B = 8
DH_B = 64
HB_W = 2 * DH_B
D_FF = 2816
N_MOD = 9
ROPE_THETA = 10000.0
EPS = 1e-6
HALF = 0.5
PAST_LEN = 4096
NEG = -1e30
LOG2E = 1.4426950408889634
Q_SCALE = DH_B ** -0.5 * LOG2E
VT_ROWS = HB_W + 16

LANES = 128
VMEM_LIMIT = 56 * 1024 * 1024

BF = jnp.bfloat16
F32 = jnp.float32

LOG_GAMMA = [math.log1p(-(2.0 ** (-5 - h))) for h in range(H_A)]


def _dot(a, b):
    return jnp.dot(a, b, preferred_element_type=F32)


def _dot_nt(a, b):
    return lax.dot_general(a, b, (((1,), (1,)), ((), ())), preferred_element_type=F32)


def _dot_tn(a, b):
    return lax.dot_general(a, b, (((0,), (0,)), ((), ())), preferred_element_type=F32)


def _rmsn(x):
    return x * lax.rsqrt(jnp.mean(x * x, axis=-1, keepdims=True) + EPS)


def _silu(x):
    return x * jax.nn.sigmoid(x)


def _resident(shape, index_map):
    return pl.BlockSpec(shape, index_map, pipeline_mode=pl.Buffered(1))


def _params(n_axes):
    return pltpu.CompilerParams(dimension_semantics=("arbitrary",) * n_axes,
                                vmem_limit_bytes=VMEM_LIMIT)


def _ada_kernel(c_ref, w_ref, b_ref, o_ref):
    sc = _silu(c_ref[...]).astype(BF)
    o_ref[...] = _dot(sc, w_ref[...].astype(BF)) + b_ref[...]


def _ada_proj(c_all, w, b, tn=1024):
    nl, d, n = w.shape
    r = c_all.shape[0]
    return pl.pallas_call(
        _ada_kernel,
        grid=(nl, n // tn),
        in_specs=[
            pl.BlockSpec((r, d), lambda l, j: (0, 0)),
            pl.BlockSpec((None, d, tn), lambda l, j: (l, 0, j)),
            pl.BlockSpec((None, 1, tn), lambda l, j: (l, 0, j)),
        ],
        out_specs=pl.BlockSpec((None, r, tn), lambda l, j: (l, 0, j)),
        out_shape=jax.ShapeDtypeStruct((nl, r, n), F32),
        compiler_params=_params(2),
        name="ada_proj",
    )(c_all, w, b.reshape(nl, 1, n))


N_KV_IN, N_KV_OUT = 6, 4


def _ffn_kernel(x_ref, mod_ref, g_ref, win_ref, wout_ref, *rest, m0, g0, fc, n_cast, vt_block, with_kv):
    n_in = (N_KV_IN if with_kv else 0) + n_cast
    kv_in, cast_src = rest[:n_in - n_cast], rest[n_in - n_cast:n_in]
    o_ref = rest[n_in]
    kv_out = rest[n_in + 1:n_in + 1 + (N_KV_OUT if with_kv else 0)]
    cast_dst = rest[n_in + 1 + len(kv_out):]
    for src, dst in zip(cast_src, cast_dst):
        dst[...] = src[...].astype(BF)
    nb, ts, d = x_ref.shape
    x = x_ref[...]
    shift = mod_ref[:, m0:m0 + 1, :]
    gain_in = g_ref[g0:g0 + 1, :] * (1.0 + mod_ref[:, m0 + 1:m0 + 2, :])
    gain_out = g_ref[g0 + 1:g0 + 2, :] * (HALF * mod_ref[:, m0 + 2:m0 + 3, :])
    h = _rmsn(x) * gain_in + shift
    hb = h.reshape(nb * ts, d).astype(BF)
    acc = jnp.zeros((nb * ts, d), F32)
    for c in range(D_FF // fc):
        a = _dot(hb, win_ref[:, c * fc:(c + 1) * fc])
        b = _dot(hb, win_ref[:, D_FF + c * fc:D_FF + (c + 1) * fc])
        u = (_silu(a) * b).astype(BF)
        acc = acc + _dot(u, wout_ref[c * fc:(c + 1) * fc, :])
    out = x + _rmsn(acc).reshape(nb, ts, d) * gain_out
    o_ref[...] = out
    if with_kv:
        _kv_tile(out, *kv_in, *kv_out, vt_block=vt_block)


def _ffn(x, mod, mod_row0, g, w_in, w_out, wsel, *, layer, idx, nb, ts, casts=(), kv=None):
    b, s, d = x.shape
    nbt, nst = b // nb, s // ts
    mrow = mod_row0 // nb
    xspec = pl.BlockSpec((nb, ts, d), lambda i, j: (i, j, 0))
    kv_args, kv_specs, kv_out_specs, kv_shapes, vt_block = [], [], [], [], None
    if kv is not None:
        modkv, g_kv, w_kv, tables, vt_block = kv
        tspec = pl.BlockSpec((nb * ts, LANES), lambda i, j: (j, 0))
        kv_args = [modkv, g_kv.reshape(1, d), w_kv, *tables]
        kv_specs = [pl.BlockSpec((nb, 2, d), lambda i, j: (mrow + i, 0, 0)),
                    pl.BlockSpec((1, d), lambda i, j: (0, 0)),
                    _resident((d, 2 * d), lambda i, j: (0, 0)),
                    tspec, tspec, tspec]
        if vt_block is None:
            vb_spec, vb_shape = xspec, jax.ShapeDtypeStruct(x.shape, BF)
        else:
            assert nb == 1
            vb_spec = pl.BlockSpec((1, ts // vt_block, H_B, VT_ROWS, vt_block), lambda i, j: (i, j, 0, 0, 0))
            vb_shape = jax.ShapeDtypeStruct((b, s // vt_block, H_B, VT_ROWS, vt_block), BF)
        kv_out_specs = [xspec, xspec, xspec, vb_spec]
        kv_shapes = [jax.ShapeDtypeStruct(x.shape, F32), jax.ShapeDtypeStruct(x.shape, F32),
                     jax.ShapeDtypeStruct(x.shape, BF), vb_shape]
    cast_args, cast_specs, cast_out_specs, cast_shapes = [], [], [], []
    for w, row0, rows, step in casts:
        assert row0 % step == 0 and rows % step == 0 and rows // step <= nbt * nst
        first, last = row0 // step, rows // step - 1
        cast_args.append(w)
        cast_specs.append(pl.BlockSpec((step, w.shape[1]),
                                       lambda i, j, first=first, last=last: (first + jnp.minimum(i * nst + j, last), 0)))
        cast_out_specs.append(pl.BlockSpec((step, w.shape[1]),
                                           lambda i, j, last=last: (jnp.minimum(i * nst + j, last), 0)))
        cast_shapes.append(jax.ShapeDtypeStruct((rows, w.shape[1]), BF))
    kern = functools.partial(_ffn_kernel, m0=6 * idx, g0=4 * idx, fc=256, n_cast=len(casts),
                             vt_block=vt_block, with_kv=kv is not None)
    outs = pl.pallas_call(
        kern,
        grid=(nbt, nst),
        in_specs=[
            xspec,
            pl.BlockSpec((None, nb, N_MOD, d), lambda i, j: (layer, mrow + i, 0, 0)),
            pl.BlockSpec((None, 6, d), lambda i, j: (layer, 0, 0)),
            _resident((None, d, 2 * D_FF), lambda i, j: (wsel, 0, 0)),
            _resident((None, D_FF, d), lambda i, j: (wsel, 0, 0)),
        ] + kv_specs + cast_specs,
        out_specs=[xspec] + kv_out_specs + cast_out_specs,
        out_shape=[jax.ShapeDtypeStruct(x.shape, F32)] + kv_shapes + cast_shapes,
        compiler_params=_params(2),
        name="ffn_kv" if kv is not None else "ffn",
    )(x, mod, g, w_in, w_out, *kv_args, *cast_args)
    return outs if (casts or kv is not None) else outs[0]


def _decay_mask(lg, t):
    i = lax.broadcasted_iota(jnp.int32, (t, t), 0)
    j = lax.broadcasted_iota(jnp.int32, (t, t), 1)
    dist = jnp.abs(i - j).astype(F32)
    vis = (j >> CHUNK_SHIFT) <= (i >> CHUNK_SHIFT)
    return jnp.where(vis, jnp.exp(lg * dist), 0.0)


def _ret_kernel(*refs, nb, t, from_input_state):
    if from_input_state:
        (x_ref, mod_ref, g_ref, win_ref, ggn_ref, wout_ref, cos_ref, sin_ref, sin_st_ref,
         o_ref, so_ref, q_scr, k_scr, v_scr, og_scr, gat_scr, mask_scr, st_scr) = refs
    else:
        (x_ref, mod_ref, g_ref, win_ref, ggn_ref, wout_ref, cos_ref, sin_ref,
         o_ref, so_ref, q_scr, k_scr, v_scr, og_scr, gat_scr, mask_scr, st_scr) = refs
        sin_st_ref = None
    d = D_MODEL
    r = nb * t
    j = pl.program_id(1)

    @pl.when(j == 0)
    def _():
        for hd in range(H_A):
            mask_scr[hd] = _decay_mask(LOG_GAMMA[hd], t)
        if not from_input_state:
            st_scr[...] = jnp.zeros_like(st_scr)

    x = x_ref[...]
    shift = mod_ref[:, 3:4, :]
    scale = mod_ref[:, 4:5, :]
    gate = mod_ref[:, 5:6, :]
    h = _rmsn(x) * g_ref[2:3, :] * (1.0 + scale) + shift
    hb = h.reshape(r, d).astype(BF)

    cos = cos_ref[...]
    sin = sin_ref[...]
    qk = _dot(hb, win_ref[:, 0:2 * QK_A])
    for c in range(H_A):
        qc = qk[:, c * DK_A:(c + 1) * DK_A]
        q_scr[:, c * DK_A:(c + 1) * DK_A] = qc * cos + pltpu.roll(qc, DK_A // 2, 1) * sin
        kc = qk[:, QK_A + c * DK_A:QK_A + (c + 1) * DK_A]
        k_scr[:, c * DK_A:(c + 1) * DK_A] = (kc * cos + pltpu.roll(kc, DK_A // 2, 1) * sin) * (DK_A ** -0.5)
    v_scr[...] = _dot(hb, win_ref[:, 2 * QK_A:2 * QK_A + V_A]).astype(BF)
    gp = _dot(hb, win_ref[:, 2 * QK_A + V_A:2 * QK_A + 2 * V_A])

    row = lax.broadcasted_iota(jnp.int32, (t, DK_A), 0).astype(F32)
    for hd in range(H_A):
        lg = LOG_GAMMA[hd]
        qdec = jnp.exp(lg * row)
        kdec = jnp.exp(lg * (float(t) - row))
        cdec = math.exp(lg * t)
        mask = mask_scr[hd]

        def one_batch(n, carry, hd=hd, qdec=qdec, kdec=kdec, cdec=cdec, mask=mask):
            rows = pl.ds(n * t, t)
            q = q_scr[rows, hd * DK_A:(hd + 1) * DK_A]
            k = k_scr[rows, hd * DK_A:(hd + 1) * DK_A]
            v = v_scr[rows, hd * DV_A:(hd + 1) * DV_A]
            if from_input_state:
                st = sin_st_ref[n, hd]
            else:
                st = st_scr[hd]
            s = _dot_nt(q.astype(BF), k.astype(BF)) * mask
            o = _dot(s.astype(BF), v) + _dot((q * qdec).astype(BF), st.astype(BF))
            st_new = st * cdec + _dot_tn((k * kdec).astype(BF), v)
            og_scr[rows, hd * DV_A:(hd + 1) * DV_A] = o
            if from_input_state:
                so_ref[n, hd] = st_new
            else:
                st_scr[hd] = st_new
            return carry

        for n in range(nb):
            one_batch(n, 0)

    for hd in range(H_A):
        cols = slice(hd * DV_A, (hd + 1) * DV_A)
        on = _rmsn(og_scr[:, cols]) * ggn_ref[:, cols]
        gat_scr[:, cols] = (_silu(gp[:, cols]) * on).astype(BF)
    y = _dot(gat_scr[...], wout_ref[...])
    yn = _rmsn(y) * g_ref[3:4, :]
    o_ref[...] = x + gate * yn.reshape(nb, t, d)

    if not from_input_state:
        @pl.when(j == pl.num_programs(1) - 1)
        def _():
            so_ref[0] = st_scr[...]


def _ret_mixer(x, mod, mod_row0, g, w_in, g_gn, w_out, cos_t, sin_t, state_in, *, nb, t):
    b, s, d = x.shape
    nbt, nst = b // nb, s // t
    r = nb * t
    mrow = mod_row0 // nb
    from_input = state_in is not None
    kern = functools.partial(_ret_kernel, nb=nb, t=t, from_input_state=from_input)
    in_specs = [
        pl.BlockSpec((nb, t, d), lambda i, j: (i, j, 0)),
        pl.BlockSpec((nb, N_MOD, d), lambda i, j: (mrow + i, 0, 0)),
        pl.BlockSpec((6, d), lambda i, j: (0, 0)),
        _resident((d, 2 * QK_A + 2 * V_A), lambda i, j: (0, 0)),
        pl.BlockSpec((1, V_A), lambda i, j: (0, 0)),
        _resident((V_A, d), lambda i, j: (0, 0)),
        pl.BlockSpec((r, DK_A), lambda i, j: (j, 0)),
        pl.BlockSpec((r, DK_A), lambda i, j: (j, 0)),
    ]
    args = [x, mod, g, w_in, g_gn.reshape(1, V_A), w_out, cos_t, sin_t]
    if from_input:
        in_specs.append(pl.BlockSpec((nb, H_A, DK_A, DV_A), lambda i, j: (i, 0, 0, 0)))
        args.append(state_in)
    out, st = pl.pallas_call(
        kern,
        grid=(nbt, nst),
        in_specs=in_specs,
        out_specs=[
            pl.BlockSpec((nb, t, d), lambda i, j: (i, j, 0)),
            pl.BlockSpec((nb, H_A, DK_A, DV_A), lambda i, j: (i, 0, 0, 0)),
        ],
        out_shape=[
            jax.ShapeDtypeStruct(x.shape, F32),
            jax.ShapeDtypeStruct((b, H_A, DK_A, DV_A), F32),
        ],
        scratch_shapes=[
            pltpu.VMEM((r, QK_A), F32),
            pltpu.VMEM((r, QK_A), F32),
            pltpu.VMEM((r, V_A), BF),
            pltpu.VMEM((r, V_A), F32),
            pltpu.VMEM((r, V_A), BF),
            pltpu.VMEM((H_A, t, t), F32),
            pltpu.VMEM((H_A, DK_A, DV_A), F32),
        ],
        compiler_params=_params(2),
        name="ret_mixer",
    )(*args)
    return out, st


def _rope64(blk, cos, sa, sb):
    return blk * cos + pltpu.roll(blk, DH_B // 2, 1) * sa + pltpu.roll(blk, LANES - DH_B // 2, 1) * sb


def _kv_tile(x, mod_ref, g_ref, w_ref, cos_ref, sa_ref, sb_ref, k_ref, v_ref, kb_ref, vb_ref, *, vt_block):
    nb, ts, d = x.shape
    shift = mod_ref[:, 0:1, :]
    scale = mod_ref[:, 1:2, :]
    h = _rmsn(x) * g_ref[...] * (1.0 + scale) + shift
    hb = h.reshape(nb * ts, d).astype(BF)
    kv = _dot(hb, w_ref[...])
    cos = cos_ref[...]
    sa = sa_ref[...]
    sb = sb_ref[...]
    ks = [_rope64(kv[:, c * LANES:(c + 1) * LANES], cos, sa, sb) for c in range(d // LANES)]
    k = jnp.concatenate(ks, axis=1).reshape(nb, ts, d)
    v = kv[:, d:2 * d].reshape(nb, ts, d)
    k_ref[...] = k
    v_ref[...] = v
    kb_ref[...] = k.astype(BF)
    if vt_block is None:
        vb_ref[...] = v.astype(BF)
    else:
        vt = kv[:, d:2 * d].T
        ones = jnp.ones((VT_ROWS - HB_W, vt_block), BF)
        for c in range(ts // vt_block):
            for hd in range(H_B):
                vb_ref[0, c, hd, 0:HB_W, :] = vt[hd * HB_W:(hd + 1) * HB_W,
                                                 c * vt_block:(c + 1) * vt_block].astype(BF)
                vb_ref[0, c, hd, HB_W:VT_ROWS, :] = ones


def _lambda(lam_ref, lam_init):
    lp = lam_ref[...]
    l1 = jnp.sum(lp[0:1, :] * lp[1:2, :], axis=-1, keepdims=True)
    l2 = jnp.sum(lp[2:3, :] * lp[3:4, :], axis=-1, keepdims=True)
    return jnp.exp(l1) - jnp.exp(l2) + lam_init


def _project_q(hb, wq_ref, cos, sa, sb, q_scr, tq):
    q = _dot(hb, wq_ref[...])
    lane = lax.broadcasted_iota(jnp.int32, (tq, LANES), 1)
    first = lane < DH_B
    for c in range(H_B):
        rq = _rope64(q[:, c * LANES:(c + 1) * LANES], cos, sa, sb) * Q_SCALE
        q_scr[0:tq, c * LANES:(c + 1) * LANES] = jnp.where(first, rq, 0.0).astype(BF)
        q_scr[tq:2 * tq, c * LANES:(c + 1) * LANES] = jnp.where(first, 0.0, rq).astype(BF)


def _finish_head(acc, l, lam, gsub, lam_init, tq):
    o = acc[0:tq] / l[0:tq] - lam * (acc[tq:2 * tq] / l[tq:2 * tq])
    return _rmsn(o) * gsub * (1.0 - lam_init)


def _roll_rows(x, shift):
    n = x.shape[0]
    return jnp.concatenate([x[n - shift:], x[:n - shift]], axis=0)


def _diffp_kernel(x_ref, mod_ref, g_ref, wqt_ref, k_ref, vt_ref, cos_ref, sa_ref, sb_ref, lam_ref,
                  gsub_ref, wout_ref, o_ref, qt_scr, m_scr, acc_scr, oh_scr, s_scr, e_scr, b_scr, *, tq, lam_init):
    qi = pl.program_id(1)

    @pl.when(qi == 0)
    def _():
        kr = lax.broadcasted_iota(jnp.int32, (tq, HB_W), 0)
        kc = lax.broadcasted_iota(jnp.int32, (tq, HB_W), 1)
        e_scr[...] = jnp.where((kr >> CHUNK_SHIFT) == kc, 1.0, 0.0).astype(BF)
        ra = lax.broadcasted_iota(jnp.int32, (HB_W, 2 * tq), 0)
        ci = lax.broadcasted_iota(jnp.int32, (HB_W, 2 * tq), 1)
        qchunk = jnp.where(ci >= tq, ci - tq, ci) >> CHUNK_SHIFT
        hidden = jnp.where(ra < tq // CHUNK, jnp.where(ra > qchunk, NEG, 0.0), 0.0)
        b_scr[...] = hidden.astype(BF)

    x = x_ref[0]
    shift = mod_ref[0, 3:4, :]
    scale = mod_ref[0, 4:5, :]
    gate = mod_ref[0, 5:6, :]
    h = _rmsn(x) * g_ref[2:3, :] * (1.0 + scale) + shift
    ht = h.T.astype(BF)
    cos = cos_ref[...]
    sa = sa_ref[...]
    sb = sb_ref[...]
    first = lax.broadcasted_iota(jnp.int32, (HB_W, tq), 0) < DH_B
    half = H_B // 2
    for hd in range(H_B):
        if hd % half == 0:
            qt4 = _dot(wqt_ref[hd * HB_W:(hd + half) * HB_W, :], ht)
        blk = qt4[(hd % half) * HB_W:(hd % half + 1) * HB_W, :]
        rq = (blk * cos + _roll_rows(blk, DH_B // 2) * sa + _roll_rows(blk, HB_W - DH_B // 2) * sb) * Q_SCALE
        qt_scr[hd] = jnp.concatenate([jnp.where(first, rq, 0.0), jnp.where(first, 0.0, rq)], axis=1).astype(BF)

    def block(kb, diagonal):
        rows = pl.ds(pl.multiple_of(kb * tq, tq), tq)
        for hd in range(H_B):
            cols = slice(hd * HB_W, (hd + 1) * HB_W)
            if diagonal:
                k_aug = jnp.concatenate([k_ref[0, rows, cols], e_scr[...]], axis=1)
                q_aug = jnp.concatenate([qt_scr[hd], b_scr[...]], axis=0)
                s_scr[hd] = _dot(k_aug, q_aug)
            else:
                s_scr[hd] = _dot(k_ref[0, rows, cols], qt_scr[hd])
        for hd in range(H_B):
            s = s_scr[hd]
            if diagonal:
                m_new = jnp.max(s, axis=0, keepdims=True)
                acc_scr[hd] = _dot(vt_ref[0, kb, hd], jnp.exp2(s - m_new).astype(BF))
            else:
                m_prev = m_scr[hd]
                m_new = jnp.maximum(m_prev, jnp.max(s, axis=0, keepdims=True))
                p = jnp.exp2(s - m_new).astype(BF)
                acc_scr[hd] = jnp.exp2(m_prev - m_new) * acc_scr[hd] + _dot(vt_ref[0, kb, hd], p)
            m_scr[hd] = m_new

    block(qi, True)

    def body(kb, carry):
        block(kb, False)
        return carry

    lax.fori_loop(0, qi, body, 0)

    lam = _lambda(lam_ref, lam_init)
    for hd in range(H_B):
        acc = acc_scr[hd, 0:HB_W, :]
        inv_l = 1.0 / acc_scr[hd, HB_W:HB_W + 1, :]
        o = acc[:, 0:tq] * inv_l[:, 0:tq] - lam * (acc[:, tq:2 * tq] * inv_l[:, tq:2 * tq])
        on = o * lax.rsqrt(jnp.mean(o * o, axis=0, keepdims=True) + EPS)
        oh_scr[hd * HB_W:(hd + 1) * HB_W, :] = (on * gsub_ref[...] * (1.0 - lam_init)).astype(BF)

    y = _dot_tn(oh_scr[...], wout_ref[...])
    o_ref[0] = x + gate * (_rmsn(y) * g_ref[3:4, :])


def _diff_prompt(x, mod, mod_row0, g, w_qt, kb, vtb, cos_t, sa_t, sb_t, lam_p, g_sub, w_out, lam_init, *, tq):
    b, s, d = x.shape
    kern = functools.partial(_diffp_kernel, tq=tq, lam_init=lam_init)
    tspec = pl.BlockSpec((HB_W, tq), lambda i, j: (0, j))
    return pl.pallas_call(
        kern,
        grid=(b, s // tq),
        in_specs=[
            pl.BlockSpec((1, tq, d), lambda i, j: (i, j, 0)),
            pl.BlockSpec((1, N_MOD, d), lambda i, j: (mod_row0 + i, 0, 0)),
            pl.BlockSpec((6, d), lambda i, j: (0, 0)),
            _resident((d, d), lambda i, j: (0, 0)),
            pl.BlockSpec((1, s, d), lambda i, j: (i, 0, 0)),
            pl.BlockSpec((1, s // tq, H_B, VT_ROWS, tq), lambda i, j: (i, 0, 0, 0, 0)),
            tspec, tspec, tspec,
            pl.BlockSpec((4, DH_B), lambda i, j: (0, 0)),
            pl.BlockSpec((HB_W, 1), lambda i, j: (0, 0)),
            _resident((d, d), lambda i, j: (0, 0)),
        ],
        out_specs=pl.BlockSpec((1, tq, d), lambda i, j: (i, j, 0)),
        out_shape=jax.ShapeDtypeStruct(x.shape, F32),
        scratch_shapes=[
            pltpu.VMEM((H_B, HB_W, 2 * tq), BF),
            pltpu.VMEM((H_B, 1, 2 * tq), F32),
            pltpu.VMEM((H_B, VT_ROWS, 2 * tq), F32),
            pltpu.VMEM((d, tq), BF),
            pltpu.VMEM((H_B, tq, 2 * tq), F32),
            pltpu.VMEM((tq, HB_W), BF),
            pltpu.VMEM((HB_W, 2 * tq), BF),
        ],
        compiler_params=_params(2),
        name="diff_prompt",
    )(x, mod, g, w_qt, kb, vtb, cos_t, sa_t, sb_t, lam_p, g_sub.reshape(HB_W, 1), w_out)


def _diffs_kernel(x_ref, mod_ref, g_ref, wq_ref, kn_ref, vn_ref, kc_ref, vc_ref, cos_ref, sa_ref, sb_ref,
                  lam_ref, gsub_ref, wout_ref, o_ref, q_scr, m_scr, acc_scr, oh_scr, s_scr, *, tq, lam_init):
    kc = pl.program_id(1)
    rows2 = 2 * tq

    def with_ones(v):
        return jnp.concatenate([v, jnp.ones(v.shape, BF)], axis=1)

    @pl.when(kc == 0)
    def _():
        x = x_ref[0]
        h = _rmsn(x) * g_ref[2:3, :] * (1.0 + mod_ref[0, 4:5, :]) + mod_ref[0, 3:4, :]
        _project_q(h.astype(BF), wq_ref, cos_ref[...], sa_ref[...], sb_ref[...], q_scr, tq)
        for hd in range(H_B):
            cols = slice(hd * HB_W, (hd + 1) * HB_W)
            s = _dot_nt(q_scr[:, cols], kn_ref[0, :, cols])
            m0 = jnp.max(s, axis=-1, keepdims=True)
            m_scr[hd] = jnp.broadcast_to(m0, (rows2, LANES))
            acc_scr[hd] = _dot(jnp.exp2(s - m0).astype(BF), with_ones(vn_ref[0, :, cols]))

    tkc = kc_ref.shape[3]
    for hd in range(H_B):
        s_scr[hd] = jnp.concatenate(
            [_dot(q_scr[t * tq:(t + 1) * tq, (2 * hd + t) * DH_B:(2 * hd + t + 1) * DH_B],
                  kc_ref[0, 2 * hd + t].astype(BF)) for t in range(2)], axis=0)
    for hd in range(H_B):
        s = s_scr[hd]
        v_h = with_ones(vc_ref[0, pl.ds(hd, tkc, stride=H_B), :].astype(BF))
        m_prev = m_scr[hd][:, 0:1]
        m_new = jnp.maximum(m_prev, jnp.max(s, axis=-1, keepdims=True))
        acc_scr[hd] = jnp.exp2(m_prev - m_new) * acc_scr[hd] + _dot(jnp.exp2(s - m_new).astype(BF), v_h)
        m_scr[hd] = jnp.broadcast_to(m_new, (rows2, LANES))

    @pl.when(kc == pl.num_programs(1) - 1)
    def _():
        lam = _lambda(lam_ref, lam_init)
        for hd in range(H_B):
            cols = slice(hd * HB_W, (hd + 1) * HB_W)
            oh_scr[:, cols] = _finish_head(acc_scr[hd, :, 0:HB_W], acc_scr[hd, :, HB_W:HB_W + 1], lam,
                                           gsub_ref[...], lam_init, tq).astype(BF)
        y = _dot(oh_scr[...], wout_ref[...])
        o_ref[0] = x_ref[0] + mod_ref[0, 5:6, :] * (_rmsn(y) * g_ref[3:4, :])


def _diff_sample(x, mod, mod_row0, g, w_q, kn, vn, cache_k, cache_v, cos_t, sa_t, sb_t, lam_p, g_sub, w_out,
                 lam_init, *, tkc):
    b, tq, d = x.shape
    past = cache_k.shape[1]
    cache_kt = jnp.transpose(cache_k, (0, 2, 3, 1))
    cache_vf = cache_v.reshape(b, past * H_B, HB_W)
    kern = functools.partial(_diffs_kernel, tq=tq, lam_init=lam_init)
    tspec = pl.BlockSpec((tq, LANES), lambda i, j: (0, 0))
    xspec = pl.BlockSpec((1, tq, d), lambda i, j: (i, 0, 0))
    kspec = pl.BlockSpec((1, 2 * H_B, DH_B, tkc), lambda i, j: (i, 0, 0, j))
    vspec = pl.BlockSpec((1, tkc * H_B, HB_W), lambda i, j: (i, j, 0))
    return pl.pallas_call(
        kern,
        grid=(b, past // tkc),
        in_specs=[
            xspec,
            pl.BlockSpec((1, N_MOD, d), lambda i, j: (mod_row0 + i, 0, 0)),
            pl.BlockSpec((6, d), lambda i, j: (0, 0)),
            _resident((d, d), lambda i, j: (0, 0)),
            xspec, xspec, kspec, vspec,
            tspec, tspec, tspec,
            pl.BlockSpec((4, DH_B), lambda i, j: (0, 0)),
            pl.BlockSpec((1, HB_W), lambda i, j: (0, 0)),
            _resident((d, d), lambda i, j: (0, 0)),
        ],
        out_specs=xspec,
        out_shape=jax.ShapeDtypeStruct(x.shape, F32),
        scratch_shapes=[
            pltpu.VMEM((2 * tq, d), BF),
            pltpu.VMEM((H_B, 2 * tq, LANES), F32),
            pltpu.VMEM((H_B, 2 * tq, 2 * HB_W), F32),
            pltpu.VMEM((tq, d), BF),
            pltpu.VMEM((H_B, 2 * tq, tkc), F32),
        ],
        compiler_params=_params(2),
        name="diff_sample",
    )(x, mod, g, w_q, kn, vn, cache_kt, cache_vf, cos_t, sa_t, sb_t, lam_p, g_sub.reshape(1, HB_W), w_out)


def _rope_tables_a(pos):
    inv = jnp.power(ROPE_THETA, -jnp.arange(0, DK_A, 2, dtype=jnp.float32) / DK_A)
    ang = pos[:, None] * inv[None, :]
    cos, sin = jnp.cos(ang), jnp.sin(ang)
    return jnp.concatenate([cos, cos], axis=-1), jnp.concatenate([-sin, sin], axis=-1)


def _rope_tables_b(pos):
    inv = jnp.power(ROPE_THETA, -jnp.arange(0, DH_B, 2, dtype=jnp.float32) / DH_B)
    ang = pos[:, None] * inv[None, :]
    cos, sin = jnp.cos(ang), jnp.sin(ang)
    zero = jnp.zeros_like(sin)
    cos_t = jnp.concatenate([cos, cos, cos, cos], axis=-1)
    sa = jnp.concatenate([zero, sin, zero, sin], axis=-1)
    sb = jnp.concatenate([-sin, zero, -sin, zero], axis=-1)
    return cos_t, sa, sb


def kernel(x_prompt, x_sample, state_ret, cache_k, cache_v, c_prompt, c_sample, w_ada, b_ada, g_norm, w_ffn_in,
           w_ffn_out, w_in_a, g_gn_a, w_out_a, w_ada_kv, b_ada_kv, g_kv, w_kv, w_q_b, lam_b, g_subln_b, w_out_b):
    d = D_MODEL
    bp, sp, _ = x_prompt.shape
    bs, ss, _ = x_sample.shape

    n_rows = -(-(bs + bp) // 16) * 16
    c_all = jnp.concatenate([c_sample, c_prompt, jnp.zeros((n_rows - bs - bp, d), F32)], axis=0)
    mod = _ada_proj(c_all, w_ada, b_ada).reshape(2, n_rows, N_MOD, d)
    modkv = _ada_proj(c_all, w_ada_kv[None], b_ada_kv[None]).reshape(n_rows, 2, d)
    row_s, row_p = 0, bs

    wfi0 = w_ffn_in[0, 0].astype(BF)[None]
    wfo0 = w_ffn_out[0, 0].astype(BF)[None]
    wqb = w_q_b.astype(BF)
    wob = w_out_b.astype(BF)
    n_ffn = w_ffn_in.shape[0] * w_ffn_in.shape[1]
    ffn_casts = (
        (w_ffn_in.reshape(n_ffn * d, 2 * D_FF), d, (n_ffn - 1) * d, 128),
        (w_ffn_out.reshape(n_ffn * D_FF, d), D_FF, (n_ffn - 1) * D_FF, 352),
        (w_in_a.reshape(d, 2 * QK_A + 2 * V_A), 0, d, 32),
        (w_out_a.reshape(V_A, d), 0, V_A, 64),
        (w_kv, 0, d, 32),
    )

    pos_p = jnp.arange(sp, dtype=jnp.float32)
    pos_s = PAST_LEN + jnp.arange(ss, dtype=jnp.float32)
    nb_s = 16
    nb_r = 4
    ts_p = 512
    t_ret = 256
    tq = 256

    cos_ap, sin_ap = _rope_tables_a(pos_p)
    cos_as, sin_as = [jnp.tile(a, (nb_r, 1)) for a in _rope_tables_a(pos_s)]
    tab_bp = _rope_tables_b(pos_p)
    tab_bs = _rope_tables_b(pos_s)
    tab_bs_tiled = [jnp.tile(a, (nb_s, 1)) for a in tab_bs]

    xp, wfi_rest, wfo_rest, wia, woa, wkv = _ffn(x_prompt, mod, row_p, g_norm, wfi0, wfo0, 0, layer=0, idx=0,
                                                 nb=1, ts=ts_p, casts=ffn_casts)
    wfi_rest = wfi_rest.reshape(n_ffn - 1, d, 2 * D_FF)
    wfo_rest = wfo_rest.reshape(n_ffn - 1, D_FF, d)

    def ffn_p(x, l, i, kv=None):
        return _ffn(x, mod, row_p, g_norm, wfi_rest, wfo_rest, 2 * l + i - 1, layer=l, idx=i, nb=1, ts=ts_p, kv=kv)

    def ffn_s(x, l, i, kv=None):
        if (l, i) == (0, 0):
            return _ffn(x, mod, row_s, g_norm, wfi0, wfo0, 0, layer=l, idx=i, nb=nb_s, ts=ss)
        return _ffn(x, mod, row_s, g_norm, wfi_rest, wfo_rest, 2 * l + i - 1, layer=l, idx=i, nb=nb_s, ts=ss,
                    kv=kv)

    xs = ffn_s(x_sample, 0, 0)
    xp, ret_p = _ret_mixer(xp, mod[0], row_p, g_norm[0], wia, g_gn_a[0], woa, cos_ap, sin_ap, None,
                           nb=1, t=t_ret)
    xs, ret_s = _ret_mixer(xs, mod[0], row_s, g_norm[0], wia, g_gn_a[0], woa, cos_as, sin_as, state_ret[0],
                           nb=nb_r, t=ss)
    xp, k_p, v_p, kb_p, vtb_p = ffn_p(xp, 0, 1, kv=(modkv, g_kv, wkv, tab_bp, tq))
    xs, k_s, v_s, kb_s, vb_s = ffn_s(xs, 0, 1, kv=(modkv, g_kv, wkv, tab_bs_tiled, None))

    lam_init = 0.8 - 0.6 * math.exp(-0.3 * 1)
    xp = ffn_p(xp, 1, 0)
    xs = ffn_s(xs, 1, 0)
    xp = _diff_prompt(xp, mod[1], row_p, g_norm[1], w_q_b[0].T.astype(BF), kb_p, vtb_p,
                      *[a.T for a in tab_bp], lam_b[0], g_subln_b[0], wob[0],
                      lam_init, tq=tq)
    xs = _diff_sample(xs, mod[1], row_s, g_norm[1], wqb[0], kb_s, vb_s,
                      cache_k, cache_v, *tab_bs, lam_b[0], g_subln_b[0], wob[0], lam_init, tkc=2048)
    xp = ffn_p(xp, 1, 1)
    xs = ffn_s(xs, 1, 1)

    return (xp, xs,
            ret_p[None], k_p.reshape(bp, sp, 2 * H_B, DH_B), v_p.reshape(bp, sp, H_B, 2 * DH_B),
            ret_s[None], k_s.reshape(bs, ss, 2 * H_B, DH_B), v_s.reshape(bs, ss, H_B, 2 * DH_B))
```

```python
import functools
import math

import jax
import jax.numpy as jnp
from jax import lax
from jax.experimental import pallas as pl
from jax.experimental.pallas import tpu as pltpu

D_MODEL = 1024
CHUNK = 64
CHUNK_SHIFT = 6
H_A = 8
DK_A = 128
DV_A = 256
QK_A = H_A * DK_A
V_A = H_A * DV_A
H_B = 8
DH_B = 64
HB_W = 2 * DH_B
D_FF = 2816
N_MOD = 9
ROPE_THETA = 10000.0
EPS = 1e-6
HALF = 0.5
PAST_LEN = 4096
NEG = -1e30
LOG2E = 1.4426950408889634
Q_SCALE = DH_B ** -0.5 * LOG2E
VT_ROWS = HB_W + 16

LANES = 128
VMEM_LIMIT = 56 * 1024 * 1024

BF = jnp.bfloat16
F32 = jnp.float32

LOG_GAMMA = [math.log1p(-(2.0 ** (-5 - h))) for h in range(H_A)]


def _dot(a, b):
    return jnp.dot(a, b, preferred_element_type=F32)


def _dot_nt(a, b):
    return lax.dot_general(a, b, (((1,), (1,)), ((), ())), preferred_element_type=F32)


def _dot_tn(a, b):
    return lax.dot_general(a, b, (((0,), (0,)), ((), ())), preferred_element_type=F32)


def _rmsn(x):
    return x * lax.rsqrt(jnp.mean(x * x, axis=-1, keepdims=True) + EPS)


def _silu(x):
    return x * jax.nn.sigmoid(x)


def _resident(shape, index_map):
    return pl.BlockSpec(shape, index_map, pipeline_mode=pl.Buffered(1))


def _params(n_axes):
    return pltpu.CompilerParams(dimension_semantics=("arbitrary",) * n_axes,
                                vmem_limit_bytes=VMEM_LIMIT)


def _ada_kernel(c_ref, w_ref, b_ref, o_ref):
    sc = _silu(c_ref[...]).astype(BF)
    o_ref[...] = _dot(sc, w_ref[...].astype(BF)) + b_ref[...]


def _ada_proj(c_all, w, b, tn=1024):
    nl, d, n = w.shape
    r = c_all.shape[0]
    return pl.pallas_call(
        _ada_kernel,
        grid=(nl, n // tn),
        in_specs=[
            pl.BlockSpec((r, d), lambda l, j: (0, 0)),
            pl.BlockSpec((None, d, tn), lambda l, j: (l, 0, j)),
            pl.BlockSpec((None, 1, tn), lambda l, j: (l, 0, j)),
        ],
        out_specs=pl.BlockSpec((None, r, tn), lambda l, j: (l, 0, j)),
        out_shape=jax.ShapeDtypeStruct((nl, r, n), F32),
        compiler_params=_params(2),
        name="ada_proj",
    )(c_all, w, b.reshape(nl, 1, n))


N_KV_IN, N_KV_OUT = 6, 4


def _ffn_kernel(x_ref, mod_ref, g_ref, win_ref, wout_ref, *rest, m0, g0, fc, n_cast, vt_block, with_kv):
    n_in = (N_KV_IN if with_kv else 0) + n_cast
    kv_in, cast_src = rest[:n_in - n_cast], rest[n_in - n_cast:n_in]
    o_ref = rest[n_in]
    kv_out = rest[n_in + 1:n_in + 1 + (N_KV_OUT if with_kv else 0)]
    cast_dst = rest[n_in + 1 + len(kv_out):]
    for src, dst in zip(cast_src, cast_dst):
        dst[...] = src[...].astype(BF)
    nb, ts, d = x_ref.shape
    x = x_ref[...]
    shift = mod_ref[:, m0:m0 + 1, :]
    gain_in = g_ref[g0:g0 + 1, :] * (1.0 + mod_ref[:, m0 + 1:m0 + 2, :])
    gain_out = g_ref[g0 + 1:g0 + 2, :] * (HALF * mod_ref[:, m0 + 2:m0 + 3, :])
    h = _rmsn(x) * gain_in + shift
    hb = h.reshape(nb * ts, d).astype(BF)
    acc = jnp.zeros((nb * ts, d), F32)
    for c in range(D_FF // fc):
        a = _dot(hb, win_ref[:, c * fc:(c + 1) * fc])
        b = _dot(hb, win_ref[:, D_FF + c * fc:D_FF + (c + 1) * fc])
        u = (_silu(a) * b).astype(BF)
        acc = acc + _dot(u, wout_ref[c * fc:(c + 1) * fc, :])
    out = x + _rmsn(acc).reshape(nb, ts, d) * gain_out
    o_ref[...] = out
    if with_kv:
        _kv_tile(out, *kv_in, *kv_out, vt_block=vt_block)


def _ffn(x, mod, mod_row0, g, w_in, w_out, wsel, *, layer, idx, nb, ts, casts=(), kv=None):
    b, s, d = x.shape
    nbt, nst = b // nb, s // ts
    mrow = mod_row0 // nb
    xspec = pl.BlockSpec((nb, ts, d), lambda i, j: (i, j, 0))
    kv_args, kv_specs, kv_out_specs, kv_shapes, vt_block = [], [], [], [], None
    if kv is not None:
        modkv, g_kv, w_kv, tables, vt_block = kv
        tspec = pl.BlockSpec((nb * ts, LANES), lambda i, j: (j, 0))
        kv_args = [modkv, g_kv.reshape(1, d), w_kv, *tables]
        kv_specs = [pl.BlockSpec((nb, 2, d), lambda i, j: (mrow + i, 0, 0)),
                    pl.BlockSpec((1, d), lambda i, j: (0, 0)),
                    _resident((d, 2 * d), lambda i, j: (0, 0)),
                    tspec, tspec, tspec]
        if vt_block is None:
            vb_spec, vb_shape = xspec, jax.ShapeDtypeStruct(x.shape, BF)
        else:
            assert nb == 1
            vb_spec = pl.BlockSpec((1, ts // vt_block, H_B, VT_ROWS, vt_block), lambda i, j: (i, j, 0, 0, 0))
            vb_shape = jax.ShapeDtypeStruct((b, s // vt_block, H_B, VT_ROWS, vt_block), BF)
        kv_out_specs = [xspec, xspec, xspec, vb_spec]
        kv_shapes = [jax.ShapeDtypeStruct(x.shape, F32), jax.ShapeDtypeStruct(x.shape, F32),
                     jax.ShapeDtypeStruct(x.shape, BF), vb_shape]
    cast_args, cast_specs, cast_out_specs, cast_shapes = [], [], [], []
    for w, row0, rows, step in casts:
        assert row0 % step == 0 and rows % step == 0 and rows // step <= nbt * nst
        first, last = row0 // step, rows // step - 1
        cast_args.append(w)
        cast_specs.append(pl.BlockSpec((step, w.shape[1]),
                                       lambda i, j, first=first, last=last: (first + jnp.minimum(i * nst + j, last), 0)))
        cast_out_specs.append(pl.BlockSpec((step, w.shape[1]),
                                           lambda i, j, last=last: (jnp.minimum(i * nst + j, last), 0)))
        cast_shapes.append(jax.ShapeDtypeStruct((rows, w.shape[1]), BF))
    kern = functools.partial(_ffn_kernel, m0=6 * idx, g0=4 * idx, fc=256, n_cast=len(casts),
                             vt_block=vt_block, with_kv=kv is not None)
    outs = pl.pallas_call(
        kern,
        grid=(nbt, nst),
        in_specs=[
            xspec,
            pl.BlockSpec((None, nb, N_MOD, d), lambda i, j: (layer, mrow + i, 0, 0)),
            pl.BlockSpec((None, 6, d), lambda i, j: (layer, 0, 0)),
            _resident((None, d, 2 * D_FF), lambda i, j: (wsel, 0, 0)),
            _resident((None, D_FF, d), lambda i, j: (wsel, 0, 0)),
        ] + kv_specs + cast_specs,
        out_specs=[xspec] + kv_out_specs + cast_out_specs,
        out_shape=[jax.ShapeDtypeStruct(x.shape, F32)] + kv_shapes + cast_shapes,
        compiler_params=_params(2),
        name="ffn_kv" if kv is not None else "ffn",
    )(x, mod, g, w_in, w_out, *kv_args, *cast_args)
    return outs if (casts or kv is not None) else outs[0]


def _decay_mask(lg, t):
    i = lax.broadcasted_iota(jnp.int32, (t, t), 0)
    j = lax.broadcasted_iota(jnp.int32, (t, t), 1)
    dist = jnp.abs(i - j).astype(F32)
    vis = (j >> CHUNK_SHIFT) <= (i >> CHUNK_SHIFT)
    return jnp.where(vis, jnp.exp(lg * dist), 0.0)


def _ret_kernel(*refs, nb, t, from_input_state):
    if from_input_state:
        (x_ref, mod_ref, g_ref, win_ref, ggn_ref, wout_ref, cos_ref, sin_ref, sin_st_ref,
         o_ref, so_ref, q_scr, k_scr, v_scr, og_scr, gat_scr, mask_scr, st_scr) = refs
    else:
        (x_ref, mod_ref, g_ref, win_ref, ggn_ref, wout_ref, cos_ref, sin_ref,
         o_ref, so_ref, q_scr, k_scr, v_scr, og_scr, gat_scr, mask_scr, st_scr) = refs
        sin_st_ref = None
    d = D_MODEL
    r = nb * t
    j = pl.program_id(1)

    @pl.when(j == 0)
    def _():
        for hd in range(H_A):
            mask_scr[hd] = _decay_mask(LOG_GAMMA[hd], t)
        if not from_input_state:
            st_scr[...] = jnp.zeros_like(st_scr)

    x = x_ref[...]
    shift = mod_ref[:, 3:4, :]
    scale = mod_ref[:, 4:5, :]
    gate = mod_ref[:, 5:6, :]
    h = _rmsn(x) * g_ref[2:3, :] * (1.0 + scale) + shift
    hb = h.reshape(r, d).astype(BF)

    cos = cos_ref[...]
    sin = sin_ref[...]
    qk = _dot(hb, win_ref[:, 0:2 * QK_A])
    for c in range(H_A):
        qc = qk[:, c * DK_A:(c + 1) * DK_A]
        q_scr[:, c * DK_A:(c + 1) * DK_A] = qc * cos + pltpu.roll(qc, DK_A // 2, 1) * sin
        kc = qk[:, QK_A + c * DK_A:QK_A + (c + 1) * DK_A]
        k_scr[:, c * DK_A:(c + 1) * DK_A] = (kc * cos + pltpu.roll(kc, DK_A // 2, 1) * sin) * (DK_A ** -0.5)
    v_scr[...] = _dot(hb, win_ref[:, 2 * QK_A:2 * QK_A + V_A]).astype(BF)
    gp = _dot(hb, win_ref[:, 2 * QK_A + V_A:2 * QK_A + 2 * V_A])

    row = lax.broadcasted_iota(jnp.int32, (t, DK_A), 0).astype(F32)
    for hd in range(H_A):
        lg = LOG_GAMMA[hd]
        qdec = jnp.exp(lg * row)
        kdec = jnp.exp(lg * (float(t) - row))
        cdec = math.exp(lg * t)
        mask = mask_scr[hd]

        def one_batch(n, carry, hd=hd, qdec=qdec, kdec=kdec, cdec=cdec, mask=mask):
            rows = pl.ds(n * t, t)
            q = q_scr[rows, hd * DK_A:(hd + 1) * DK_A]
            k = k_scr[rows, hd * DK_A:(hd + 1) * DK_A]
            v = v_scr[rows, hd * DV_A:(hd + 1) * DV_A]
            if from_input_state:
                st = sin_st_ref[n, hd]
            else:
                st = st_scr[hd]
            s = _dot_nt(q.astype(BF), k.astype(BF)) * mask
            o = _dot(s.astype(BF), v) + _dot((q * qdec).astype(BF), st.astype(BF))
            st_new = st * cdec + _dot_tn((k * kdec).astype(BF), v)
            og_scr[rows, hd * DV_A:(hd + 1) * DV_A] = o
            if from_input_state:
                so_ref[n, hd] = st_new
            else:
                st_scr[hd] = st_new
            return carry

        for n in range(nb):
            one_batch(n, 0)

    for hd in range(H_A):
        cols = slice(hd * DV_A, (hd + 1) * DV_A)
        on = _rmsn(og_scr[:, cols]) * ggn_ref[:, cols]
        gat_scr[:, cols] = (_silu(gp[:, cols]) * on).astype(BF)
    y = _dot(gat_scr[...], wout_ref[...])
    yn = _rmsn(y) * g_ref[3:4, :]
    o_ref[...] = x + gate * yn.reshape(nb, t, d)

    if not from_input_state:
        @pl.when(j == pl.num_programs(1) - 1)
        def _():
            so_ref[0] = st_scr[...]


def _ret_mixer(x, mod, mod_row0, g, w_in, g_gn, w_out, cos_t, sin_t, state_in, *, nb, t):
    b, s, d = x.shape
    nbt, nst = b // nb, s // t
    r = nb * t
    mrow = mod_row0 // nb
    from_input = state_in is not None
    kern = functools.partial(_ret_kernel, nb=nb, t=t, from_input_state=from_input)
    in_specs = [
        pl.BlockSpec((nb, t, d), lambda i, j: (i, j, 0)),
        pl.BlockSpec((nb, N_MOD, d), lambda i, j: (mrow + i, 0, 0)),
        pl.BlockSpec((6, d), lambda i, j: (0, 0)),
        _resident((d, 2 * QK_A + 2 * V_A), lambda i, j: (0, 0)),
        pl.BlockSpec((1, V_A), lambda i, j: (0, 0)),
        _resident((V_A, d), lambda i, j: (0, 0)),
        pl.BlockSpec((r, DK_A), lambda i, j: (j, 0)),
        pl.BlockSpec((r, DK_A), lambda i, j: (j, 0)),
    ]
    args = [x, mod, g, w_in, g_gn.reshape(1, V_A), w_out, cos_t, sin_t]
    if from_input:
        in_specs.append(pl.BlockSpec((nb, H_A, DK_A, DV_A), lambda i, j: (i, 0, 0, 0)))
        args.append(state_in)
    out, st = pl.pallas_call(
        kern,
        grid=(nbt, nst),
        in_specs=in_specs,
        out_specs=[
            pl.BlockSpec((nb, t, d), lambda i, j: (i, j, 0)),
            pl.BlockSpec((nb, H_A, DK_A, DV_A), lambda i, j: (i, 0, 0, 0)),
        ],
        out_shape=[
            jax.ShapeDtypeStruct(x.shape, F32),
            jax.ShapeDtypeStruct((b, H_A, DK_A, DV_A), F32),
        ],
        scratch_shapes=[
            pltpu.VMEM((r, QK_A), F32),
            pltpu.VMEM((r, QK_A), F32),
            pltpu.VMEM((r, V_A), BF),
            pltpu.VMEM((r, V_A), F32),
            pltpu.VMEM((r, V_A), BF),
            pltpu.VMEM((H_A, t, t), F32),
            pltpu.VMEM((H_A, DK_A, DV_A), F32),
        ],
        compiler_params=_params(2),
        name="ret_mixer",
    )(*args)
    return out, st


def _rope64(blk, cos, sa, sb):
    return blk * cos + pltpu.roll(blk, DH_B // 2, 1) * sa + pltpu.roll(blk, LANES - DH_B // 2, 1) * sb


def _kv_tile(x, mod_ref, g_ref, w_ref, cos_ref, sa_ref, sb_ref, k_ref, v_ref, kb_ref, vb_ref, *, vt_block):
    nb, ts, d = x.shape
    shift = mod_ref[:, 0:1, :]
    scale = mod_ref[:, 1:2, :]
    h = _rmsn(x) * g_ref[...] * (1.0 + scale) + shift
    hb = h.reshape(nb * ts, d).astype(BF)
    kv = _dot(hb, w_ref[...])
    cos = cos_ref[...]
    sa = sa_ref[...]
    sb = sb_ref[...]
    ks = [_rope64(kv[:, c * LANES:(c + 1) * LANES], cos, sa, sb) for c in range(d // LANES)]
    k = jnp.concatenate(ks, axis=1).reshape(nb, ts, d)
    v = kv[:, d:2 * d].reshape(nb, ts, d)
    k_ref[...] = k
    v_ref[...] = v
    kb_ref[...] = k.astype(BF)
    if vt_block is None:
        vb_ref[...] = v.astype(BF)
    else:
        vt = kv[:, d:2 * d].T
        ones = jnp.ones((VT_ROWS - HB_W, vt_block), BF)
        for c in range(ts // vt_block):
            for hd in range(H_B):
                vb_ref[0, c, hd, 0:HB_W, :] = vt[hd * HB_W:(hd + 1) * HB_W,
                                                 c * vt_block:(c + 1) * vt_block].astype(BF)
                vb_ref[0, c, hd, HB_W:VT_ROWS, :] = ones


def _lambda(lam_ref, lam_init):
    lp = lam_ref[...]
    l1 = jnp.sum(lp[0:1, :] * lp[1:2, :], axis=-1, keepdims=True)
    l2 = jnp.sum(lp[2:3, :] * lp[3:4, :], axis=-1, keepdims=True)
    return jnp.exp(l1) - jnp.exp(l2) + lam_init


def _project_q(hb, wq_ref, cos, sa, sb, q_scr, tq):
    q = _dot(hb, wq_ref[...])
    lane = lax.broadcasted_iota(jnp.int32, (tq, LANES), 1)
    first = lane < DH_B
    for c in range(H_B):
        rq = _rope64(q[:, c * LANES:(c + 1) * LANES], cos, sa, sb) * Q_SCALE
        q_scr[0:tq, c * LANES:(c + 1) * LANES] = jnp.where(first, rq, 0.0).astype(BF)
        q_scr[tq:2 * tq, c * LANES:(c + 1) * LANES] = jnp.where(first, 0.0, rq).astype(BF)


def _finish_head(acc, l, lam, gsub, lam_init, tq):
    o = acc[0:tq] / l[0:tq] - lam * (acc[tq:2 * tq] / l[tq:2 * tq])
    return _rmsn(o) * gsub * (1.0 - lam_init)


def _roll_rows(x, shift):
    n = x.shape[0]
    return jnp.concatenate([x[n - shift:], x[:n - shift]], axis=0)


def _diffp_kernel(x_ref, mod_ref, g_ref, wqt_ref, k_ref, vt_ref, cos_ref, sa_ref, sb_ref, lam_ref,
                  gsub_ref, wout_ref, o_ref, qt_scr, m_scr, acc_scr, oh_scr, s_scr, e_scr, b_scr, *, tq, lam_init):
    qi = pl.program_id(1)

    @pl.when(qi == 0)
    def _():
        kr = lax.broadcasted_iota(jnp.int32, (tq, HB_W), 0)
        kc = lax.broadcasted_iota(jnp.int32, (tq, HB_W), 1)
        e_scr[...] = jnp.where((kr >> CHUNK_SHIFT) == kc, 1.0, 0.0).astype(BF)
        ra = lax.broadcasted_iota(jnp.int32, (HB_W, 2 * tq), 0)
        ci = lax.broadcasted_iota(jnp.int32, (HB_W, 2 * tq), 1)
        qchunk = jnp.where(ci >= tq, ci - tq, ci) >> CHUNK_SHIFT
        hidden = jnp.where(ra < tq // CHUNK, jnp.where(ra > qchunk, NEG, 0.0), 0.0)
        b_scr[...] = hidden.astype(BF)

    x = x_ref[0]
    shift = mod_ref[0, 3:4, :]
    scale = mod_ref[0, 4:5, :]
    gate = mod_ref[0, 5:6, :]
    h = _rmsn(x) * g_ref[2:3, :] * (1.0 + scale) + shift
    ht = h.T.astype(BF)
    cos = cos_ref[...]
    sa = sa_ref[...]
    sb = sb_ref[...]
    first = lax.broadcasted_iota(jnp.int32, (HB_W, tq), 0) < DH_B
    half = H_B // 2
    for hd in range(H_B):
        if hd % half == 0:
            qt4 = _dot(wqt_ref[hd * HB_W:(hd + half) * HB_W, :], ht)
        blk = qt4[(hd % half) * HB_W:(hd % half + 1) * HB_W, :]
        rq = (blk * cos + _roll_rows(blk, DH_B // 2) * sa + _roll_rows(blk, HB_W - DH_B // 2) * sb) * Q_SCALE
        qt_scr[hd] = jnp.concatenate([jnp.where(first, rq, 0.0), jnp.where(first, 0.0, rq)], axis=1).astype(BF)

    def block(kb, diagonal):
        rows = pl.ds(pl.multiple_of(kb * tq, tq), tq)
        for hd in range(H_B):
            cols = slice(hd * HB_W, (hd + 1) * HB_W)
            if diagonal:
                k_aug = jnp.concatenate([k_ref[0, rows, cols], e_scr[...]], axis=1)
                q_aug = jnp.concatenate([qt_scr[hd], b_scr[...]], axis=0)
                s_scr[hd] = _dot(k_aug, q_aug)
            else:
                s_scr[hd] = _dot(k_ref[0, rows, cols], qt_scr[hd])
        for hd in range(H_B):
            s = s_scr[hd]
            if diagonal:
                m_new = jnp.max(s, axis=0, keepdims=True)
                acc_scr[hd] = _dot(vt_ref[0, kb, hd], jnp.exp2(s - m_new).astype(BF))
            else:
                m_prev = m_scr[hd]
                m_new = jnp.maximum(m_prev, jnp.max(s, axis=0, keepdims=True))
                p = jnp.exp2(s - m_new).astype(BF)
                acc_scr[hd] = jnp.exp2(m_prev - m_new) * acc_scr[hd] + _dot(vt_ref[0, kb, hd], p)
            m_scr[hd] = m_new

    block(qi, True)

    def body(kb, carry):
        block(kb, False)
        return carry

    lax.fori_loop(0, qi, body, 0)

    lam = _lambda(lam_ref, lam_init)
    for hd in range(H_B):
        acc = acc_scr[hd, 0:HB_W, :]
        inv_l = 1.0 / acc_scr[hd, HB_W:HB_W + 1, :]
        o = acc[:, 0:tq] * inv_l[:, 0:tq] - lam * (acc[:, tq:2 * tq] * inv_l[:, tq:2 * tq])
        on = o * lax.rsqrt(jnp.mean(o * o, axis=0, keepdims=True) + EPS)
        oh_scr[hd * HB_W:(hd + 1) * HB_W, :] = (on * gsub_ref[...] * (1.0 - lam_init)).astype(BF)

    y = _dot_tn(oh_scr[...], wout_ref[...])
    o_ref[0] = x + gate * (_rmsn(y) * g_ref[3:4, :])


def _diff_prompt(x, mod, mod_row0, g, w_qt, kb, vtb, cos_t, sa_t, sb_t, lam_p, g_sub, w_out, lam_init, *, tq):
    b, s, d = x.shape
    kern = functools.partial(_diffp_kernel, tq=tq, lam_init=lam_init)
    tspec = pl.BlockSpec((HB_W, tq), lambda i, j: (0, j))
    return pl.pallas_call(
        kern,
        grid=(b, s // tq),
        in_specs=[
            pl.BlockSpec((1, tq, d), lambda i, j: (i, j, 0)),
            pl.BlockSpec((1, N_MOD, d), lambda i, j: (mod_row0 + i, 0, 0)),
            pl.BlockSpec((6, d), lambda i, j: (0, 0)),
            _resident((d, d), lambda i, j: (0, 0)),
            pl.BlockSpec((1, s, d), lambda i, j: (i, 0, 0)),
            pl.BlockSpec((1, s // tq, H_B, VT_ROWS, tq), lambda i, j: (i, 0, 0, 0, 0)),
            tspec, tspec, tspec,
            pl.BlockSpec((4, DH_B), lambda i, j: (0, 0)),
            pl.BlockSpec((HB_W, 1), lambda i, j: (0, 0)),
            _resident((d, d), lambda i, j: (0, 0)),
        ],
        out_specs=pl.BlockSpec((1, tq, d), lambda i, j: (i, j, 0)),
        out_shape=jax.ShapeDtypeStruct(x.shape, F32),
        scratch_shapes=[
            pltpu.VMEM((H_B, HB_W, 2 * tq), BF),
            pltpu.VMEM((H_B, 1, 2 * tq), F32),
            pltpu.VMEM((H_B, VT_ROWS, 2 * tq), F32),
            pltpu.VMEM((d, tq), BF),
            pltpu.VMEM((H_B, tq, 2 * tq), F32),
            pltpu.VMEM((tq, HB_W), BF),
            pltpu.VMEM((HB_W, 2 * tq), BF),
        ],
        compiler_params=_params(2),
        name="diff_prompt",
    )(x, mod, g, w_qt, kb, vtb, cos_t, sa_t, sb_t, lam_p, g_sub.reshape(HB_W, 1), w_out)


def _diffs_kernel(x_ref, mod_ref, g_ref, wq_ref, kn_ref, vn_ref, kc_ref, vc_ref, cos_ref, sa_ref, sb_ref,
                  lam_ref, gsub_ref, wout_ref, o_ref, q_scr, m_scr, acc_scr, oh_scr, s_scr, *, tq, lam_init):
    kc = pl.program_id(1)
    rows2 = 2 * tq

    def with_ones(v):
        return jnp.concatenate([v, jnp.ones(v.shape, BF)], axis=1)

    @pl.when(kc == 0)
    def _():
        x = x_ref[0]
        h = _rmsn(x) * g_ref[2:3, :] * (1.0 + mod_ref[0, 4:5, :]) + mod_ref[0, 3:4, :]
        _project_q(h.astype(BF), wq_ref, cos_ref[...], sa_ref[...], sb_ref[...], q_scr, tq)
        for hd in range(H_B):
            cols = slice(hd * HB_W, (hd + 1) * HB_W)
            s = _dot_nt(q_scr[:, cols], kn_ref[0, :, cols])
            m0 = jnp.max(s, axis=-1, keepdims=True)
            m_scr[hd] = jnp.broadcast_to(m0, (rows2, LANES))
            acc_scr[hd] = _dot(jnp.exp2(s - m0).astype(BF), with_ones(vn_ref[0, :, cols]))

    tkc = kc_ref.shape[3]
    for hd in range(H_B):
        s_scr[hd] = jnp.concatenate(
            [_dot(q_scr[t * tq:(t + 1) * tq, (2 * hd + t) * DH_B:(2 * hd + t + 1) * DH_B],
                  kc_ref[0, 2 * hd + t].astype(BF)) for t in range(2)], axis=0)
    for hd in range(H_B):
        s = s_scr[hd]
        v_h = with_ones(vc_ref[0, pl.ds(hd, tkc, stride=H_B), :].astype(BF))
        m_prev = m_scr[hd][:, 0:1]
        m_new = jnp.maximum(m_prev, jnp.max(s, axis=-1, keepdims=True))
        acc_scr[hd] = jnp.exp2(m_prev - m_new) * acc_scr[hd] + _dot(jnp.exp2(s - m_new).astype(BF), v_h)
        m_scr[hd] = jnp.broadcast_to(m_new, (rows2, LANES))

    @pl.when(kc == pl.num_programs(1) - 1)
    def _():
        lam = _lambda(lam_ref, lam_init)
        for hd in range(H_B):
            cols = slice(hd * HB_W, (hd + 1) * HB_W)
            oh_scr[:, cols] = _finish_head(acc_scr[hd, :, 0:HB_W], acc_scr[hd, :, HB_W:HB_W + 1], lam,
                                           gsub_ref[...], lam_init, tq).astype(BF)
        y = _dot(oh_scr[...], wout_ref[...])
        o_ref[0] = x_ref[0] + mod_ref[0, 5:6, :] * (_rmsn(y) * g_ref[3:4, :])


def _diff_sample(x, mod, mod_row0, g, w_q, kn, vn, cache_k, cache_v, cos_t, sa_t, sb_t, lam_p, g_sub, w_out,
                 lam_init, *, tkc):
    b, tq, d = x.shape
    past = cache_k.shape[1]
    cache_kt = jnp.transpose(cache_k, (0, 2, 3, 1))
    cache_vf = cache_v.reshape(b, past * H_B, HB_W)
    kern = functools.partial(_diffs_kernel, tq=tq, lam_init=lam_init)
    tspec = pl.BlockSpec((tq, LANES), lambda i, j: (0, 0))
    xspec = pl.BlockSpec((1, tq, d), lambda i, j: (i, 0, 0))
    kspec = pl.BlockSpec((1, 2 * H_B, DH_B, tkc), lambda i, j: (i, 0, 0, j))
    vspec = pl.BlockSpec((1, tkc * H_B, HB_W), lambda i, j: (i, j, 0))
    return pl.pallas_call(
        kern,
        grid=(b, past // tkc),
        in_specs=[
            xspec,
            pl.BlockSpec((1, N_MOD, d), lambda i, j: (mod_row0 + i, 0, 0)),
            pl.BlockSpec((6, d), lambda i, j: (0, 0)),
            _resident((d, d), lambda i, j: (0, 0)),
            xspec, xspec, kspec, vspec,
            tspec, tspec, tspec,
            pl.BlockSpec((4, DH_B), lambda i, j: (0, 0)),
            pl.BlockSpec((1, HB_W), lambda i, j: (0, 0)),
            _resident((d, d), lambda i, j: (0, 0)),
        ],
        out_specs=xspec,
        out_shape=jax.ShapeDtypeStruct(x.shape, F32),
        scratch_shapes=[
            pltpu.VMEM((2 * tq, d), BF),
            pltpu.VMEM((H_B, 2 * tq, LANES), F32),
            pltpu.VMEM((H_B, 2 * tq, 2 * HB_W), F32),
            pltpu.VMEM((tq, d), BF),
            pltpu.VMEM((H_B, 2 * tq, tkc), F32),
        ],
        compiler_params=_params(2),
        name="diff_sample",
    )(x, mod, g, w_q, kn, vn, cache_kt, cache_vf, cos_t, sa_t, sb_t, lam_p, g_sub.reshape(1, HB_W), w_out)


def _rope_tables_a(pos):
    inv = jnp.power(ROPE_THETA, -jnp.arange(0, DK_A, 2, dtype=jnp.float32) / DK_A)
    ang = pos[:, None] * inv[None, :]
    cos, sin = jnp.cos(ang), jnp.sin(ang)
    return jnp.concatenate([cos, cos], axis=-1), jnp.concatenate([-sin, sin], axis=-1)


def _rope_tables_b(pos):
    inv = jnp.power(ROPE_THETA, -jnp.arange(0, DH_B, 2, dtype=jnp.float32) / DH_B)
    ang = pos[:, None] * inv[None, :]
    cos, sin = jnp.cos(ang), jnp.sin(ang)
    zero = jnp.zeros_like(sin)
    cos_t = jnp.concatenate([cos, cos, cos, cos], axis=-1)
    sa = jnp.concatenate([zero, sin, zero, sin], axis=-1)
    sb = jnp.concatenate([-sin, zero, -sin, zero], axis=-1)
    return cos_t, sa, sb


def kernel(x_prompt, x_sample, state_ret, cache_k, cache_v, c_prompt, c_sample, w_ada, b_ada, g_norm, w_ffn_in,
           w_ffn_out, w_in_a, g_gn_a, w_out_a, w_ada_kv, b_ada_kv, g_kv, w_kv, w_q_b, lam_b, g_subln_b, w_out_b):
    d = D_MODEL
    bp, sp, _ = x_prompt.shape
    bs, ss, _ = x_sample.shape

    n_rows = -(-(bs + bp) // 16) * 16
    c_all = jnp.concatenate([c_sample, c_prompt, jnp.zeros((n_rows - bs - bp, d), F32)], axis=0)
    mod = _ada_proj(c_all, w_ada, b_ada, tn=N_MOD * d // 4).reshape(2, n_rows, N_MOD, d)
    modkv = _ada_proj(c_all, w_ada_kv[None], b_ada_kv[None]).reshape(n_rows, 2, d)
    row_s, row_p = 0, bs

    wfi0 = w_ffn_in[0, 0].astype(BF)[None]
    wfo0 = w_ffn_out[0, 0].astype(BF)[None]
    wqb = w_q_b.astype(BF)
    wob = w_out_b.astype(BF)
    n_ffn = w_ffn_in.shape[0] * w_ffn_in.shape[1]
    ffn_casts = (
        (w_ffn_in.reshape(n_ffn * d, 2 * D_FF), d, (n_ffn - 1) * d, 128),
        (w_ffn_out.reshape(n_ffn * D_FF, d), D_FF, (n_ffn - 1) * D_FF, 352),
        (w_in_a.reshape(d, 2 * QK_A + 2 * V_A), 0, d, 32),
        (w_out_a.reshape(V_A, d), 0, V_A, 64),
        (w_kv, 0, d, 32),
    )

    pos_p = jnp.arange(sp, dtype=jnp.float32)
    pos_s = PAST_LEN + jnp.arange(ss, dtype=jnp.float32)
    nb_s = 16
    nb_r = 4
    ts_p = 512
    t_ret = 256
    tq = 256

    cos_ap, sin_ap = _rope_tables_a(pos_p)
    cos_as, sin_as = [jnp.tile(a, (nb_r, 1)) for a in _rope_tables_a(pos_s)]
    tab_bp = _rope_tables_b(pos_p)
    tab_bs = _rope_tables_b(pos_s)
    tab_bs_tiled = [jnp.tile(a, (nb_s, 1)) for a in tab_bs]

    xp, wfi_rest, wfo_rest, wia, woa, wkv = _ffn(x_prompt, mod, row_p, g_norm, wfi0, wfo0, 0, layer=0, idx=0,
                                                 nb=1, ts=ts_p, casts=ffn_casts)
    wfi_rest = wfi_rest.reshape(n_ffn - 1, d, 2 * D_FF)
    wfo_rest = wfo_rest.reshape(n_ffn - 1, D_FF, d)

    def ffn_p(x, l, i, kv=None):
        ts = ts_p if kv is not None else 2 * ts_p
        return _ffn(x, mod, row_p, g_norm, wfi_rest, wfo_rest, 2 * l + i - 1, layer=l, idx=i, nb=1, ts=ts, kv=kv)

    def ffn_s(x, l, i, kv=None):
        if (l, i) == (0, 0):
            return _ffn(x, mod, row_s, g_norm, wfi0, wfo0, 0, layer=l, idx=i, nb=nb_s, ts=ss)
        return _ffn(x, mod, row_s, g_norm, wfi_rest, wfo_rest, 2 * l + i - 1, layer=l, idx=i, nb=nb_s, ts=ss,
                    kv=kv)

    xs = ffn_s(x_sample, 0, 0)
    xp, ret_p = _ret_mixer(xp, mod[0], row_p, g_norm[0], wia, g_gn_a[0], woa, cos_ap, sin_ap, None,
                           nb=1, t=t_ret)
    xs, ret_s = _ret_mixer(xs, mod[0], row_s, g_norm[0], wia, g_gn_a[0], woa, cos_as, sin_as, state_ret[0],
                           nb=nb_r, t=ss)
    xp, k_p, v_p, kb_p, vtb_p = ffn_p(xp, 0, 1, kv=(modkv, g_kv, wkv, tab_bp, tq))
    xs, k_s, v_s, kb_s, vb_s = ffn_s(xs, 0, 1, kv=(modkv, g_kv, wkv, tab_bs_tiled, None))

    lam_init = 0.8 - 0.6 * math.exp(-0.3 * 1)
    xp = ffn_p(xp, 1, 0)
    xs = ffn_s(xs, 1, 0)
    xp = _diff_prompt(xp, mod[1], row_p, g_norm[1], w_q_b[0].T.astype(BF), kb_p, vtb_p,
                      *[a.T for a in tab_bp], lam_b[0], g_subln_b[0], wob[0],
                      lam_init, tq=tq)
    xs = _diff_sample(xs, mod[1], row_s, g_norm[1], wqb[0], kb_s, vb_s,
                      cache_k, cache_v, *tab_bs, lam_b[0], g_subln_b[0], wob[0], lam_init, tkc=2048)
    xp = ffn_p(xp, 1, 1)
    xs = ffn_s(xs, 1, 1)

    return (xp, xs,
            ret_p[None], k_p.reshape(bp, sp, 2 * H_B, DH_B), v_p.reshape(bp, sp, H_B, 2 * DH_B),
            ret_s[None], k_s.reshape(bs, ss, 2 * H_B, DH_B), v_s.reshape(bs, ss, H_B, 2 * DH_B))
```

```python
import functools
import math

import jax
import jax.numpy as jnp
from jax import lax
from jax.experimental import pallas as pl
from jax.experimental.pallas import tpu as pltpu

D_MODEL = 1024
CHUNK = 64
CHUNK_SHIFT = 6
H_A = 8
DK_A = 128
DV_A = 256
QK_A = H_A * DK_A
V_A = H_A * DV_A
H_B = 8
DH_B = 64
HB_W = 2 * DH_B
D_FF = 2816
N_MOD = 9
ROPE_THETA = 10000.0
EPS = 1e-6
HALF = 0.5
PAST_LEN = 4096
NEG = -1e30
LOG2E = 1.4426950408889634
Q_SCALE = DH_B ** -0.5 * LOG2E
VT_ROWS = HB_W + 16

LANES = 128
VMEM_LIMIT = 56 * 1024 * 1024

BF = jnp.bfloat16
F32 = jnp.float32

LOG_GAMMA = [math.log1p(-(2.0 ** (-5 - h))) for h in range(H_A)]


def _dot(a, b):
    return jnp.dot(a, b, preferred_element_type=F32)


def _dot_nt(a, b):
    return lax.dot_general(a, b, (((1,), (1,)), ((), ())), preferred_element_type=F32)


def _dot_tn(a, b):
    return lax.dot_general(a, b, (((0,), (0,)), ((), ())), preferred_element_type=F32)


def _rmsn(x):
    return x * lax.rsqrt(jnp.mean(x * x, axis=-1, keepdims=True) + EPS)


def _silu(x):
    return x * jax.nn.sigmoid(x)


def _resident(shape, index_map):
    return pl.BlockSpec(shape, index_map, pipeline_mode=pl.Buffered(1))


def _params(n_axes):
    return pltpu.CompilerParams(dimension_semantics=("arbitrary",) * n_axes,
                                vmem_limit_bytes=VMEM_LIMIT)


def _ada_kernel(c_ref, w_ref, b_ref, o_ref):
    sc = _silu(c_ref[...]).astype(BF)
    o_ref[...] = _dot(sc, w_ref[...].astype(BF)) + b_ref[...]


def _ada_proj(c_all, w, b, tn=1024):
    nl, d, n = w.shape
    r = c_all.shape[0]
    return pl.pallas_call(
        _ada_kernel,
        grid=(nl, n // tn),
        in_specs=[
            pl.BlockSpec((r, d), lambda l, j: (0, 0)),
            pl.BlockSpec((None, d, tn), lambda l, j: (l, 0, j)),
            pl.BlockSpec((None, 1, tn), lambda l, j: (l, 0, j)),
        ],
        out_specs=pl.BlockSpec((None, r, tn), lambda l, j: (l, 0, j)),
        out_shape=jax.ShapeDtypeStruct((nl, r, n), F32),
        compiler_params=_params(2),
        name="ada_proj",
    )(c_all, w, b.reshape(nl, 1, n))


N_KV_IN, N_KV_OUT = 6, 4


def _ffn_kernel(x_ref, mod_ref, g_ref, win_ref, wout_ref, *rest, m0, g0, fc, n_cast, vt_block, with_kv):
    n_in = (N_KV_IN if with_kv else 0) + n_cast
    kv_in, cast_src = rest[:n_in - n_cast], rest[n_in - n_cast:n_in]
    o_ref = rest[n_in]
    kv_out = rest[n_in + 1:n_in + 1 + (N_KV_OUT if with_kv else 0)]
    cast_dst = rest[n_in + 1 + len(kv_out):]
    for src, dst in zip(cast_src, cast_dst):
        dst[...] = src[...].astype(BF)
    nb, ts, d = x_ref.shape
    x = x_ref[...]
    shift = mod_ref[:, m0:m0 + 1, :]
    gain_in = g_ref[g0:g0 + 1, :] * (1.0 + mod_ref[:, m0 + 1:m0 + 2, :])
    gain_out = g_ref[g0 + 1:g0 + 2, :] * (HALF * mod_ref[:, m0 + 2:m0 + 3, :])
    h = _rmsn(x) * gain_in + shift
    hb = h.reshape(nb * ts, d).astype(BF)
    acc = jnp.zeros((nb * ts, d), F32)
    for c in range(D_FF // fc):
        a = _dot(hb, win_ref[:, c * fc:(c + 1) * fc])
        b = _dot(hb, win_ref[:, D_FF + c * fc:D_FF + (c + 1) * fc])
        u = (_silu(a) * b).astype(BF)
        acc = acc + _dot(u, wout_ref[c * fc:(c + 1) * fc, :])
    out = x + _rmsn(acc).reshape(nb, ts, d) * gain_out
    o_ref[...] = out
    if with_kv:
        _kv_tile(out, *kv_in, *kv_out, vt_block=vt_block)


def _ffn(x, mod, mod_row0, g, w_in, w_out, wsel, *, layer, idx, nb, ts, casts=(), kv=None):
    b, s, d = x.shape
    nbt, nst = b // nb, s // ts
    mrow = mod_row0 // nb
    xspec = pl.BlockSpec((nb, ts, d), lambda i, j: (i, j, 0))
    kv_args, kv_specs, kv_out_specs, kv_shapes, vt_block = [], [], [], [], None
    if kv is not None:
        modkv, g_kv, w_kv, tables, vt_block = kv
        tspec = pl.BlockSpec((nb * ts, LANES), lambda i, j: (j, 0))
        kv_args = [modkv, g_kv.reshape(1, d), w_kv, *tables]
        kv_specs = [pl.BlockSpec((nb, 2, d), lambda i, j: (mrow + i, 0, 0)),
                    pl.BlockSpec((1, d), lambda i, j: (0, 0)),
                    _resident((d, 2 * d), lambda i, j: (0, 0)),
                    tspec, tspec, tspec]
        if vt_block is None:
            vb_spec, vb_shape = xspec, jax.ShapeDtypeStruct(x.shape, BF)
        else:
            assert nb == 1
            vb_spec = pl.BlockSpec((1, ts // vt_block, H_B, VT_ROWS, vt_block), lambda i, j: (i, j, 0, 0, 0))
            vb_shape = jax.ShapeDtypeStruct((b, s // vt_block, H_B, VT_ROWS, vt_block), BF)
        kv_out_specs = [xspec, xspec, xspec, vb_spec]
        kv_shapes = [jax.ShapeDtypeStruct(x.shape, F32), jax.ShapeDtypeStruct(x.shape, F32),
                     jax.ShapeDtypeStruct(x.shape, BF), vb_shape]
    cast_args, cast_specs, cast_out_specs, cast_shapes = [], [], [], []
    for w, row0, rows, step in casts:
        assert row0 % step == 0 and rows % step == 0 and rows // step <= nbt * nst
        first, last = row0 // step, rows // step - 1
        cast_args.append(w)
        cast_specs.append(pl.BlockSpec((step, w.shape[1]),
                                       lambda i, j, first=first, last=last: (first + jnp.minimum(i * nst + j, last), 0)))
        cast_out_specs.append(pl.BlockSpec((step, w.shape[1]),
                                           lambda i, j, last=last: (jnp.minimum(i * nst + j, last), 0)))
        cast_shapes.append(jax.ShapeDtypeStruct((rows, w.shape[1]), BF))
    kern = functools.partial(_ffn_kernel, m0=6 * idx, g0=4 * idx, fc=256, n_cast=len(casts),
                             vt_block=vt_block, with_kv=kv is not None)
    outs = pl.pallas_call(
        kern,
        grid=(nbt, nst),
        in_specs=[
            xspec,
            pl.BlockSpec((None, nb, N_MOD, d), lambda i, j: (layer, mrow + i, 0, 0)),
            pl.BlockSpec((None, 6, d), lambda i, j: (layer, 0, 0)),
            _resident((None, d, 2 * D_FF), lambda i, j: (wsel, 0, 0)),
            _resident((None, D_FF, d), lambda i, j: (wsel, 0, 0)),
        ] + kv_specs + cast_specs,
        out_specs=[xspec] + kv_out_specs + cast_out_specs,
        out_shape=[jax.ShapeDtypeStruct(x.shape, F32)] + kv_shapes + cast_shapes,
        compiler_params=_params(2),
        name="ffn_kv" if kv is not None else "ffn",
    )(x, mod, g, w_in, w_out, *kv_args, *cast_args)
    return outs if (casts or kv is not None) else outs[0]


def _decay_mask(lg, t):
    i = lax.broadcasted_iota(jnp.int32, (t, t), 0)
    j = lax.broadcasted_iota(jnp.int32, (t, t), 1)
    dist = jnp.abs(i - j).astype(F32)
    vis = (j >> CHUNK_SHIFT) <= (i >> CHUNK_SHIFT)
    return jnp.where(vis, jnp.exp(lg * dist), 0.0)


def _ret_kernel(*refs, nb, t, from_input_state):
    if from_input_state:
        (x_ref, mod_ref, g_ref, win_ref, ggn_ref, wout_ref, cos_ref, sin_ref, sin_st_ref,
         o_ref, so_ref, q_scr, k_scr, v_scr, og_scr, gat_scr, mask_scr, st_scr) = refs
    else:
        (x_ref, mod_ref, g_ref, win_ref, ggn_ref, wout_ref, cos_ref, sin_ref,
         o_ref, so_ref, q_scr, k_scr, v_scr, og_scr, gat_scr, mask_scr, st_scr) = refs
        sin_st_ref = None
    d = D_MODEL
    r = nb * t
    j = pl.program_id(1)

    @pl.when(j == 0)
    def _():
        for hd in range(H_A):
            mask_scr[hd] = _decay_mask(LOG_GAMMA[hd], t)
        if not from_input_state:
            st_scr[...] = jnp.zeros_like(st_scr)

    x = x_ref[...]
    shift = mod_ref[:, 3:4, :]
    scale = mod_ref[:, 4:5, :]
    gate = mod_ref[:, 5:6, :]
    h = _rmsn(x) * g_ref[2:3, :] * (1.0 + scale) + shift
    hb = h.reshape(r, d).astype(BF)

    cos = cos_ref[...]
    sin = sin_ref[...]
    qk = _dot(hb, win_ref[:, 0:2 * QK_A])
    for c in range(H_A):
        qc = qk[:, c * DK_A:(c + 1) * DK_A]
        q_scr[:, c * DK_A:(c + 1) * DK_A] = qc * cos + pltpu.roll(qc, DK_A // 2, 1) * sin
        kc = qk[:, QK_A + c * DK_A:QK_A + (c + 1) * DK_A]
        k_scr[:, c * DK_A:(c + 1) * DK_A] = (kc * cos + pltpu.roll(kc, DK_A // 2, 1) * sin) * (DK_A ** -0.5)
    v_scr[...] = _dot(hb, win_ref[:, 2 * QK_A:2 * QK_A + V_A]).astype(BF)
    gp = _dot(hb, win_ref[:, 2 * QK_A + V_A:2 * QK_A + 2 * V_A])

    row = lax.broadcasted_iota(jnp.int32, (t, DK_A), 0).astype(F32)
    for hd in range(H_A):
        lg = LOG_GAMMA[hd]
        qdec = jnp.exp(lg * row)
        kdec = jnp.exp(lg * (float(t) - row))
        cdec = math.exp(lg * t)
        mask = mask_scr[hd]

        def one_batch(n, carry, hd=hd, qdec=qdec, kdec=kdec, cdec=cdec, mask=mask):
            rows = pl.ds(n * t, t)
            q = q_scr[rows, hd * DK_A:(hd + 1) * DK_A]
            k = k_scr[rows, hd * DK_A:(hd + 1) * DK_A]
            v = v_scr[rows, hd * DV_A:(hd + 1) * DV_A]
            if from_input_state:
                st = sin_st_ref[n, hd]
            else:
                st = st_scr[hd]
            s = _dot_nt(q.astype(BF), k.astype(BF)) * mask
            o = _dot(s.astype(BF), v) + _dot((q * qdec).astype(BF), st.astype(BF))
            st_new = st * cdec + _dot_tn((k * kdec).astype(BF), v)
            og_scr[rows, hd * DV_A:(hd + 1) * DV_A] = o
            if from_input_state:
                so_ref[n, hd] = st_new
            else:
                st_scr[hd] = st_new
            return carry

        for n in range(nb):
            one_batch(n, 0)

    for hd in range(H_A):
        cols = slice(hd * DV_A, (hd + 1) * DV_A)
        on = _rmsn(og_scr[:, cols]) * ggn_ref[:, cols]
        gat_scr[:, cols] = (_silu(gp[:, cols]) * on).astype(BF)
    y = _dot(gat_scr[...], wout_ref[...])
    yn = _rmsn(y) * g_ref[3:4, :]
    o_ref[...] = x + gate * yn.reshape(nb, t, d)

    if not from_input_state:
        @pl.when(j == pl.num_programs(1) - 1)
        def _():
            so_ref[0] = st_scr[...]


def _ret_mixer(x, mod, mod_row0, g, w_in, g_gn, w_out, cos_t, sin_t, state_in, *, nb, t):
    b, s, d = x.shape
    nbt, nst = b // nb, s // t
    r = nb * t
    mrow = mod_row0 // nb
    from_input = state_in is not None
    kern = functools.partial(_ret_kernel, nb=nb, t=t, from_input_state=from_input)
    in_specs = [
        pl.BlockSpec((nb, t, d), lambda i, j: (i, j, 0)),
        pl.BlockSpec((nb, N_MOD, d), lambda i, j: (mrow + i, 0, 0)),
        pl.BlockSpec((6, d), lambda i, j: (0, 0)),
        _resident((d, 2 * QK_A + 2 * V_A), lambda i, j: (0, 0)),
        pl.BlockSpec((1, V_A), lambda i, j: (0, 0)),
        _resident((V_A, d), lambda i, j: (0, 0)),
        pl.BlockSpec((r, DK_A), lambda i, j: (j, 0)),
        pl.BlockSpec((r, DK_A), lambda i, j: (j, 0)),
    ]
    args = [x, mod, g, w_in, g_gn.reshape(1, V_A), w_out, cos_t, sin_t]
    if from_input:
        in_specs.append(pl.BlockSpec((nb, H_A, DK_A, DV_A), lambda i, j: (i, 0, 0, 0)))
        args.append(state_in)
    out, st = pl.pallas_call(
        kern,
        grid=(nbt, nst),
        in_specs=in_specs,
        out_specs=[
            pl.BlockSpec((nb, t, d), lambda i, j: (i, j, 0)),
            pl.BlockSpec((nb, H_A, DK_A, DV_A), lambda i, j: (i, 0, 0, 0)),
        ],
        out_shape=[
            jax.ShapeDtypeStruct(x.shape, F32),
            jax.ShapeDtypeStruct((b, H_A, DK_A, DV_A), F32),
        ],
        scratch_shapes=[
            pltpu.VMEM((r, QK_A), F32),
            pltpu.VMEM((r, QK_A), F32),
            pltpu.VMEM((r, V_A), BF),
            pltpu.VMEM((r, V_A), F32),
            pltpu.VMEM((r, V_A), BF),
            pltpu.VMEM((H_A, t, t), F32),
            pltpu.VMEM((H_A, DK_A, DV_A), F32),
        ],
        compiler_params=_params(2),
        name="ret_mixer",
    )(*args)
    return out, st


def _rope64(blk, cos, sa, sb):
    return blk * cos + pltpu.roll(blk, DH_B // 2, 1) * sa + pltpu.roll(blk, LANES - DH_B // 2, 1) * sb


def _kv_tile(x, mod_ref, g_ref, w_ref, cos_ref, sa_ref, sb_ref, k_ref, v_ref, kb_ref, vb_ref, *, vt_block):
    nb, ts, d = x.shape
    shift = mod_ref[:, 0:1, :]
    scale = mod_ref[:, 1:2, :]
    h = _rmsn(x) * g_ref[...] * (1.0 + scale) + shift
    hb = h.reshape(nb * ts, d).astype(BF)
    kv = _dot(hb, w_ref[...])
    cos = cos_ref[...]
    sa = sa_ref[...]
    sb = sb_ref[...]
    ks = [_rope64(kv[:, c * LANES:(c + 1) * LANES], cos, sa, sb) for c in range(d // LANES)]
    k = jnp.concatenate(ks, axis=1).reshape(nb, ts, d)
    v = kv[:, d:2 * d].reshape(nb, ts, d)
    k_ref[...] = k
    v_ref[...] = v
    kb_ref[...] = k.astype(BF)
    if vt_block is None:
        vb_ref[...] = v.astype(BF)
    else:
        vt = kv[:, d:2 * d].T
        ones = jnp.ones((VT_ROWS - HB_W, vt_block), BF)
        for c in range(ts // vt_block):
            for hd in range(H_B):
                vb_ref[0, c, hd, 0:HB_W, :] = vt[hd * HB_W:(hd + 1) * HB_W,
                                                 c * vt_block:(c + 1) * vt_block].astype(BF)
                vb_ref[0, c, hd, HB_W:VT_ROWS, :] = ones


def _lambda(lam_ref, lam_init):
    lp = lam_ref[...]
    l1 = jnp.sum(lp[0:1, :] * lp[1:2, :], axis=-1, keepdims=True)
    l2 = jnp.sum(lp[2:3, :] * lp[3:4, :], axis=-1, keepdims=True)
    return jnp.exp(l1) - jnp.exp(l2) + lam_init


def _project_q(hb, wq_ref, cos, sa, sb, q_scr, tq):
    q = _dot(hb, wq_ref[...])
    lane = lax.broadcasted_iota(jnp.int32, (tq, LANES), 1)
    first = lane < DH_B
    for c in range(H_B):
        rq = _rope64(q[:, c * LANES:(c + 1) * LANES], cos, sa, sb) * Q_SCALE
        q_scr[0:tq, c * LANES:(c + 1) * LANES] = jnp.where(first, rq, 0.0).astype(BF)
        q_scr[tq:2 * tq, c * LANES:(c + 1) * LANES] = jnp.where(first, 0.0, rq).astype(BF)


def _finish_head(acc, l, lam, gsub, lam_init, tq):
    o = acc[0:tq] / l[0:tq] - lam * (acc[tq:2 * tq] / l[tq:2 * tq])
    return _rmsn(o) * gsub * (1.0 - lam_init)


def _roll_rows(x, shift):
    n = x.shape[0]
    return jnp.concatenate([x[n - shift:], x[:n - shift]], axis=0)


def _diffp_kernel(x_ref, mod_ref, g_ref, wqt_ref, k_ref, vt_ref, cos_ref, sa_ref, sb_ref, lam_ref,
                  gsub_ref, wout_ref, o_ref, qt_scr, m_scr, acc_scr, oh_scr, s_scr, e_scr, b_scr, *, tq, lam_init):
    qi = pl.program_id(1)

    @pl.when(qi == 0)
    def _():
        kr = lax.broadcasted_iota(jnp.int32, (tq, HB_W), 0)
        kc = lax.broadcasted_iota(jnp.int32, (tq, HB_W), 1)
        e_scr[...] = jnp.where((kr >> CHUNK_SHIFT) == kc, 1.0, 0.0).astype(BF)
        ra = lax.broadcasted_iota(jnp.int32, (HB_W, 2 * tq), 0)
        ci = lax.broadcasted_iota(jnp.int32, (HB_W, 2 * tq), 1)
        qchunk = jnp.where(ci >= tq, ci - tq, ci) >> CHUNK_SHIFT
        hidden = jnp.where(ra < tq // CHUNK, jnp.where(ra > qchunk, NEG, 0.0), 0.0)
        b_scr[...] = hidden.astype(BF)

    x = x_ref[0]
    shift = mod_ref[0, 3:4, :]
    scale = mod_ref[0, 4:5, :]
    gate = mod_ref[0, 5:6, :]
    h = _rmsn(x) * g_ref[2:3, :] * (1.0 + scale) + shift
    ht = h.T.astype(BF)
    cos = cos_ref[...]
    sa = sa_ref[...]
    sb = sb_ref[...]
    first = lax.broadcasted_iota(jnp.int32, (HB_W, tq), 0) < DH_B
    half = H_B // 2
    for hd in range(H_B):
        if hd % half == 0:
            qt4 = _dot(wqt_ref[hd * HB_W:(hd + half) * HB_W, :], ht)
        blk = qt4[(hd % half) * HB_W:(hd % half + 1) * HB_W, :]
        rq = (blk * cos + _roll_rows(blk, DH_B // 2) * sa + _roll_rows(blk, HB_W - DH_B // 2) * sb) * Q_SCALE
        qt_scr[hd] = jnp.concatenate([jnp.where(first, rq, 0.0), jnp.where(first, 0.0, rq)], axis=1).astype(BF)

    def block(kb, diagonal):
        rows = pl.ds(pl.multiple_of(kb * tq, tq), tq)
        for hd in range(H_B):
            cols = slice(hd * HB_W, (hd + 1) * HB_W)
            if diagonal:
                k_aug = jnp.concatenate([k_ref[0, rows, cols], e_scr[...]], axis=1)
                q_aug = jnp.concatenate([qt_scr[hd], b_scr[...]], axis=0)
                s_scr[hd] = _dot(k_aug, q_aug)
            else:
                s_scr[hd] = _dot(k_ref[0, rows, cols], qt_scr[hd])
        for hd in range(H_B):
            s = s_scr[hd]
            if diagonal:
                m_new = jnp.max(s, axis=0, keepdims=True)
                acc_scr[hd] = _dot(vt_ref[0, kb, hd], jnp.exp2(s - m_new).astype(BF))
            else:
                m_prev = m_scr[hd]
                m_new = jnp.maximum(m_prev, jnp.max(s, axis=0, keepdims=True))
                p = jnp.exp2(s - m_new).astype(BF)
                acc_scr[hd] = jnp.exp2(m_prev - m_new) * acc_scr[hd] + _dot(vt_ref[0, kb, hd], p)
            m_scr[hd] = m_new

    block(qi, True)

    def body(kb, carry):
        block(kb, False)
        return carry

    lax.fori_loop(0, qi, body, 0)

    lam = _lambda(lam_ref, lam_init)
    for hd in range(H_B):
        acc = acc_scr[hd, 0:HB_W, :]
        inv_l = 1.0 / acc_scr[hd, HB_W:HB_W + 1, :]
        o = acc[:, 0:tq] * inv_l[:, 0:tq] - lam * (acc[:, tq:2 * tq] * inv_l[:, tq:2 * tq])
        on = o * lax.rsqrt(jnp.mean(o * o, axis=0, keepdims=True) + EPS)
        oh_scr[hd * HB_W:(hd + 1) * HB_W, :] = (on * gsub_ref[...] * (1.0 - lam_init)).astype(BF)

    y = _dot_tn(oh_scr[...], wout_ref[...])
    o_ref[0] = x + gate * (_rmsn(y) * g_ref[3:4, :])


def _diff_prompt(x, mod, mod_row0, g, w_qt, kb, vtb, cos_t, sa_t, sb_t, lam_p, g_sub, w_out, lam_init, *, tq):
    b, s, d = x.shape
    kern = functools.partial(_diffp_kernel, tq=tq, lam_init=lam_init)
    tspec = pl.BlockSpec((HB_W, tq), lambda i, j: (0, j))
    return pl.pallas_call(
        kern,
        grid=(b, s // tq),
        in_specs=[
            pl.BlockSpec((1, tq, d), lambda i, j: (i, j, 0)),
            pl.BlockSpec((1, N_MOD, d), lambda i, j: (mod_row0 + i, 0, 0)),
            pl.BlockSpec((6, d), lambda i, j: (0, 0)),
            _resident((d, d), lambda i, j: (0, 0)),
            pl.BlockSpec((1, s, d), lambda i, j: (i, 0, 0)),
            pl.BlockSpec((1, s // tq, H_B, VT_ROWS, tq), lambda i, j: (i, 0, 0, 0, 0)),
            tspec, tspec, tspec,
            pl.BlockSpec((4, DH_B), lambda i, j: (0, 0)),
            pl.BlockSpec((HB_W, 1), lambda i, j: (0, 0)),
            _resident((d, d), lambda i, j: (0, 0)),
        ],
        out_specs=pl.BlockSpec((1, tq, d), lambda i, j: (i, j, 0)),
        out_shape=jax.ShapeDtypeStruct(x.shape, F32),
        scratch_shapes=[
            pltpu.VMEM((H_B, HB_W, 2 * tq), BF),
            pltpu.VMEM((H_B, 1, 2 * tq), F32),
            pltpu.VMEM((H_B, VT_ROWS, 2 * tq), F32),
            pltpu.VMEM((d, tq), BF),
            pltpu.VMEM((H_B, tq, 2 * tq), F32),
            pltpu.VMEM((tq, HB_W), BF),
            pltpu.VMEM((HB_W, 2 * tq), BF),
        ],
        compiler_params=_params(2),
        name="diff_prompt",
    )(x, mod, g, w_qt, kb, vtb, cos_t, sa_t, sb_t, lam_p, g_sub.reshape(HB_W, 1), w_out)


def _diffs_step(kc, n_kc, x_ref, mod_ref, g_ref, wq_ref, kn_ref, vn_ref, kc_ref, vc_ref, cos_ref, sa_ref, sb_ref,
                lam_ref, gsub_ref, wout_ref, o_ref, q_scr, m_scr, acc_scr, oh_scr, s_scr, *, tq, lam_init):
    rows2 = 2 * tq

    def with_ones(v):
        return jnp.concatenate([v, jnp.ones(v.shape, BF)], axis=1)

    @pl.when(kc == 0)
    def _():
        x = x_ref[0]
        h = _rmsn(x) * g_ref[2:3, :] * (1.0 + mod_ref[0, 4:5, :]) + mod_ref[0, 3:4, :]
        _project_q(h.astype(BF), wq_ref, cos_ref[...], sa_ref[...], sb_ref[...], q_scr, tq)
        for hd in range(H_B):
            cols = slice(hd * HB_W, (hd + 1) * HB_W)
            s = _dot_nt(q_scr[:, cols], kn_ref[0, :, cols])
            m0 = jnp.max(s, axis=-1, keepdims=True)
            m_scr[hd] = jnp.broadcast_to(m0, (rows2, LANES))
            acc_scr[hd] = _dot(jnp.exp2(s - m0).astype(BF), with_ones(vn_ref[0, :, cols]))

    tkc = kc_ref.shape[3]
    for hd in range(H_B):
        s_scr[hd] = jnp.concatenate(
            [_dot(q_scr[t * tq:(t + 1) * tq, (2 * hd + t) * DH_B:(2 * hd + t + 1) * DH_B],
                  kc_ref[0, 2 * hd + t].astype(BF)) for t in range(2)], axis=0)
    for hd in range(H_B):
        s = s_scr[hd]
        v_h = with_ones(vc_ref[0, pl.ds(hd, tkc, stride=H_B), :].astype(BF))
        m_prev = m_scr[hd][:, 0:1]
        m_new = jnp.maximum(m_prev, jnp.max(s, axis=-1, keepdims=True))
        acc_scr[hd] = jnp.exp2(m_prev - m_new) * acc_scr[hd] + _dot(jnp.exp2(s - m_new).astype(BF), v_h)
        m_scr[hd] = jnp.broadcast_to(m_new, (rows2, LANES))

    @pl.when(kc == n_kc - 1)
    def _():
        lam = _lambda(lam_ref, lam_init)
        for hd in range(H_B):
            cols = slice(hd * HB_W, (hd + 1) * HB_W)
            oh_scr[:, cols] = _finish_head(acc_scr[hd, :, 0:HB_W], acc_scr[hd, :, HB_W:HB_W + 1], lam,
                                           gsub_ref[...], lam_init, tq).astype(BF)
        y = _dot(oh_scr[...], wout_ref[...])
        o_ref[0] = x_ref[0] + mod_ref[0, 5:6, :] * (_rmsn(y) * g_ref[3:4, :])


N_DIFFS_IN = 14
FFN_PHASES = 4


def _ffn_diffs_kernel(xp_ref, modp_ref, gp_ref, win_ref, wout_ref, *rest, m0, g0, fc, tq, lam_init):
    d_in = rest[:N_DIFFS_IN]
    op_ref, os_ref = rest[N_DIFFS_IN:N_DIFFS_IN + 2]
    h_scr, accf_scr = rest[N_DIFFS_IN + 2:N_DIFFS_IN + 4]
    d_scr = rest[N_DIFFS_IN + 4:]
    d = xp_ref.shape[-1]
    phase = lax.rem(pl.program_id(0), FFN_PHASES)
    n_chunks = D_FF // fc
    per = -(-n_chunks // FFN_PHASES)

    for p in range(FFN_PHASES):
        @pl.when(phase == p)
        def _(p=p):
            if p == 0:
                gain_in = gp_ref[g0:g0 + 1, :] * (1.0 + modp_ref[:, m0 + 1:m0 + 2, :])
                h = _rmsn(xp_ref[...]) * gain_in + modp_ref[:, m0:m0 + 1, :]
                h_scr[...] = h.reshape(-1, d).astype(BF)
            hb = h_scr[...]
            acc = None
            for c in range(p * per, min((p + 1) * per, n_chunks)):
                a = _dot(hb, win_ref[:, c * fc:(c + 1) * fc])
                b = _dot(hb, win_ref[:, D_FF + c * fc:D_FF + (c + 1) * fc])
                part = _dot((_silu(a) * b).astype(BF), wout_ref[c * fc:(c + 1) * fc, :])
                acc = part if acc is None else acc + part
            if p > 0:
                acc = accf_scr[...] + acc
            if p < FFN_PHASES - 1:
                accf_scr[...] = acc
            else:
                x = xp_ref[...]
                gain_out = gp_ref[g0 + 1:g0 + 2, :] * (HALF * modp_ref[:, m0 + 2:m0 + 3, :])
                op_ref[...] = x + _rmsn(acc).reshape(x.shape) * gain_out

    _diffs_step(phase, FFN_PHASES, *d_in, os_ref, *d_scr, tq=tq, lam_init=lam_init)


def _ffn_diff_sample(xp, mod, row_p, g, w_in, w_out, wsel, x, mod1, row_s, g1, w_q, kn, vn, cache_k, cache_v,
                     cos_t, sa_t, sb_t, lam_p, g_sub, w_out_b, lam_init, *, layer, idx, ts):
    bp, sp, d = xp.shape
    b, tq, _ = x.shape
    past = cache_k.shape[1]
    tkc = past // FFN_PHASES
    nst = sp // ts
    assert bp * nst == b
    cache_kt = jnp.transpose(cache_k, (0, 2, 3, 1))
    cache_vf = cache_v.reshape(b, past * H_B, HB_W)

    def tile(t):
        return t // FFN_PHASES

    xp_spec = pl.BlockSpec((1, ts, d), lambda t: (tile(t) // nst, tile(t) % nst, 0))
    tspec = pl.BlockSpec((tq, LANES), lambda t: (0, 0))
    xspec = pl.BlockSpec((1, tq, d), lambda t: (tile(t), 0, 0))
    kspec = pl.BlockSpec((1, 2 * H_B, DH_B, tkc), lambda t: (tile(t), 0, 0, t % FFN_PHASES))
    vspec = pl.BlockSpec((1, tkc * H_B, HB_W), lambda t: (tile(t), t % FFN_PHASES, 0))
    kern = functools.partial(_ffn_diffs_kernel, m0=6 * idx, g0=4 * idx, fc=256, tq=tq, lam_init=lam_init)
    return pl.pallas_call(
        kern,
        grid=(b * FFN_PHASES,),
        in_specs=[
            xp_spec,
            pl.BlockSpec((None, 1, N_MOD, d), lambda t: (layer, row_p + tile(t) // nst, 0, 0)),
            pl.BlockSpec((None, 6, d), lambda t: (layer, 0, 0)),
            _resident((None, d, 2 * D_FF), lambda t: (wsel, 0, 0)),
            _resident((None, D_FF, d), lambda t: (wsel, 0, 0)),
            xspec,
            pl.BlockSpec((1, N_MOD, d), lambda t: (row_s + tile(t), 0, 0)),
            pl.BlockSpec((6, d), lambda t: (0, 0)),
            _resident((d, d), lambda t: (0, 0)),
            xspec, xspec, kspec, vspec,
            tspec, tspec, tspec,
            pl.BlockSpec((4, DH_B), lambda t: (0, 0)),
            pl.BlockSpec((1, HB_W), lambda t: (0, 0)),
            _resident((d, d), lambda t: (0, 0)),
        ],
        out_specs=[xp_spec, xspec],
        out_shape=[jax.ShapeDtypeStruct(xp.shape, F32), jax.ShapeDtypeStruct(x.shape, F32)],
        scratch_shapes=[
            pltpu.VMEM((ts, d), BF),
            pltpu.VMEM((ts, d), F32),
            pltpu.VMEM((2 * tq, d), BF),
            pltpu.VMEM((H_B, 2 * tq, LANES), F32),
            pltpu.VMEM((H_B, 2 * tq, 2 * HB_W), F32),
            pltpu.VMEM((tq, d), BF),
            pltpu.VMEM((H_B, 2 * tq, tkc), F32),
        ],
        compiler_params=_params(1),
        name="ffn_diff_sample",
    )(xp, mod, g, w_in, w_out, x, mod1, g1, w_q, kn, vn, cache_kt, cache_vf, cos_t, sa_t, sb_t, lam_p,
      g_sub.reshape(1, HB_W), w_out_b)


def _rope_tables_a(pos):
    inv = jnp.power(ROPE_THETA, -jnp.arange(0, DK_A, 2, dtype=jnp.float32) / DK_A)
    ang = pos[:, None] * inv[None, :]
    cos, sin = jnp.cos(ang), jnp.sin(ang)
    return jnp.concatenate([cos, cos], axis=-1), jnp.concatenate([-sin, sin], axis=-1)


def _rope_tables_b(pos):
    inv = jnp.power(ROPE_THETA, -jnp.arange(0, DH_B, 2, dtype=jnp.float32) / DH_B)
    ang = pos[:, None] * inv[None, :]
    cos, sin = jnp.cos(ang), jnp.sin(ang)
    zero = jnp.zeros_like(sin)
    cos_t = jnp.concatenate([cos, cos, cos, cos], axis=-1)
    sa = jnp.concatenate([zero, sin, zero, sin], axis=-1)
    sb = jnp.concatenate([-sin, zero, -sin, zero], axis=-1)
    return cos_t, sa, sb


def kernel(x_prompt, x_sample, state_ret, cache_k, cache_v, c_prompt, c_sample, w_ada, b_ada, g_norm, w_ffn_in,
           w_ffn_out, w_in_a, g_gn_a, w_out_a, w_ada_kv, b_ada_kv, g_kv, w_kv, w_q_b, lam_b, g_subln_b, w_out_b):
    d = D_MODEL
    bp, sp, _ = x_prompt.shape
    bs, ss, _ = x_sample.shape

    n_rows = -(-(bs + bp) // 16) * 16
    c_all = jnp.concatenate([c_sample, c_prompt, jnp.zeros((n_rows - bs - bp, d), F32)], axis=0)
    mod = _ada_proj(c_all, w_ada, b_ada).reshape(2, n_rows, N_MOD, d)
    modkv = _ada_proj(c_all, w_ada_kv[None], b_ada_kv[None]).reshape(n_rows, 2, d)
    row_s, row_p = 0, bs

    wfi0 = w_ffn_in[0, 0].astype(BF)[None]
    wfo0 = w_ffn_out[0, 0].astype(BF)[None]
    wqb = w_q_b.astype(BF)
    wob = w_out_b.astype(BF)
    n_ffn = w_ffn_in.shape[0] * w_ffn_in.shape[1]
    ffn_casts = (
        (w_ffn_in.reshape(n_ffn * d, 2 * D_FF), d, (n_ffn - 1) * d, 128),
        (w_ffn_out.reshape(n_ffn * D_FF, d), D_FF, (n_ffn - 1) * D_FF, 352),
        (w_in_a.reshape(d, 2 * QK_A + 2 * V_A), 0, d, 32),
        (w_out_a.reshape(V_A, d), 0, V_A, 64),
        (w_kv, 0, d, 32),
    )

    pos_p = jnp.arange(sp, dtype=jnp.float32)
    pos_s = PAST_LEN + jnp.arange(ss, dtype=jnp.float32)
    nb_s = 16
    nb_r = 4
    ts_p = 512
    t_ret = 256
    tq = 256

    cos_ap, sin_ap = _rope_tables_a(pos_p)
    cos_as, sin_as = [jnp.tile(a, (nb_r, 1)) for a in _rope_tables_a(pos_s)]
    tab_bp = _rope_tables_b(pos_p)
    tab_bs = _rope_tables_b(pos_s)
    tab_bs_tiled = [jnp.tile(a, (nb_s, 1)) for a in tab_bs]

    xp, wfi_rest, wfo_rest, wia, woa, wkv = _ffn(x_prompt, mod, row_p, g_norm, wfi0, wfo0, 0, layer=0, idx=0,
                                                 nb=1, ts=ts_p, casts=ffn_casts)
    wfi_rest = wfi_rest.reshape(n_ffn - 1, d, 2 * D_FF)
    wfo_rest = wfo_rest.reshape(n_ffn - 1, D_FF, d)

    def ffn_p(x, l, i, kv=None):
        return _ffn(x, mod, row_p, g_norm, wfi_rest, wfo_rest, 2 * l + i - 1, layer=l, idx=i, nb=1, ts=ts_p, kv=kv)

    def ffn_s(x, l, i, kv=None):
        if (l, i) == (0, 0):
            return _ffn(x, mod, row_s, g_norm, wfi0, wfo0, 0, layer=l, idx=i, nb=nb_s, ts=ss)
        return _ffn(x, mod, row_s, g_norm, wfi_rest, wfo_rest, 2 * l + i - 1, layer=l, idx=i, nb=nb_s, ts=ss,
                    kv=kv)

    xs = ffn_s(x_sample, 0, 0)
    xp, ret_p = _ret_mixer(xp, mod[0], row_p, g_norm[0], wia, g_gn_a[0], woa, cos_ap, sin_ap, None,
                           nb=1, t=t_ret)
    xs, ret_s = _ret_mixer(xs, mod[0], row_s, g_norm[0], wia, g_gn_a[0], woa, cos_as, sin_as, state_ret[0],
                           nb=nb_r, t=ss)
    xp, k_p, v_p, kb_p, vtb_p = ffn_p(xp, 0, 1, kv=(modkv, g_kv, wkv, tab_bp, tq))
    xs, k_s, v_s, kb_s, vb_s = ffn_s(xs, 0, 1, kv=(modkv, g_kv, wkv, tab_bs_tiled, None))

    lam_init = 0.8 - 0.6 * math.exp(-0.3 * 1)
    xs = ffn_s(xs, 1, 0)
    xp, xs = _ffn_diff_sample(xp, mod, row_p, g_norm, wfi_rest, wfo_rest, 1,
                              xs, mod[1], row_s, g_norm[1], wqb[0], kb_s, vb_s, cache_k, cache_v,
                              *tab_bs, lam_b[0], g_subln_b[0], wob[0], lam_init, layer=1, idx=0, ts=ts_p)
    xp = _diff_prompt(xp, mod[1], row_p, g_norm[1], w_q_b[0].T.astype(BF), kb_p, vtb_p,
                      *[a.T for a in tab_bp], lam_b[0], g_subln_b[0], wob[0],
                      lam_init, tq=tq)
    xp = ffn_p(xp, 1, 1)
    xs = ffn_s(xs, 1, 1)

    return (xp, xs,
            ret_p[None], k_p.reshape(bp, sp, 2 * H_B, DH_B), v_p.reshape(bp, sp, H_B, 2 * DH_B),
            ret_s[None], k_s.reshape(bs, ss, 2 * H_B, DH_B), v_s.reshape(bs, ss, H_B, 2 * DH_B))
```

```python
import functools
import math

import jax
import jax.numpy as jnp
from jax import lax
from jax.experimental import pallas as pl
from jax.experimental.pallas import tpu as pltpu

D_MODEL = 1024
CHUNK = 64
CHUNK_SHIFT = 6
H_A = 8
DK_A = 128
DV_A = 256
QK_A = H_A * DK_A
V_A = H_A * DV_A
H_B = 8
DH_B = 64
HB_W = 2 * DH_B
D_FF = 2816
N_MOD = 9
ROPE_THETA = 10000.0
EPS = 1e-6
HALF = 0.5
PAST_LEN = 4096
NEG = -1e30
LOG2E = 1.4426950408889634
Q_SCALE = DH_B ** -0.5 * LOG2E
VT_ROWS = HB_W + 16

LANES = 128
VMEM_LIMIT = 56 * 1024 * 1024

BF = jnp.bfloat16
F32 = jnp.float32

LOG_GAMMA = [math.log1p(-(2.0 ** (-5 - h))) for h in range(H_A)]


def _dot(a, b):
    return jnp.dot(a, b, preferred_element_type=F32)


def _dot_nt(a, b):
    return lax.dot_general(a, b, (((1,), (1,)), ((), ())), preferred_element_type=F32)


def _dot_tn(a, b):
    return lax.dot_general(a, b, (((0,), (0,)), ((), ())), preferred_element_type=F32)


def _rmsn(x):
    return x * lax.rsqrt(jnp.mean(x * x, axis=-1, keepdims=True) + EPS)


def _silu(x):
    return x * jax.nn.sigmoid(x)


def _resident(shape, index_map):
    return pl.BlockSpec(shape, index_map, pipeline_mode=pl.Buffered(1))


def _params(n_axes):
    return pltpu.CompilerParams(dimension_semantics=("arbitrary",) * n_axes,
                                vmem_limit_bytes=VMEM_LIMIT)


def _ada_kernel(c_ref, w_ref, b_ref, o_ref):
    sc = _silu(c_ref[...]).astype(BF)
    o_ref[...] = _dot(sc, w_ref[...].astype(BF)) + b_ref[...]


def _ada_proj(c_all, w, b, tn=1024):
    nl, d, n = w.shape
    r = c_all.shape[0]
    return pl.pallas_call(
        _ada_kernel,
        grid=(nl, n // tn),
        in_specs=[
            pl.BlockSpec((r, d), lambda l, j: (0, 0)),
            pl.BlockSpec((None, d, tn), lambda l, j: (l, 0, j)),
            pl.BlockSpec((None, 1, tn), lambda l, j: (l, 0, j)),
        ],
        out_specs=pl.BlockSpec((None, r, tn), lambda l, j: (l, 0, j)),
        out_shape=jax.ShapeDtypeStruct((nl, r, n), F32),
        compiler_params=_params(2),
        name="ada_proj",
    )(c_all, w, b.reshape(nl, 1, n))


N_KV_IN, N_KV_OUT = 6, 4


def _ffn_kernel(x_ref, mod_ref, g_ref, win_ref, wout_ref, *rest, m0, g0, fc, n_cast, vt_block, with_kv):
    n_in = (N_KV_IN if with_kv else 0) + n_cast
    kv_in, cast_src = rest[:n_in - n_cast], rest[n_in - n_cast:n_in]
    o_ref = rest[n_in]
    kv_out = rest[n_in + 1:n_in + 1 + (N_KV_OUT if with_kv else 0)]
    cast_dst = rest[n_in + 1 + len(kv_out):]
    for src, dst in zip(cast_src, cast_dst):
        dst[...] = src[...].astype(BF)
    nb, ts, d = x_ref.shape
    x = x_ref[...]
    shift = mod_ref[:, m0:m0 + 1, :]
    gain_in = g_ref[g0:g0 + 1, :] * (1.0 + mod_ref[:, m0 + 1:m0 + 2, :])
    gain_out = g_ref[g0 + 1:g0 + 2, :] * (HALF * mod_ref[:, m0 + 2:m0 + 3, :])
    h = _rmsn(x) * gain_in + shift
    hb = h.reshape(nb * ts, d).astype(BF)
    acc = jnp.zeros((nb * ts, d), F32)
    for c in range(D_FF // fc):
        a = _dot(hb, win_ref[:, c * fc:(c + 1) * fc])
        b = _dot(hb, win_ref[:, D_FF + c * fc:D_FF + (c + 1) * fc])
        u = (_silu(a) * b).astype(BF)
        acc = acc + _dot(u, wout_ref[c * fc:(c + 1) * fc, :])
    out = x + _rmsn(acc).reshape(nb, ts, d) * gain_out
    o_ref[...] = out
    if with_kv:
        _kv_tile(out, *kv_in, *kv_out, vt_block=vt_block)


def _ffn(x, mod, mod_row0, g, w_in, w_out, wsel, *, layer, idx, nb, ts, casts=(), kv=None):
    b, s, d = x.shape
    nbt, nst = b // nb, s // ts
    mrow = mod_row0 // nb
    xspec = pl.BlockSpec((nb, ts, d), lambda i, j: (i, j, 0))
    kv_args, kv_specs, kv_out_specs, kv_shapes, vt_block = [], [], [], [], None
    if kv is not None:
        modkv, g_kv, w_kv, tables, vt_block = kv
        tspec = pl.BlockSpec((nb * ts, LANES), lambda i, j: (j, 0))
        kv_args = [modkv, g_kv.reshape(1, d), w_kv, *tables]
        kv_specs = [pl.BlockSpec((nb, 2, d), lambda i, j: (mrow + i, 0, 0)),
                    pl.BlockSpec((1, d), lambda i, j: (0, 0)),
                    _resident((d, 2 * d), lambda i, j: (0, 0)),
                    tspec, tspec, tspec]
        if vt_block is None:
            vb_spec, vb_shape = xspec, jax.ShapeDtypeStruct(x.shape, BF)
        else:
            assert nb == 1
            vb_spec = pl.BlockSpec((1, ts // vt_block, H_B, VT_ROWS, vt_block), lambda i, j: (i, j, 0, 0, 0))
            vb_shape = jax.ShapeDtypeStruct((b, s // vt_block, H_B, VT_ROWS, vt_block), BF)
        kv_out_specs = [xspec, xspec, xspec, vb_spec]
        kv_shapes = [jax.ShapeDtypeStruct(x.shape, F32), jax.ShapeDtypeStruct(x.shape, F32),
                     jax.ShapeDtypeStruct(x.shape, BF), vb_shape]
    cast_args, cast_specs, cast_out_specs, cast_shapes = [], [], [], []
    for w, row0, rows, step in casts:
        assert row0 % step == 0 and rows % step == 0 and rows // step <= nbt * nst
        first, last = row0 // step, rows // step - 1
        cast_args.append(w)
        cast_specs.append(pl.BlockSpec((step, w.shape[1]),
                                       lambda i, j, first=first, last=last: (first + jnp.minimum(i * nst + j, last), 0)))
        cast_out_specs.append(pl.BlockSpec((step, w.shape[1]),
                                           lambda i, j, last=last: (jnp.minimum(i * nst + j, last), 0)))
        cast_shapes.append(jax.ShapeDtypeStruct((rows, w.shape[1]), BF))
    kern = functools.partial(_ffn_kernel, m0=6 * idx, g0=4 * idx, fc=256, n_cast=len(casts),
                             vt_block=vt_block, with_kv=kv is not None)
    outs = pl.pallas_call(
        kern,
        grid=(nbt, nst),
        in_specs=[
            xspec,
            pl.BlockSpec((None, nb, N_MOD, d), lambda i, j: (layer, mrow + i, 0, 0)),
            pl.BlockSpec((None, 6, d), lambda i, j: (layer, 0, 0)),
            _resident((None, d, 2 * D_FF), lambda i, j: (wsel, 0, 0)),
            _resident((None, D_FF, d), lambda i, j: (wsel, 0, 0)),
        ] + kv_specs + cast_specs,
        out_specs=[xspec] + kv_out_specs + cast_out_specs,
        out_shape=[jax.ShapeDtypeStruct(x.shape, F32)] + kv_shapes + cast_shapes,
        compiler_params=_params(2),
        name="ffn_kv" if kv is not None else "ffn",
    )(x, mod, g, w_in, w_out, *kv_args, *cast_args)
    return outs if (casts or kv is not None) else outs[0]


def _decay_mask(lg, t):
    i = lax.broadcasted_iota(jnp.int32, (t, t), 0)
    j = lax.broadcasted_iota(jnp.int32, (t, t), 1)
    dist = jnp.abs(i - j).astype(F32)
    vis = (j >> CHUNK_SHIFT) <= (i >> CHUNK_SHIFT)
    return jnp.where(vis, jnp.exp(lg * dist), 0.0)


def _ret_kernel(*refs, nb, t, from_input_state):
    if from_input_state:
        (x_ref, mod_ref, g_ref, win_ref, ggn_ref, wout_ref, cos_ref, sin_ref, sin_st_ref,
         o_ref, so_ref, q_scr, k_scr, v_scr, og_scr, gat_scr, mask_scr, st_scr) = refs
    else:
        (x_ref, mod_ref, g_ref, win_ref, ggn_ref, wout_ref, cos_ref, sin_ref,
         o_ref, so_ref, q_scr, k_scr, v_scr, og_scr, gat_scr, mask_scr, st_scr) = refs
        sin_st_ref = None
    d = D_MODEL
    r = nb * t
    j = pl.program_id(1)

    @pl.when(j == 0)
    def _():
        for hd in range(H_A):
            mask_scr[hd] = _decay_mask(LOG_GAMMA[hd], t)
        if not from_input_state:
            st_scr[...] = jnp.zeros_like(st_scr)

    x = x_ref[...]
    shift = mod_ref[:, 3:4, :]
    scale = mod_ref[:, 4:5, :]
    gate = mod_ref[:, 5:6, :]
    h = _rmsn(x) * g_ref[2:3, :] * (1.0 + scale) + shift
    hb = h.reshape(r, d).astype(BF)

    cos = cos_ref[...]
    sin = sin_ref[...]
    qk = _dot(hb, win_ref[:, 0:2 * QK_A])
    for c in range(H_A):
        qc = qk[:, c * DK_A:(c + 1) * DK_A]
        q_scr[:, c * DK_A:(c + 1) * DK_A] = qc * cos + pltpu.roll(qc, DK_A // 2, 1) * sin
        kc = qk[:, QK_A + c * DK_A:QK_A + (c + 1) * DK_A]
        k_scr[:, c * DK_A:(c + 1) * DK_A] = (kc * cos + pltpu.roll(kc, DK_A // 2, 1) * sin) * (DK_A ** -0.5)
    v_scr[...] = _dot(hb, win_ref[:, 2 * QK_A:2 * QK_A + V_A]).astype(BF)
    gp = _dot(hb, win_ref[:, 2 * QK_A + V_A:2 * QK_A + 2 * V_A])

    row = lax.broadcasted_iota(jnp.int32, (t, DK_A), 0).astype(F32)
    for hd in range(H_A):
        lg = LOG_GAMMA[hd]
        qdec = jnp.exp(lg * row)
        kdec = jnp.exp(lg * (float(t) - row))
        cdec = math.exp(lg * t)
        mask = mask_scr[hd]

        def one_batch(n, carry, hd=hd, qdec=qdec, kdec=kdec, cdec=cdec, mask=mask):
            rows = pl.ds(n * t, t)
            q = q_scr[rows, hd * DK_A:(hd + 1) * DK_A]
            k = k_scr[rows, hd * DK_A:(hd + 1) * DK_A]
            v = v_scr[rows, hd * DV_A:(hd + 1) * DV_A]
            if from_input_state:
                st = sin_st_ref[n, hd]
            else:
                st = st_scr[hd]
            s = _dot_nt(q.astype(BF), k.astype(BF)) * mask
            o = _dot(s.astype(BF), v) + _dot((q * qdec).astype(BF), st.astype(BF))
            st_new = st * cdec + _dot_tn((k * kdec).astype(BF), v)
            og_scr[rows, hd * DV_A:(hd + 1) * DV_A] = o
            if from_input_state:
                so_ref[n, hd] = st_new
            else:
                st_scr[hd] = st_new
            return carry

        for n in range(nb):
            one_batch(n, 0)

    for hd in range(H_A):
        cols = slice(hd * DV_A, (hd + 1) * DV_A)
        on = _rmsn(og_scr[:, cols]) * ggn_ref[:, cols]
        gat_scr[:, cols] = (_silu(gp[:, cols]) * on).astype(BF)
    y = _dot(gat_scr[...], wout_ref[...])
    yn = _rmsn(y) * g_ref[3:4, :]
    o_ref[...] = x + gate * yn.reshape(nb, t, d)

    if not from_input_state:
        @pl.when(j == pl.num_programs(1) - 1)
        def _():
            so_ref[0] = st_scr[...]


def _ret_mixer(x, mod, mod_row0, g, w_in, g_gn, w_out, cos_t, sin_t, state_in, *, nb, t):
    b, s, d = x.shape
    nbt, nst = b // nb, s // t
    r = nb * t
    mrow = mod_row0 // nb
    from_input = state_in is not None
    kern = functools.partial(_ret_kernel, nb=nb, t=t, from_input_state=from_input)
    in_specs = [
        pl.BlockSpec((nb, t, d), lambda i, j: (i, j, 0)),
        pl.BlockSpec((nb, N_MOD, d), lambda i, j: (mrow + i, 0, 0)),
        pl.BlockSpec((6, d), lambda i, j: (0, 0)),
        _resident((d, 2 * QK_A + 2 * V_A), lambda i, j: (0, 0)),
        pl.BlockSpec((1, V_A), lambda i, j: (0, 0)),
        _resident((V_A, d), lambda i, j: (0, 0)),
        pl.BlockSpec((r, DK_A), lambda i, j: (j, 0)),
        pl.BlockSpec((r, DK_A), lambda i, j: (j, 0)),
    ]
    args = [x, mod, g, w_in, g_gn.reshape(1, V_A), w_out, cos_t, sin_t]
    if from_input:
        in_specs.append(pl.BlockSpec((nb, H_A, DK_A, DV_A), lambda i, j: (i, 0, 0, 0)))
        args.append(state_in)
    out, st = pl.pallas_call(
        kern,
        grid=(nbt, nst),
        in_specs=in_specs,
        out_specs=[
            pl.BlockSpec((nb, t, d), lambda i, j: (i, j, 0)),
            pl.BlockSpec((nb, H_A, DK_A, DV_A), lambda i, j: (i, 0, 0, 0)),
        ],
        out_shape=[
            jax.ShapeDtypeStruct(x.shape, F32),
            jax.ShapeDtypeStruct((b, H_A, DK_A, DV_A), F32),
        ],
        scratch_shapes=[
            pltpu.VMEM((r, QK_A), F32),
            pltpu.VMEM((r, QK_A), F32),
            pltpu.VMEM((r, V_A), BF),
            pltpu.VMEM((r, V_A), F32),
            pltpu.VMEM((r, V_A), BF),
            pltpu.VMEM((H_A, t, t), F32),
            pltpu.VMEM((H_A, DK_A, DV_A), F32),
        ],
        compiler_params=_params(2),
        name="ret_mixer",
    )(*args)
    return out, st


def _rope64(blk, cos, sa, sb):
    return blk * cos + pltpu.roll(blk, DH_B // 2, 1) * sa + pltpu.roll(blk, LANES - DH_B // 2, 1) * sb


def _kv_tile(x, mod_ref, g_ref, w_ref, cos_ref, sa_ref, sb_ref, k_ref, v_ref, kb_ref, vb_ref, *, vt_block):
    nb, ts, d = x.shape
    shift = mod_ref[:, 0:1, :]
    scale = mod_ref[:, 1:2, :]
    h = _rmsn(x) * g_ref[...] * (1.0 + scale) + shift
    hb = h.reshape(nb * ts, d).astype(BF)
    kv = _dot(hb, w_ref[...])
    cos = cos_ref[...]
    sa = sa_ref[...]
    sb = sb_ref[...]
    ks = [_rope64(kv[:, c * LANES:(c + 1) * LANES], cos, sa, sb) for c in range(d // LANES)]
    k = jnp.concatenate(ks, axis=1).reshape(nb, ts, d)
    v = kv[:, d:2 * d].reshape(nb, ts, d)
    k_ref[...] = k
    v_ref[...] = v
    kb_ref[...] = k.astype(BF)
    if vt_block is None:
        vb_ref[...] = v.astype(BF)
    else:
        vt = kv[:, d:2 * d].T
        ones = jnp.ones((VT_ROWS - HB_W, vt_block), BF)
        for c in range(ts // vt_block):
            for hd in range(H_B):
                vb_ref[0, c, hd, 0:HB_W, :] = vt[hd * HB_W:(hd + 1) * HB_W,
                                                 c * vt_block:(c + 1) * vt_block].astype(BF)
                vb_ref[0, c, hd, HB_W:VT_ROWS, :] = ones


def _lambda(lam_ref, lam_init):
    lp = lam_ref[...]
    l1 = jnp.sum(lp[0:1, :] * lp[1:2, :], axis=-1, keepdims=True)
    l2 = jnp.sum(lp[2:3, :] * lp[3:4, :], axis=-1, keepdims=True)
    return jnp.exp(l1) - jnp.exp(l2) + lam_init


def _project_q(hb, wq_ref, cos, sa, sb, q_scr, tq):
    q = _dot(hb, wq_ref[...])
    lane = lax.broadcasted_iota(jnp.int32, (tq, LANES), 1)
    first = lane < DH_B
    for c in range(H_B):
        rq = _rope64(q[:, c * LANES:(c + 1) * LANES], cos, sa, sb) * Q_SCALE
        q_scr[0:tq, c * LANES:(c + 1) * LANES] = jnp.where(first, rq, 0.0).astype(BF)
        q_scr[tq:2 * tq, c * LANES:(c + 1) * LANES] = jnp.where(first, 0.0, rq).astype(BF)


def _finish_head(acc, l, lam, gsub, lam_init, tq):
    o = acc[0:tq] / l[0:tq] - lam * (acc[tq:2 * tq] / l[tq:2 * tq])
    return _rmsn(o) * gsub * (1.0 - lam_init)


def _roll_rows(x, shift):
    n = x.shape[0]
    return jnp.concatenate([x[n - shift:], x[:n - shift]], axis=0)


def _diffp_kernel(x_ref, mod_ref, g_ref, wqt_ref, k_ref, vt_ref, cos_ref, sa_ref, sb_ref, lam_ref,
                  gsub_ref, wout_ref, o_ref, qt_scr, m_scr, acc_scr, oh_scr, s_scr, e_scr, b_scr, *, tq, lam_init):
    qi = pl.program_id(1)

    @pl.when(qi == 0)
    def _():
        kr = lax.broadcasted_iota(jnp.int32, (tq, HB_W), 0)
        kc = lax.broadcasted_iota(jnp.int32, (tq, HB_W), 1)
        e_scr[...] = jnp.where((kr >> CHUNK_SHIFT) == kc, 1.0, 0.0).astype(BF)
        ra = lax.broadcasted_iota(jnp.int32, (HB_W, 2 * tq), 0)
        ci = lax.broadcasted_iota(jnp.int32, (HB_W, 2 * tq), 1)
        qchunk = jnp.where(ci >= tq, ci - tq, ci) >> CHUNK_SHIFT
        hidden = jnp.where(ra < tq // CHUNK, jnp.where(ra > qchunk, NEG, 0.0), 0.0)
        b_scr[...] = hidden.astype(BF)

    x = x_ref[0]
    shift = mod_ref[0, 3:4, :]
    scale = mod_ref[0, 4:5, :]
    gate = mod_ref[0, 5:6, :]
    h = _rmsn(x) * g_ref[2:3, :] * (1.0 + scale) + shift
    ht = h.T.astype(BF)
    cos = cos_ref[...]
    sa = sa_ref[...]
    sb = sb_ref[...]
    first = lax.broadcasted_iota(jnp.int32, (HB_W, tq), 0) < DH_B
    half = H_B // 2
    for hd in range(H_B):
        if hd % half == 0:
            qt4 = _dot(wqt_ref[hd * HB_W:(hd + half) * HB_W, :], ht)
        blk = qt4[(hd % half) * HB_W:(hd % half + 1) * HB_W, :]
        rq = (blk * cos + _roll_rows(blk, DH_B // 2) * sa + _roll_rows(blk, HB_W - DH_B // 2) * sb) * Q_SCALE
        qt_scr[hd] = jnp.concatenate([jnp.where(first, rq, 0.0), jnp.where(first, 0.0, rq)], axis=1).astype(BF)

    def block(kb, diagonal):
        rows = pl.ds(pl.multiple_of(kb * tq, tq), tq)
        for hd in range(H_B):
            cols = slice(hd * HB_W, (hd + 1) * HB_W)
            if diagonal:
                k_aug = jnp.concatenate([k_ref[0, rows, cols], e_scr[...]], axis=1)
                q_aug = jnp.concatenate([qt_scr[hd], b_scr[...]], axis=0)
                s_scr[hd] = _dot(k_aug, q_aug)
            else:
                s_scr[hd] = _dot(k_ref[0, rows, cols], qt_scr[hd])
        for hd in range(H_B):
            s = s_scr[hd]
            if diagonal:
                m_new = jnp.max(s, axis=0, keepdims=True)
                acc_scr[hd] = _dot(vt_ref[0, kb, hd], jnp.exp2(s - m_new).astype(BF))
            else:
                m_prev = m_scr[hd]
                m_new = jnp.maximum(m_prev, jnp.max(s, axis=0, keepdims=True))
                p = jnp.exp2(s - m_new).astype(BF)
                acc_scr[hd] = jnp.exp2(m_prev - m_new) * acc_scr[hd] + _dot(vt_ref[0, kb, hd], p)
            m_scr[hd] = m_new

    block(qi, True)

    def body(kb, carry):
        block(kb, False)
        return carry

    lax.fori_loop(0, qi, body, 0)

    lam = _lambda(lam_ref, lam_init)
    for hd in range(H_B):
        acc = acc_scr[hd, 0:HB_W, :]
        inv_l = 1.0 / acc_scr[hd, HB_W:HB_W + 1, :]
        o = acc[:, 0:tq] * inv_l[:, 0:tq] - lam * (acc[:, tq:2 * tq] * inv_l[:, tq:2 * tq])
        on = o * lax.rsqrt(jnp.mean(o * o, axis=0, keepdims=True) + EPS)
        oh_scr[hd * HB_W:(hd + 1) * HB_W, :] = (on * gsub_ref[...] * (1.0 - lam_init)).astype(BF)

    y = _dot_tn(oh_scr[...], wout_ref[...])
    o_ref[0] = x + gate * (_rmsn(y) * g_ref[3:4, :])


def _diff_prompt(x, mod, mod_row0, g, w_qt, kb, vtb, cos_t, sa_t, sb_t, lam_p, g_sub, w_out, lam_init, *, tq):
    b, s, d = x.shape
    kern = functools.partial(_diffp_kernel, tq=tq, lam_init=lam_init)
    tspec = pl.BlockSpec((HB_W, tq), lambda i, j: (0, j))
    return pl.pallas_call(
        kern,
        grid=(b, s // tq),
        in_specs=[
            pl.BlockSpec((1, tq, d), lambda i, j: (i, j, 0)),
            pl.BlockSpec((1, N_MOD, d), lambda i, j: (mod_row0 + i, 0, 0)),
            pl.BlockSpec((6, d), lambda i, j: (0, 0)),
            _resident((d, d), lambda i, j: (0, 0)),
            pl.BlockSpec((1, s, d), lambda i, j: (i, 0, 0)),
            pl.BlockSpec((1, s // tq, H_B, VT_ROWS, tq), lambda i, j: (i, 0, 0, 0, 0)),
            tspec, tspec, tspec,
            pl.BlockSpec((4, DH_B), lambda i, j: (0, 0)),
            pl.BlockSpec((HB_W, 1), lambda i, j: (0, 0)),
            _resident((d, d), lambda i, j: (0, 0)),
        ],
        out_specs=pl.BlockSpec((1, tq, d), lambda i, j: (i, j, 0)),
        out_shape=jax.ShapeDtypeStruct(x.shape, F32),
        scratch_shapes=[
            pltpu.VMEM((H_B, HB_W, 2 * tq), BF),
            pltpu.VMEM((H_B, 1, 2 * tq), F32),
            pltpu.VMEM((H_B, VT_ROWS, 2 * tq), F32),
            pltpu.VMEM((d, tq), BF),
            pltpu.VMEM((H_B, tq, 2 * tq), F32),
            pltpu.VMEM((tq, HB_W), BF),
            pltpu.VMEM((HB_W, 2 * tq), BF),
        ],
        compiler_params=_params(2),
        name="diff_prompt",
    )(x, mod, g, w_qt, kb, vtb, cos_t, sa_t, sb_t, lam_p, g_sub.reshape(HB_W, 1), w_out)


def _diffs_step(kc, n_kc, x_ref, mod_ref, g_ref, wq_ref, kn_ref, vn_ref, kc_ref, vc_ref, cos_ref, sa_ref, sb_ref,
                lam_ref, gsub_ref, wout_ref, o_ref, q_scr, m_scr, acc_scr, oh_scr, s_scr, *, tq, lam_init):
    rows2 = 2 * tq

    def with_ones(v):
        return jnp.concatenate([v, jnp.ones(v.shape, BF)], axis=1)

    @pl.when(kc == 0)
    def _():
        x = x_ref[0]
        h = _rmsn(x) * g_ref[2:3, :] * (1.0 + mod_ref[0, 4:5, :]) + mod_ref[0, 3:4, :]
        _project_q(h.astype(BF), wq_ref, cos_ref[...], sa_ref[...], sb_ref[...], q_scr, tq)
        for hd in range(H_B):
            cols = slice(hd * HB_W, (hd + 1) * HB_W)
            s = _dot_nt(q_scr[:, cols], kn_ref[0, :, cols])
            m0 = jnp.max(s, axis=-1, keepdims=True)
            m_scr[hd] = jnp.broadcast_to(m0, (rows2, LANES))
            acc_scr[hd] = _dot(jnp.exp2(s - m0).astype(BF), with_ones(vn_ref[0, :, cols]))

    tkc = kc_ref.shape[3]
    for hd in range(H_B):
        s_scr[hd] = jnp.concatenate(
            [_dot(q_scr[t * tq:(t + 1) * tq, (2 * hd + t) * DH_B:(2 * hd + t + 1) * DH_B],
                  kc_ref[0, 2 * hd + t].astype(BF)) for t in range(2)], axis=0)
    for hd in range(H_B):
        s = s_scr[hd]
        v_h = with_ones(vc_ref[0, pl.ds(hd, tkc, stride=H_B), :].astype(BF))
        m_prev = m_scr[hd][:, 0:1]
        m_new = jnp.maximum(m_prev, jnp.max(s, axis=-1, keepdims=True))
        acc_scr[hd] = jnp.exp2(m_prev - m_new) * acc_scr[hd] + _dot(jnp.exp2(s - m_new).astype(BF), v_h)
        m_scr[hd] = jnp.broadcast_to(m_new, (rows2, LANES))

    @pl.when(kc == n_kc - 1)
    def _():
        lam = _lambda(lam_ref, lam_init)
        for hd in range(H_B):
            cols = slice(hd * HB_W, (hd + 1) * HB_W)
            oh_scr[:, cols] = _finish_head(acc_scr[hd, :, 0:HB_W], acc_scr[hd, :, HB_W:HB_W + 1], lam,
                                           gsub_ref[...], lam_init, tq).astype(BF)
        y = _dot(oh_scr[...], wout_ref[...])
        o_ref[0] = x_ref[0] + mod_ref[0, 5:6, :] * (_rmsn(y) * g_ref[3:4, :])


N_DIFFS_IN = 14
FFN_PHASES = 4


def _ffn_diffs_kernel(xp_ref, modp_ref, gp_ref, win_ref, wout_ref, *rest, m0, g0, fc, tq, lam_init):
    d_in = rest[:N_DIFFS_IN]
    op_ref, os_ref = rest[N_DIFFS_IN:N_DIFFS_IN + 2]
    h_scr, accf_scr = rest[N_DIFFS_IN + 2:N_DIFFS_IN + 4]
    d_scr = rest[N_DIFFS_IN + 4:]
    d = xp_ref.shape[-1]
    phase = lax.rem(pl.program_id(0), FFN_PHASES)
    n_chunks = D_FF // fc
    per = -(-n_chunks // FFN_PHASES)

    for p in range(FFN_PHASES):
        @pl.when(phase == p)
        def _(p=p):
            if p == 0:
                gain_in = gp_ref[g0:g0 + 1, :] * (1.0 + modp_ref[:, m0 + 1:m0 + 2, :])
                h = _rmsn(xp_ref[...]) * gain_in + modp_ref[:, m0:m0 + 1, :]
                h_scr[...] = h.reshape(-1, d).astype(BF)
            hb = h_scr[...]
            acc = None
            for c in range(p * per, min((p + 1) * per, n_chunks)):
                a = _dot(hb, win_ref[:, c * fc:(c + 1) * fc])
                b = _dot(hb, win_ref[:, D_FF + c * fc:D_FF + (c + 1) * fc])
                part = _dot((_silu(a) * b).astype(BF), wout_ref[c * fc:(c + 1) * fc, :])
                acc = part if acc is None else acc + part
            if p > 0:
                acc = accf_scr[...] + acc
            if p < FFN_PHASES - 1:
                accf_scr[...] = acc
            else:
                x = xp_ref[...]
                gain_out = gp_ref[g0 + 1:g0 + 2, :] * (HALF * modp_ref[:, m0 + 2:m0 + 3, :])
                op_ref[...] = x + _rmsn(acc).reshape(x.shape) * gain_out

    _diffs_step(phase, FFN_PHASES, *d_in, os_ref, *d_scr, tq=tq, lam_init=lam_init)


def _ffn_diff_sample(xp, mod, row_p, g, w_in, w_out, wsel, x, mod1, row_s, g1, w_q, kn, vn, cache_k, cache_v,
                     cos_t, sa_t, sb_t, lam_p, g_sub, w_out_b, lam_init, *, layer, idx, ts):
    bp, sp, d = xp.shape
    b, tq, _ = x.shape
    past = cache_k.shape[1]
    tkc = past // FFN_PHASES
    nst = sp // ts
    assert bp * nst == b
    cache_kt = jnp.transpose(cache_k, (0, 2, 3, 1))
    cache_vf = cache_v.reshape(b, past * H_B, HB_W)

    def tile(t):
        return t // FFN_PHASES

    xp_spec = pl.BlockSpec((1, ts, d), lambda t: (tile(t) // nst, tile(t) % nst, 0))
    tspec = pl.BlockSpec((tq, LANES), lambda t: (0, 0))
    xspec = pl.BlockSpec((1, tq, d), lambda t: (tile(t), 0, 0))
    kspec = pl.BlockSpec((1, 2 * H_B, DH_B, tkc), lambda t: (tile(t), 0, 0, t % FFN_PHASES))
    vspec = pl.BlockSpec((1, tkc * H_B, HB_W), lambda t: (tile(t), t % FFN_PHASES, 0))
    kern = functools.partial(_ffn_diffs_kernel, m0=6 * idx, g0=4 * idx, fc=256, tq=tq, lam_init=lam_init)
    return pl.pallas_call(
        kern,
        grid=(b * FFN_PHASES,),
        in_specs=[
            xp_spec,
            pl.BlockSpec((None, 1, N_MOD, d), lambda t: (layer, row_p + tile(t) // nst, 0, 0)),
            pl.BlockSpec((None, 6, d), lambda t: (layer, 0, 0)),
            _resident((None, d, 2 * D_FF), lambda t: (wsel, 0, 0)),
            _resident((None, D_FF, d), lambda t: (wsel, 0, 0)),
            xspec,
            pl.BlockSpec((1, N_MOD, d), lambda t: (row_s + tile(t), 0, 0)),
            pl.BlockSpec((6, d), lambda t: (0, 0)),
            _resident((d, d), lambda t: (0, 0)),
            xspec, xspec, kspec, vspec,
            tspec, tspec, tspec,
            pl.BlockSpec((4, DH_B), lambda t: (0, 0)),
            pl.BlockSpec((1, HB_W), lambda t: (0, 0)),
            _resident((d, d), lambda t: (0, 0)),
        ],
        out_specs=[xp_spec, xspec],
        out_shape=[jax.ShapeDtypeStruct(xp.shape, F32), jax.ShapeDtypeStruct(x.shape, F32)],
        scratch_shapes=[
            pltpu.VMEM((ts, d), BF),
            pltpu.VMEM((ts, d), F32),
            pltpu.VMEM((2 * tq, d), BF),
            pltpu.VMEM((H_B, 2 * tq, LANES), F32),
            pltpu.VMEM((H_B, 2 * tq, 2 * HB_W), F32),
            pltpu.VMEM((tq, d), BF),
            pltpu.VMEM((H_B, 2 * tq, tkc), F32),
        ],
        compiler_params=_params(1),
        name="ffn_diff_sample",
    )(xp, mod, g, w_in, w_out, x, mod1, g1, w_q, kn, vn, cache_kt, cache_vf, cos_t, sa_t, sb_t, lam_p,
      g_sub.reshape(1, HB_W), w_out_b)


def _rope_tables_a(pos):
    inv = jnp.power(ROPE_THETA, -jnp.arange(0, DK_A, 2, dtype=jnp.float32) / DK_A)
    ang = pos[:, None] * inv[None, :]
    cos, sin = jnp.cos(ang), jnp.sin(ang)
    return jnp.concatenate([cos, cos], axis=-1), jnp.concatenate([-sin, sin], axis=-1)


def _rope_tables_b(pos):
    inv = jnp.power(ROPE_THETA, -jnp.arange(0, DH_B, 2, dtype=jnp.float32) / DH_B)
    ang = pos[:, None] * inv[None, :]
    cos, sin = jnp.cos(ang), jnp.sin(ang)
    zero = jnp.zeros_like(sin)
    cos_t = jnp.concatenate([cos, cos, cos, cos], axis=-1)
    sa = jnp.concatenate([zero, sin, zero, sin], axis=-1)
    sb = jnp.concatenate([-sin, zero, -sin, zero], axis=-1)
    return cos_t, sa, sb


def kernel(x_prompt, x_sample, state_ret, cache_k, cache_v, c_prompt, c_sample, w_ada, b_ada, g_norm, w_ffn_in,
           w_ffn_out, w_in_a, g_gn_a, w_out_a, w_ada_kv, b_ada_kv, g_kv, w_kv, w_q_b, lam_b, g_subln_b, w_out_b):
    d = D_MODEL
    bp, sp, _ = x_prompt.shape
    bs, ss, _ = x_sample.shape

    n_rows = -(-(bs + bp) // 16) * 16
    c_all = jnp.concatenate([c_sample, c_prompt, jnp.zeros((n_rows - bs - bp, d), F32)], axis=0)
    mod = _ada_proj(c_all, w_ada, b_ada).reshape(2, n_rows, N_MOD, d)
    modkv = _ada_proj(c_all, w_ada_kv[None], b_ada_kv[None]).reshape(n_rows, 2, d)
    row_s, row_p = 0, bs

    wfi0 = w_ffn_in[0, 0].astype(BF)[None]
    wfo0 = w_ffn_out[0, 0].astype(BF)[None]
    wqb = w_q_b.astype(BF)
    wob = w_out_b.astype(BF)
    n_ffn = w_ffn_in.shape[0] * w_ffn_in.shape[1]
    ffn_casts = (
        (w_ffn_in.reshape(n_ffn * d, 2 * D_FF), d, (n_ffn - 1) * d, 128),
        (w_ffn_out.reshape(n_ffn * D_FF, d), D_FF, (n_ffn - 1) * D_FF, 352),
        (w_in_a.reshape(d, 2 * QK_A + 2 * V_A), 0, d, 32),
        (w_out_a.reshape(V_A, d), 0, V_A, 64),
        (w_kv, 0, d, 32),
    )

    pos_p = jnp.arange(sp, dtype=jnp.float32)
    pos_s = PAST_LEN + jnp.arange(ss, dtype=jnp.float32)
    nb_s = 16
    nb_r = 4
    ts_p = 512
    t_ret = 256
    tq = 256

    cos_ap, sin_ap = _rope_tables_a(pos_p)
    cos_as, sin_as = [jnp.tile(a, (nb_r, 1)) for a in _rope_tables_a(pos_s)]
    tab_bp = _rope_tables_b(pos_p)
    tab_bs = _rope_tables_b(pos_s)
    tab_bs_tiled = [jnp.tile(a, (nb_s, 1)) for a in tab_bs]

    xp, wfi_rest, wfo_rest, wia, woa, wkv = _ffn(x_prompt, mod, row_p, g_norm, wfi0, wfo0, 0, layer=0, idx=0,
                                                 nb=1, ts=ts_p, casts=ffn_casts)
    wfi_rest = wfi_rest.reshape(n_ffn - 1, d, 2 * D_FF)
    wfo_rest = wfo_rest.reshape(n_ffn - 1, D_FF, d)

    def ffn_p(x, l, i, kv=None):
        return _ffn(x, mod, row_p, g_norm, wfi_rest, wfo_rest, 2 * l + i - 1, layer=l, idx=i, nb=1, ts=ts_p, kv=kv)

    def ffn_s(x, l, i, kv=None):
        if (l, i) == (0, 0):
            return _ffn(x, mod, row_s, g_norm, wfi0, wfo0, 0, layer=l, idx=i, nb=nb_s, ts=ss)
        return _ffn(x, mod, row_s, g_norm, wfi_rest, wfo_rest, 2 * l + i - 1, layer=l, idx=i, nb=nb_s, ts=ss,
                    kv=kv)

    xs = ffn_s(x_sample, 0, 0)
    xp, ret_p = _ret_mixer(xp, mod[0], row_p, g_norm[0], wia, g_gn_a[0], woa, cos_ap, sin_ap, None,
                           nb=1, t=t_ret)
    xs, ret_s = _ret_mixer(xs, mod[0], row_s, g_norm[0], wia, g_gn_a[0], woa, cos_as, sin_as, state_ret[0],
                           nb=nb_r, t=ss)
    xp, k_p, v_p, kb_p, vtb_p = ffn_p(xp, 0, 1, kv=(modkv, g_kv, wkv, tab_bp, tq))
    xs, k_s, v_s, kb_s, vb_s = ffn_s(xs, 0, 1, kv=(modkv, g_kv, wkv, tab_bs_tiled, None))

    lam_init = 0.8 - 0.6 * math.exp(-0.3 * 1)
    xp = ffn_p(xp, 1, 0)
    xs = ffn_s(xs, 1, 0)
    xp = _diff_prompt(xp, mod[1], row_p, g_norm[1], w_q_b[0].T.astype(BF), kb_p, vtb_p,
                      *[a.T for a in tab_bp], lam_b[0], g_subln_b[0], wob[0],
                      lam_init, tq=tq)
    xp, xs = _ffn_diff_sample(xp, mod, row_p, g_norm, wfi_rest, wfo_rest, 2,
                              xs, mod[1], row_s, g_norm[1], wqb[0], kb_s, vb_s, cache_k, cache_v,
                              *tab_bs, lam_b[0], g_subln_b[0], wob[0], lam_init, layer=1, idx=1, ts=ts_p)
    xs = ffn_s(xs, 1, 1)

    return (xp, xs,
            ret_p[None], k_p.reshape(bp, sp, 2 * H_B, DH_B), v_p.reshape(bp, sp, H_B, 2 * DH_B),
            ret_s[None], k_s.reshape(bs, ss, 2 * H_B, DH_B), v_s.reshape(bs, ss, H_B, 2 * DH_B))
```

```python
import functools
import math

import jax
import jax.numpy as jnp
from jax import lax
from jax.experimental import pallas as pl
from jax.experimental.pallas import tpu as pltpu

D_MODEL = 1024
CHUNK = 64
CHUNK_SHIFT = 6
H_A = 8
DK_A = 128
DV_A = 256
QK_A = H_A * DK_A
V_A = H_A * DV_A
H_B = 8
DH_B = 64
HB_W = 2 * DH_B
D_FF = 2816
N_MOD = 9
ROPE_THETA = 10000.0
EPS = 1e-6
HALF = 0.5
PAST_LEN = 4096
NEG = -1e30
LOG2E = 1.4426950408889634
Q_SCALE = DH_B ** -0.5 * LOG2E
VT_ROWS = HB_W + 16

LANES = 128
VMEM_LIMIT = 56 * 1024 * 1024

BF = jnp.bfloat16
F32 = jnp.float32

LOG_GAMMA = [math.log1p(-(2.0 ** (-5 - h))) for h in range(H_A)]


def _dot(a, b):
    return jnp.dot(a, b, preferred_element_type=F32)


def _dot_nt(a, b):
    return lax.dot_general(a, b, (((1,), (1,)), ((), ())), preferred_element_type=F32)


def _dot_tn(a, b):
    return lax.dot_general(a, b, (((0,), (0,)), ((), ())), preferred_element_type=F32)


def _rmsn(x):
    return x * lax.rsqrt(jnp.mean(x * x, axis=-1, keepdims=True) + EPS)


def _silu(x):
    return x * jax.nn.sigmoid(x)


def _resident(shape, index_map):
    return pl.BlockSpec(shape, index_map, pipeline_mode=pl.Buffered(1))


def _params(n_axes):
    return pltpu.CompilerParams(dimension_semantics=("arbitrary",) * n_axes,
                                vmem_limit_bytes=VMEM_LIMIT)


def _ada_kernel(c_ref, w_ref, b_ref, o_ref):
    sc = _silu(c_ref[...]).astype(BF)
    o_ref[...] = _dot(sc, w_ref[...].astype(BF)) + b_ref[...]


def _ada_proj(c_all, w, b, tn=1024):
    nl, d, n = w.shape
    r = c_all.shape[0]
    return pl.pallas_call(
        _ada_kernel,
        grid=(nl, n // tn),
        in_specs=[
            pl.BlockSpec((r, d), lambda l, j: (0, 0)),
            pl.BlockSpec((None, d, tn), lambda l, j: (l, 0, j)),
            pl.BlockSpec((None, 1, tn), lambda l, j: (l, 0, j)),
        ],
        out_specs=pl.BlockSpec((None, r, tn), lambda l, j: (l, 0, j)),
        out_shape=jax.ShapeDtypeStruct((nl, r, n), F32),
        compiler_params=_params(2),
        name="ada_proj",
    )(c_all, w, b.reshape(nl, 1, n))


N_KV_IN, N_KV_OUT = 6, 4


def _ffn_kernel(x_ref, mod_ref, g_ref, win_ref, wout_ref, *rest, m0, g0, fc, n_cast, vt_block, with_kv):
    n_in = (N_KV_IN if with_kv else 0) + n_cast
    kv_in, cast_src = rest[:n_in - n_cast], rest[n_in - n_cast:n_in]
    o_ref = rest[n_in]
    kv_out = rest[n_in + 1:n_in + 1 + (N_KV_OUT if with_kv else 0)]
    cast_dst = rest[n_in + 1 + len(kv_out):]
    for src, dst in zip(cast_src, cast_dst):
        dst[...] = src[...].astype(BF)
    nb, ts, d = x_ref.shape
    x = x_ref[...]
    shift = mod_ref[:, m0:m0 + 1, :]
    gain_in = g_ref[g0:g0 + 1, :] * (1.0 + mod_ref[:, m0 + 1:m0 + 2, :])
    gain_out = g_ref[g0 + 1:g0 + 2, :] * (HALF * mod_ref[:, m0 + 2:m0 + 3, :])
    h = _rmsn(x) * gain_in + shift
    hb = h.reshape(nb * ts, d).astype(BF)
    acc = jnp.zeros((nb * ts, d), F32)
    for c in range(D_FF // fc):
        a = _dot(hb, win_ref[:, c * fc:(c + 1) * fc])
        b = _dot(hb, win_ref[:, D_FF + c * fc:D_FF + (c + 1) * fc])
        u = (_silu(a) * b).astype(BF)
        acc = acc + _dot(u, wout_ref[c * fc:(c + 1) * fc, :])
    out = x + _rmsn(acc).reshape(nb, ts, d) * gain_out
    o_ref[...] = out
    if with_kv:
        _kv_tile(out, *kv_in, *kv_out, vt_block=vt_block)


def _ffn(x, mod, mod_row0, g, w_in, w_out, wsel, *, layer, idx, nb, ts, casts=(), kv=None):
    b, s, d = x.shape
    nbt, nst = b // nb, s // ts
    mrow = mod_row0 // nb
    xspec = pl.BlockSpec((nb, ts, d), lambda i, j: (i, j, 0))
    kv_args, kv_specs, kv_out_specs, kv_shapes, vt_block = [], [], [], [], None
    if kv is not None:
        modkv, g_kv, w_kv, tables, vt_block = kv
        tspec = pl.BlockSpec((nb * ts, LANES), lambda i, j: (j, 0))
        kv_args = [modkv, g_kv.reshape(1, d), w_kv, *tables]
        kv_specs = [pl.BlockSpec((nb, 2, d), lambda i, j: (mrow + i, 0, 0)),
                    pl.BlockSpec((1, d), lambda i, j: (0, 0)),
                    _resident((d, 2 * d), lambda i, j: (0, 0)),
                    tspec, tspec, tspec]
        if vt_block is None:
            vb_spec, vb_shape = xspec, jax.ShapeDtypeStruct(x.shape, BF)
            k_spec, k_shape = xspec, jax.ShapeDtypeStruct(x.shape, F32)
            v_spec, v_shape = xspec, jax.ShapeDtypeStruct(x.shape, F32)
        else:
            assert nb == 1
            vb_spec = pl.BlockSpec((1, ts // vt_block, H_B, VT_ROWS, vt_block), lambda i, j: (i, j, 0, 0, 0))
            vb_shape = jax.ShapeDtypeStruct((b, s // vt_block, H_B, VT_ROWS, vt_block), BF)
            k_spec = pl.BlockSpec((1, 2 * H_B, DH_B, ts), lambda i, j: (i, 0, 0, j))
            k_shape = jax.ShapeDtypeStruct((b, 2 * H_B, DH_B, s), F32)
            v_spec = pl.BlockSpec((1, ts * H_B, HB_W), lambda i, j: (i, j, 0))
            v_shape = jax.ShapeDtypeStruct((b, s * H_B, HB_W), F32)
        kv_out_specs = [k_spec, v_spec, xspec, vb_spec]
        kv_shapes = [k_shape, v_shape, jax.ShapeDtypeStruct(x.shape, BF), vb_shape]
    cast_args, cast_specs, cast_out_specs, cast_shapes = [], [], [], []
    for w, row0, rows, step in casts:
        assert row0 % step == 0 and rows % step == 0 and rows // step <= nbt * nst
        first, last = row0 // step, rows // step - 1
        cast_args.append(w)
        cast_specs.append(pl.BlockSpec((step, w.shape[1]),
                                       lambda i, j, first=first, last=last: (first + jnp.minimum(i * nst + j, last), 0)))
        cast_out_specs.append(pl.BlockSpec((step, w.shape[1]),
                                           lambda i, j, last=last: (jnp.minimum(i * nst + j, last), 0)))
        cast_shapes.append(jax.ShapeDtypeStruct((rows, w.shape[1]), BF))
    kern = functools.partial(_ffn_kernel, m0=6 * idx, g0=4 * idx, fc=256, n_cast=len(casts),
                             vt_block=vt_block, with_kv=kv is not None)
    outs = pl.pallas_call(
        kern,
        grid=(nbt, nst),
        in_specs=[
            xspec,
            pl.BlockSpec((None, nb, N_MOD, d), lambda i, j: (layer, mrow + i, 0, 0)),
            pl.BlockSpec((None, 6, d), lambda i, j: (layer, 0, 0)),
            _resident((None, d, 2 * D_FF), lambda i, j: (wsel, 0, 0)),
            _resident((None, D_FF, d), lambda i, j: (wsel, 0, 0)),
        ] + kv_specs + cast_specs,
        out_specs=[xspec] + kv_out_specs + cast_out_specs,
        out_shape=[jax.ShapeDtypeStruct(x.shape, F32)] + kv_shapes + cast_shapes,
        compiler_params=_params(2),
        name="ffn_kv" if kv is not None else "ffn",
    )(x, mod, g, w_in, w_out, *kv_args, *cast_args)
    return outs if (casts or kv is not None) else outs[0]


def _decay_mask(lg, t):
    i = lax.broadcasted_iota(jnp.int32, (t, t), 0)
    j = lax.broadcasted_iota(jnp.int32, (t, t), 1)
    dist = jnp.abs(i - j).astype(F32)
    vis = (j >> CHUNK_SHIFT) <= (i >> CHUNK_SHIFT)
    return jnp.where(vis, jnp.exp(lg * dist), 0.0)


def _ret_kernel(*refs, nb, t, from_input_state):
    if from_input_state:
        (x_ref, mod_ref, g_ref, win_ref, ggn_ref, wout_ref, cos_ref, sin_ref, sin_st_ref,
         o_ref, so_ref, q_scr, k_scr, v_scr, og_scr, gat_scr, mask_scr, st_scr) = refs
    else:
        (x_ref, mod_ref, g_ref, win_ref, ggn_ref, wout_ref, cos_ref, sin_ref,
         o_ref, so_ref, q_scr, k_scr, v_scr, og_scr, gat_scr, mask_scr, st_scr) = refs
        sin_st_ref = None
    d = D_MODEL
    r = nb * t
    j = pl.program_id(1)

    @pl.when(j == 0)
    def _():
        for hd in range(H_A):
            mask_scr[hd] = _decay_mask(LOG_GAMMA[hd], t)
        if not from_input_state:
            st_scr[...] = jnp.zeros_like(st_scr)

    x = x_ref[...]
    shift = mod_ref[:, 3:4, :]
    scale = mod_ref[:, 4:5, :]
    gate = mod_ref[:, 5:6, :]
    h = _rmsn(x) * g_ref[2:3, :] * (1.0 + scale) + shift
    hb = h.reshape(r, d).astype(BF)

    cos = cos_ref[...]
    sin = sin_ref[...]
    qk = _dot(hb, win_ref[:, 0:2 * QK_A])
    for c in range(H_A):
        qc = qk[:, c * DK_A:(c + 1) * DK_A]
        q_scr[:, c * DK_A:(c + 1) * DK_A] = qc * cos + pltpu.roll(qc, DK_A // 2, 1) * sin
        kc = qk[:, QK_A + c * DK_A:QK_A + (c + 1) * DK_A]
        k_scr[:, c * DK_A:(c + 1) * DK_A] = (kc * cos + pltpu.roll(kc, DK_A // 2, 1) * sin) * (DK_A ** -0.5)
    v_scr[...] = _dot(hb, win_ref[:, 2 * QK_A:2 * QK_A + V_A]).astype(BF)
    gp = _dot(hb, win_ref[:, 2 * QK_A + V_A:2 * QK_A + 2 * V_A])

    row = lax.broadcasted_iota(jnp.int32, (t, DK_A), 0).astype(F32)
    for hd in range(H_A):
        lg = LOG_GAMMA[hd]
        qdec = jnp.exp(lg * row)
        kdec = jnp.exp(lg * (float(t) - row))
        cdec = math.exp(lg * t)
        mask = mask_scr[hd]

        def one_batch(n, carry, hd=hd, qdec=qdec, kdec=kdec, cdec=cdec, mask=mask):
            rows = pl.ds(n * t, t)
            q = q_scr[rows, hd * DK_A:(hd + 1) * DK_A]
            k = k_scr[rows, hd * DK_A:(hd + 1) * DK_A]
            v = v_scr[rows, hd * DV_A:(hd + 1) * DV_A]
            if from_input_state:
                st = sin_st_ref[n, hd]
            else:
                st = st_scr[hd]
            s = _dot_nt(q.astype(BF), k.astype(BF)) * mask
            o = _dot(s.astype(BF), v) + _dot((q * qdec).astype(BF), st.astype(BF))
            st_new = st * cdec + _dot_tn((k * kdec).astype(BF), v)
            og_scr[rows, hd * DV_A:(hd + 1) * DV_A] = o
            if from_input_state:
                so_ref[n, hd] = st_new
            else:
                st_scr[hd] = st_new
            return carry

        for n in range(nb):
            one_batch(n, 0)

    for hd in range(H_A):
        cols = slice(hd * DV_A, (hd + 1) * DV_A)
        on = _rmsn(og_scr[:, cols]) * ggn_ref[:, cols]
        gat_scr[:, cols] = (_silu(gp[:, cols]) * on).astype(BF)
    y = _dot(gat_scr[...], wout_ref[...])
    yn = _rmsn(y) * g_ref[3:4, :]
    o_ref[...] = x + gate * yn.reshape(nb, t, d)

    if not from_input_state:
        @pl.when(j == pl.num_programs(1) - 1)
        def _():
            so_ref[0] = st_scr[...]


def _ret_mixer(x, mod, mod_row0, g, w_in, g_gn, w_out, cos_t, sin_t, state_in, *, nb, t):
    b, s, d = x.shape
    nbt, nst = b // nb, s // t
    r = nb * t
    mrow = mod_row0 // nb
    from_input = state_in is not None
    kern = functools.partial(_ret_kernel, nb=nb, t=t, from_input_state=from_input)
    in_specs = [
        pl.BlockSpec((nb, t, d), lambda i, j: (i, j, 0)),
        pl.BlockSpec((nb, N_MOD, d), lambda i, j: (mrow + i, 0, 0)),
        pl.BlockSpec((6, d), lambda i, j: (0, 0)),
        _resident((d, 2 * QK_A + 2 * V_A), lambda i, j: (0, 0)),
        pl.BlockSpec((1, V_A), lambda i, j: (0, 0)),
        _resident((V_A, d), lambda i, j: (0, 0)),
        pl.BlockSpec((r, DK_A), lambda i, j: (j, 0)),
        pl.BlockSpec((r, DK_A), lambda i, j: (j, 0)),
    ]
    args = [x, mod, g, w_in, g_gn.reshape(1, V_A), w_out, cos_t, sin_t]
    if from_input:
        in_specs.append(pl.BlockSpec((nb, H_A, DK_A, DV_A), lambda i, j: (i, 0, 0, 0)))
        args.append(state_in)
    out, st = pl.pallas_call(
        kern,
        grid=(nbt, nst),
        in_specs=in_specs,
        out_specs=[
            pl.BlockSpec((nb, t, d), lambda i, j: (i, j, 0)),
            pl.BlockSpec((nb, H_A, DK_A, DV_A), lambda i, j: (i, 0, 0, 0)),
        ],
        out_shape=[
            jax.ShapeDtypeStruct(x.shape, F32),
            jax.ShapeDtypeStruct((b, H_A, DK_A, DV_A), F32),
        ],
        scratch_shapes=[
            pltpu.VMEM((r, QK_A), F32),
            pltpu.VMEM((r, QK_A), F32),
            pltpu.VMEM((r, V_A), BF),
            pltpu.VMEM((r, V_A), F32),
            pltpu.VMEM((r, V_A), BF),
            pltpu.VMEM((H_A, t, t), F32),
            pltpu.VMEM((H_A, DK_A, DV_A), F32),
        ],
        compiler_params=_params(2),
        name="ret_mixer",
    )(*args)
    return out, st


def _rope64(blk, cos, sa, sb):
    return blk * cos + pltpu.roll(blk, DH_B // 2, 1) * sa + pltpu.roll(blk, LANES - DH_B // 2, 1) * sb


def _kv_tile(x, mod_ref, g_ref, w_ref, cos_ref, sa_ref, sb_ref, k_ref, v_ref, kb_ref, vb_ref, *, vt_block):
    nb, ts, d = x.shape
    shift = mod_ref[:, 0:1, :]
    scale = mod_ref[:, 1:2, :]
    h = _rmsn(x) * g_ref[...] * (1.0 + scale) + shift
    hb = h.reshape(nb * ts, d).astype(BF)
    kv = _dot(hb, w_ref[...])
    cos = cos_ref[...]
    sa = sa_ref[...]
    sb = sb_ref[...]
    ks = [_rope64(kv[:, c * LANES:(c + 1) * LANES], cos, sa, sb) for c in range(d // LANES)]
    k2 = jnp.concatenate(ks, axis=1)
    k = k2.reshape(nb, ts, d)
    v = kv[:, d:2 * d].reshape(nb, ts, d)
    kb_ref[...] = k.astype(BF)
    if vt_block is None:
        k_ref[...] = k
        v_ref[...] = v
        vb_ref[...] = v.astype(BF)
    else:
        k_ref[0] = k2.T.reshape(2 * H_B, DH_B, ts)
        for hd in range(H_B):
            v_ref[0, pl.ds(hd, ts, stride=H_B), :] = kv[:, d + hd * HB_W:d + (hd + 1) * HB_W]
        vt = kv[:, d:2 * d].T
        ones = jnp.ones((VT_ROWS - HB_W, vt_block), BF)
        for c in range(ts // vt_block):
            for hd in range(H_B):
                vb_ref[0, c, hd, 0:HB_W, :] = vt[hd * HB_W:(hd + 1) * HB_W,
                                                 c * vt_block:(c + 1) * vt_block].astype(BF)
                vb_ref[0, c, hd, HB_W:VT_ROWS, :] = ones


def _lambda(lam_ref, lam_init):
    lp = lam_ref[...]
    l1 = jnp.sum(lp[0:1, :] * lp[1:2, :], axis=-1, keepdims=True)
    l2 = jnp.sum(lp[2:3, :] * lp[3:4, :], axis=-1, keepdims=True)
    return jnp.exp(l1) - jnp.exp(l2) + lam_init


def _project_q(hb, wq_ref, cos, sa, sb, q_scr, tq):
    q = _dot(hb, wq_ref[...])
    lane = lax.broadcasted_iota(jnp.int32, (tq, LANES), 1)
    first = lane < DH_B
    for c in range(H_B):
        rq = _rope64(q[:, c * LANES:(c + 1) * LANES], cos, sa, sb) * Q_SCALE
        q_scr[0:tq, c * LANES:(c + 1) * LANES] = jnp.where(first, rq, 0.0).astype(BF)
        q_scr[tq:2 * tq, c * LANES:(c + 1) * LANES] = jnp.where(first, 0.0, rq).astype(BF)


def _finish_head(acc, l, lam, gsub, lam_init, tq):
    o = acc[0:tq] / l[0:tq] - lam * (acc[tq:2 * tq] / l[tq:2 * tq])
    return _rmsn(o) * gsub * (1.0 - lam_init)


def _roll_rows(x, shift):
    n = x.shape[0]
    return jnp.concatenate([x[n - shift:], x[:n - shift]], axis=0)


def _diffp_kernel(x_ref, mod_ref, g_ref, wqt_ref, k_ref, vt_ref, cos_ref, sa_ref, sb_ref, lam_ref,
                  gsub_ref, wout_ref, o_ref, qt_scr, m_scr, acc_scr, oh_scr, s_scr, e_scr, b_scr, *, tq, lam_init):
    qi = pl.program_id(1)

    @pl.when(qi == 0)
    def _():
        kr = lax.broadcasted_iota(jnp.int32, (tq, HB_W), 0)
        kc = lax.broadcasted_iota(jnp.int32, (tq, HB_W), 1)
        e_scr[...] = jnp.where((kr >> CHUNK_SHIFT) == kc, 1.0, 0.0).astype(BF)
        ra = lax.broadcasted_iota(jnp.int32, (HB_W, 2 * tq), 0)
        ci = lax.broadcasted_iota(jnp.int32, (HB_W, 2 * tq), 1)
        qchunk = jnp.where(ci >= tq, ci - tq, ci) >> CHUNK_SHIFT
        hidden = jnp.where(ra < tq // CHUNK, jnp.where(ra > qchunk, NEG, 0.0), 0.0)
        b_scr[...] = hidden.astype(BF)

    x = x_ref[0]
    shift = mod_ref[0, 3:4, :]
    scale = mod_ref[0, 4:5, :]
    gate = mod_ref[0, 5:6, :]
    h = _rmsn(x) * g_ref[2:3, :] * (1.0 + scale) + shift
    ht = h.T.astype(BF)
    cos = cos_ref[...]
    sa = sa_ref[...]
    sb = sb_ref[...]
    first = lax.broadcasted_iota(jnp.int32, (HB_W, tq), 0) < DH_B
    half = H_B // 2
    for hd in range(H_B):
        if hd % half == 0:
            qt4 = _dot(wqt_ref[hd * HB_W:(hd + half) * HB_W, :], ht)
        blk = qt4[(hd % half) * HB_W:(hd % half + 1) * HB_W, :]
        rq = (blk * cos + _roll_rows(blk, DH_B // 2) * sa + _roll_rows(blk, HB_W - DH_B // 2) * sb) * Q_SCALE
        qt_scr[hd] = jnp.concatenate([jnp.where(first, rq, 0.0), jnp.where(first, 0.0, rq)], axis=1).astype(BF)

    def block(kb, diagonal):
        rows = pl.ds(pl.multiple_of(kb * tq, tq), tq)
        for hd in range(H_B):
            cols = slice(hd * HB_W, (hd + 1) * HB_W)
            if diagonal:
                k_aug = jnp.concatenate([k_ref[0, rows, cols], e_scr[...]], axis=1)
                q_aug = jnp.concatenate([qt_scr[hd], b_scr[...]], axis=0)
                s_scr[hd] = _dot(k_aug, q_aug)
            else:
                s_scr[hd] = _dot(k_ref[0, rows, cols], qt_scr[hd])
        for hd in range(H_B):
            s = s_scr[hd]
            if diagonal:
                m_new = jnp.max(s, axis=0, keepdims=True)
                acc_scr[hd] = _dot(vt_ref[0, kb, hd], jnp.exp2(s - m_new).astype(BF))
            else:
                m_prev = m_scr[hd]
                m_new = jnp.maximum(m_prev, jnp.max(s, axis=0, keepdims=True))
                p = jnp.exp2(s - m_new).astype(BF)
                acc_scr[hd] = jnp.exp2(m_prev - m_new) * acc_scr[hd] + _dot(vt_ref[0, kb, hd], p)
            m_scr[hd] = m_new

    block(qi, True)

    def body(kb, carry):
        block(kb, False)
        return carry

    lax.fori_loop(0, qi, body, 0)

    lam = _lambda(lam_ref, lam_init)
    for hd in range(H_B):
        acc = acc_scr[hd, 0:HB_W, :]
        inv_l = 1.0 / acc_scr[hd, HB_W:HB_W + 1, :]
        o = acc[:, 0:tq] * inv_l[:, 0:tq] - lam * (acc[:, tq:2 * tq] * inv_l[:, tq:2 * tq])
        on = o * lax.rsqrt(jnp.mean(o * o, axis=0, keepdims=True) + EPS)
        oh_scr[hd * HB_W:(hd + 1) * HB_W, :] = (on * gsub_ref[...] * (1.0 - lam_init)).astype(BF)

    y = _dot_tn(oh_scr[...], wout_ref[...])
    o_ref[0] = x + gate * (_rmsn(y) * g_ref[3:4, :])


def _diff_prompt(x, mod, mod_row0, g, w_qt, kb, vtb, cos_t, sa_t, sb_t, lam_p, g_sub, w_out, lam_init, *, tq):
    b, s, d = x.shape
    kern = functools.partial(_diffp_kernel, tq=tq, lam_init=lam_init)
    tspec = pl.BlockSpec((HB_W, tq), lambda i, j: (0, j))
    return pl.pallas_call(
        kern,
        grid=(b, s // tq),
        in_specs=[
            pl.BlockSpec((1, tq, d), lambda i, j: (i, j, 0)),
            pl.BlockSpec((1, N_MOD, d), lambda i, j: (mod_row0 + i, 0, 0)),
            pl.BlockSpec((6, d), lambda i, j: (0, 0)),
            _resident((d, d), lambda i, j: (0, 0)),
            pl.BlockSpec((1, s, d), lambda i, j: (i, 0, 0)),
            pl.BlockSpec((1, s // tq, H_B, VT_ROWS, tq), lambda i, j: (i, 0, 0, 0, 0)),
            tspec, tspec, tspec,
            pl.BlockSpec((4, DH_B), lambda i, j: (0, 0)),
            pl.BlockSpec((HB_W, 1), lambda i, j: (0, 0)),
            _resident((d, d), lambda i, j: (0, 0)),
        ],
        out_specs=pl.BlockSpec((1, tq, d), lambda i, j: (i, j, 0)),
        out_shape=jax.ShapeDtypeStruct(x.shape, F32),
        scratch_shapes=[
            pltpu.VMEM((H_B, HB_W, 2 * tq), BF),
            pltpu.VMEM((H_B, 1, 2 * tq), F32),
            pltpu.VMEM((H_B, VT_ROWS, 2 * tq), F32),
            pltpu.VMEM((d, tq), BF),
            pltpu.VMEM((H_B, tq, 2 * tq), F32),
            pltpu.VMEM((tq, HB_W), BF),
            pltpu.VMEM((HB_W, 2 * tq), BF),
        ],
        compiler_params=_params(2),
        name="diff_prompt",
    )(x, mod, g, w_qt, kb, vtb, cos_t, sa_t, sb_t, lam_p, g_sub.reshape(HB_W, 1), w_out)


def _diffs_step(kc, n_kc, x_ref, mod_ref, g_ref, wq_ref, kn_ref, vn_ref, kc_ref, vc_ref, cos_ref, sa_ref, sb_ref,
                lam_ref, gsub_ref, wout_ref, o_ref, q_scr, m_scr, acc_scr, oh_scr, s_scr, *, tq, lam_init):
    rows2 = 2 * tq

    def with_ones(v):
        return jnp.concatenate([v, jnp.ones(v.shape, BF)], axis=1)

    @pl.when(kc == 0)
    def _():
        x = x_ref[0]
        h = _rmsn(x) * g_ref[2:3, :] * (1.0 + mod_ref[0, 4:5, :]) + mod_ref[0, 3:4, :]
        _project_q(h.astype(BF), wq_ref, cos_ref[...], sa_ref[...], sb_ref[...], q_scr, tq)
        for hd in range(H_B):
            cols = slice(hd * HB_W, (hd + 1) * HB_W)
            s = _dot_nt(q_scr[:, cols], kn_ref[0, :, cols])
            m0 = jnp.max(s, axis=-1, keepdims=True)
            m_scr[hd] = jnp.broadcast_to(m0, (rows2, LANES))
            acc_scr[hd] = _dot(jnp.exp2(s - m0).astype(BF), with_ones(vn_ref[0, :, cols]))

    tkc = kc_ref.shape[3]
    for hd in range(H_B):
        s_scr[hd] = jnp.concatenate(
            [_dot(q_scr[t * tq:(t + 1) * tq, (2 * hd + t) * DH_B:(2 * hd + t + 1) * DH_B],
                  kc_ref[0, 2 * hd + t].astype(BF)) for t in range(2)], axis=0)
    for hd in range(H_B):
        s = s_scr[hd]
        v_h = with_ones(vc_ref[0, pl.ds(hd, tkc, stride=H_B), :].astype(BF))
        m_prev = m_scr[hd][:, 0:1]
        m_new = jnp.maximum(m_prev, jnp.max(s, axis=-1, keepdims=True))
        acc_scr[hd] = jnp.exp2(m_prev - m_new) * acc_scr[hd] + _dot(jnp.exp2(s - m_new).astype(BF), v_h)
        m_scr[hd] = jnp.broadcast_to(m_new, (rows2, LANES))

    @pl.when(kc == n_kc - 1)
    def _():
        lam = _lambda(lam_ref, lam_init)
        for hd in range(H_B):
            cols = slice(hd * HB_W, (hd + 1) * HB_W)
            oh_scr[:, cols] = _finish_head(acc_scr[hd, :, 0:HB_W], acc_scr[hd, :, HB_W:HB_W + 1], lam,
                                           gsub_ref[...], lam_init, tq).astype(BF)
        y = _dot(oh_scr[...], wout_ref[...])
        o_ref[0] = x_ref[0] + mod_ref[0, 5:6, :] * (_rmsn(y) * g_ref[3:4, :])


N_DIFFS_IN = 14
FFN_PHASES = 4


def _ffn_diffs_kernel(xp_ref, modp_ref, gp_ref, win_ref, wout_ref, *rest, m0, g0, fc, tq, lam_init):
    d_in = rest[:N_DIFFS_IN]
    op_ref, os_ref = rest[N_DIFFS_IN:N_DIFFS_IN + 2]
    h_scr, accf_scr = rest[N_DIFFS_IN + 2:N_DIFFS_IN + 4]
    d_scr = rest[N_DIFFS_IN + 4:]
    d = xp_ref.shape[-1]
    phase = lax.rem(pl.program_id(0), FFN_PHASES)
    n_chunks = D_FF // fc
    per = -(-n_chunks // FFN_PHASES)

    for p in range(FFN_PHASES):
        @pl.when(phase == p)
        def _(p=p):
            if p == 0:
                gain_in = gp_ref[g0:g0 + 1, :] * (1.0 + modp_ref[:, m0 + 1:m0 + 2, :])
                h = _rmsn(xp_ref[...]) * gain_in + modp_ref[:, m0:m0 + 1, :]
                h_scr[...] = h.reshape(-1, d).astype(BF)
            hb = h_scr[...]
            acc = None
            for c in range(p * per, min((p + 1) * per, n_chunks)):
                a = _dot(hb, win_ref[:, c * fc:(c + 1) * fc])
                b = _dot(hb, win_ref[:, D_FF + c * fc:D_FF + (c + 1) * fc])
                part = _dot((_silu(a) * b).astype(BF), wout_ref[c * fc:(c + 1) * fc, :])
                acc = part if acc is None else acc + part
            if p > 0:
                acc = accf_scr[...] + acc
            if p < FFN_PHASES - 1:
                accf_scr[...] = acc
            else:
                x = xp_ref[...]
                gain_out = gp_ref[g0 + 1:g0 + 2, :] * (HALF * modp_ref[:, m0 + 2:m0 + 3, :])
                op_ref[...] = x + _rmsn(acc).reshape(x.shape) * gain_out

    _diffs_step(phase, FFN_PHASES, *d_in, os_ref, *d_scr, tq=tq, lam_init=lam_init)


def _ffn_diff_sample(xp, mod, row_p, g, w_in, w_out, wsel, x, mod1, row_s, g1, w_q, kn, vn, cache_k, cache_v,
                     cos_t, sa_t, sb_t, lam_p, g_sub, w_out_b, lam_init, *, layer, idx, ts):
    bp, sp, d = xp.shape
    b, tq, _ = x.shape
    past = cache_k.shape[1]
    tkc = past // FFN_PHASES
    nst = sp // ts
    assert bp * nst == b
    cache_kt = jnp.transpose(cache_k, (0, 2, 3, 1))
    cache_vf = cache_v.reshape(b, past * H_B, HB_W)

    def tile(t):
        return t // FFN_PHASES

    xp_spec = pl.BlockSpec((1, ts, d), lambda t: (tile(t) // nst, tile(t) % nst, 0))
    tspec = pl.BlockSpec((tq, LANES), lambda t: (0, 0))
    xspec = pl.BlockSpec((1, tq, d), lambda t: (tile(t), 0, 0))
    kspec = pl.BlockSpec((1, 2 * H_B, DH_B, tkc), lambda t: (tile(t), 0, 0, t % FFN_PHASES))
    vspec = pl.BlockSpec((1, tkc * H_B, HB_W), lambda t: (tile(t), t % FFN_PHASES, 0))
    kern = functools.partial(_ffn_diffs_kernel, m0=6 * idx, g0=4 * idx, fc=256, tq=tq, lam_init=lam_init)
    return pl.pallas_call(
        kern,
        grid=(b * FFN_PHASES,),
        in_specs=[
            xp_spec,
            pl.BlockSpec((None, 1, N_MOD, d), lambda t: (layer, row_p + tile(t) // nst, 0, 0)),
            pl.BlockSpec((None, 6, d), lambda t: (layer, 0, 0)),
            _resident((None, d, 2 * D_FF), lambda t: (wsel, 0, 0)),
            _resident((None, D_FF, d), lambda t: (wsel, 0, 0)),
            xspec,
            pl.BlockSpec((1, N_MOD, d), lambda t: (row_s + tile(t), 0, 0)),
            pl.BlockSpec((6, d), lambda t: (0, 0)),
            _resident((d, d), lambda t: (0, 0)),
            xspec, xspec, kspec, vspec,
            tspec, tspec, tspec,
            pl.BlockSpec((4, DH_B), lambda t: (0, 0)),
            pl.BlockSpec((1, HB_W), lambda t: (0, 0)),
            _resident((d, d), lambda t: (0, 0)),
        ],
        out_specs=[xp_spec, xspec],
        out_shape=[jax.ShapeDtypeStruct(xp.shape, F32), jax.ShapeDtypeStruct(x.shape, F32)],
        scratch_shapes=[
            pltpu.VMEM((ts, d), BF),
            pltpu.VMEM((ts, d), F32),
            pltpu.VMEM((2 * tq, d), BF),
            pltpu.VMEM((H_B, 2 * tq, LANES), F32),
            pltpu.VMEM((H_B, 2 * tq, 2 * HB_W), F32),
            pltpu.VMEM((tq, d), BF),
            pltpu.VMEM((H_B, 2 * tq, tkc), F32),
        ],
        compiler_params=_params(1),
        name="ffn_diff_sample",
    )(xp, mod, g, w_in, w_out, x, mod1, g1, w_q, kn, vn, cache_kt, cache_vf, cos_t, sa_t, sb_t, lam_p,
      g_sub.reshape(1, HB_W), w_out_b)


def _rope_tables_a(pos):
    inv = jnp.power(ROPE_THETA, -jnp.arange(0, DK_A, 2, dtype=jnp.float32) / DK_A)
    ang = pos[:, None] * inv[None, :]
    cos, sin = jnp.cos(ang), jnp.sin(ang)
    return jnp.concatenate([cos, cos], axis=-1), jnp.concatenate([-sin, sin], axis=-1)


def _rope_tables_b(pos):
    inv = jnp.power(ROPE_THETA, -jnp.arange(0, DH_B, 2, dtype=jnp.float32) / DH_B)
    ang = pos[:, None] * inv[None, :]
    cos, sin = jnp.cos(ang), jnp.sin(ang)
    zero = jnp.zeros_like(sin)
    cos_t = jnp.concatenate([cos, cos, cos, cos], axis=-1)
    sa = jnp.concatenate([zero, sin, zero, sin], axis=-1)
    sb = jnp.concatenate([-sin, zero, -sin, zero], axis=-1)
    return cos_t, sa, sb


def kernel(x_prompt, x_sample, state_ret, cache_k, cache_v, c_prompt, c_sample, w_ada, b_ada, g_norm, w_ffn_in,
           w_ffn_out, w_in_a, g_gn_a, w_out_a, w_ada_kv, b_ada_kv, g_kv, w_kv, w_q_b, lam_b, g_subln_b, w_out_b):
    d = D_MODEL
    bp, sp, _ = x_prompt.shape
    bs, ss, _ = x_sample.shape

    n_rows = -(-(bs + bp) // 16) * 16
    c_all = jnp.concatenate([c_sample, c_prompt, jnp.zeros((n_rows - bs - bp, d), F32)], axis=0)
    mod = _ada_proj(c_all, w_ada, b_ada).reshape(2, n_rows, N_MOD, d)
    modkv = _ada_proj(c_all, w_ada_kv[None], b_ada_kv[None]).reshape(n_rows, 2, d)
    row_s, row_p = 0, bs

    wfi0 = w_ffn_in[0, 0].astype(BF)[None]
    wfo0 = w_ffn_out[0, 0].astype(BF)[None]
    wqb = w_q_b.astype(BF)
    wob = w_out_b.astype(BF)
    n_ffn = w_ffn_in.shape[0] * w_ffn_in.shape[1]
    ffn_casts = (
        (w_ffn_in.reshape(n_ffn * d, 2 * D_FF), d, (n_ffn - 1) * d, 128),
        (w_ffn_out.reshape(n_ffn * D_FF, d), D_FF, (n_ffn - 1) * D_FF, 352),
        (w_in_a.reshape(d, 2 * QK_A + 2 * V_A), 0, d, 32),
        (w_out_a.reshape(V_A, d), 0, V_A, 64),
        (w_kv, 0, d, 32),
    )

    pos_p = jnp.arange(sp, dtype=jnp.float32)
    pos_s = PAST_LEN + jnp.arange(ss, dtype=jnp.float32)
    nb_s = 16
    nb_r = 4
    ts_p = 512
    t_ret = 256
    tq = 256

    cos_ap, sin_ap = _rope_tables_a(pos_p)
    cos_as, sin_as = [jnp.tile(a, (nb_r, 1)) for a in _rope_tables_a(pos_s)]
    tab_bp = _rope_tables_b(pos_p)
    tab_bs = _rope_tables_b(pos_s)
    tab_bs_tiled = [jnp.tile(a, (nb_s, 1)) for a in tab_bs]

    xp, wfi_rest, wfo_rest, wia, woa, wkv = _ffn(x_prompt, mod, row_p, g_norm, wfi0, wfo0, 0, layer=0, idx=0,
                                                 nb=1, ts=ts_p, casts=ffn_casts)
    wfi_rest = wfi_rest.reshape(n_ffn - 1, d, 2 * D_FF)
    wfo_rest = wfo_rest.reshape(n_ffn - 1, D_FF, d)

    def ffn_p(x, l, i, kv=None):
        return _ffn(x, mod, row_p, g_norm, wfi_rest, wfo_rest, 2 * l + i - 1, layer=l, idx=i, nb=1, ts=ts_p, kv=kv)

    def ffn_s(x, l, i, kv=None):
        if (l, i) == (0, 0):
            return _ffn(x, mod, row_s, g_norm, wfi0, wfo0, 0, layer=l, idx=i, nb=nb_s, ts=ss)
        return _ffn(x, mod, row_s, g_norm, wfi_rest, wfo_rest, 2 * l + i - 1, layer=l, idx=i, nb=nb_s, ts=ss,
                    kv=kv)

    xs = ffn_s(x_sample, 0, 0)
    xp, ret_p = _ret_mixer(xp, mod[0], row_p, g_norm[0], wia, g_gn_a[0], woa, cos_ap, sin_ap, None,
                           nb=1, t=t_ret)
    xs, ret_s = _ret_mixer(xs, mod[0], row_s, g_norm[0], wia, g_gn_a[0], woa, cos_as, sin_as, state_ret[0],
                           nb=nb_r, t=ss)
    xp, k_p, v_p, kb_p, vtb_p = ffn_p(xp, 0, 1, kv=(modkv, g_kv, wkv, tab_bp, tq))
    xs, k_s, v_s, kb_s, vb_s = ffn_s(xs, 0, 1, kv=(modkv, g_kv, wkv, tab_bs_tiled, None))

    lam_init = 0.8 - 0.6 * math.exp(-0.3 * 1)
    xp = ffn_p(xp, 1, 0)
    xs = ffn_s(xs, 1, 0)
    xp = _diff_prompt(xp, mod[1], row_p, g_norm[1], w_q_b[0].T.astype(BF), kb_p, vtb_p,
                      *[a.T for a in tab_bp], lam_b[0], g_subln_b[0], wob[0],
                      lam_init, tq=tq)
    xp, xs = _ffn_diff_sample(xp, mod, row_p, g_norm, wfi_rest, wfo_rest, 2,
                              xs, mod[1], row_s, g_norm[1], wqb[0], kb_s, vb_s, cache_k, cache_v,
                              *tab_bs, lam_b[0], g_subln_b[0], wob[0], lam_init, layer=1, idx=1, ts=ts_p)
    xs = ffn_s(xs, 1, 1)

    return (xp, xs,
            ret_p[None], jnp.transpose(k_p, (0, 3, 1, 2)), v_p.reshape(bp, sp, H_B, 2 * DH_B),
            ret_s[None], k_s.reshape(bs, ss, 2 * H_B, DH_B), v_s.reshape(bs, ss, H_B, 2 * DH_B))
```

```python
import functools
import math

import jax
import jax.numpy as jnp
from jax import lax
from jax.experimental import pallas as pl
from jax.experimental.pallas import tpu as pltpu

D_MODEL = 1024
CHUNK = 64
CHUNK_SHIFT = 6
H_A = 8
DK_A = 128
DV_A = 256
QK_A = H_A * DK_A
V_A = H_A * DV_A
H_B = 8
DH_B = 64
HB_W = 2 * DH_B
D_FF = 2816
N_MOD = 9
ROPE_THETA = 10000.0
EPS = 1e-6
HALF = 0.5
PAST_LEN = 4096
NEG = -1e30
LOG2E = 1.4426950408889634
Q_SCALE = DH_B ** -0.5 * LOG2E
VT_ROWS = HB_W + 16

LANES = 128
VMEM_LIMIT = 56 * 1024 * 1024

BF = jnp.bfloat16
F32 = jnp.float32

LOG_GAMMA = [math.log1p(-(2.0 ** (-5 - h))) for h in range(H_A)]


def _dot(a, b):
    return jnp.dot(a, b, preferred_element_type=F32)


def _dot_nt(a, b):
    return lax.dot_general(a, b, (((1,), (1,)), ((), ())), preferred_element_type=F32)


def _dot_tn(a, b):
    return lax.dot_general(a, b, (((0,), (0,)), ((), ())), preferred_element_type=F32)


def _rmsn(x):
    return x * lax.rsqrt(jnp.mean(x * x, axis=-1, keepdims=True) + EPS)


def _silu(x):
    return x * jax.nn.sigmoid(x)


def _resident(shape, index_map):
    return pl.BlockSpec(shape, index_map, pipeline_mode=pl.Buffered(1))


def _params(n_axes):
    return pltpu.CompilerParams(dimension_semantics=("arbitrary",) * n_axes,
                                vmem_limit_bytes=VMEM_LIMIT)


def _ada_kernel(c_ref, w_ref, b_ref, o_ref):
    sc = _silu(c_ref[...]).astype(BF)
    o_ref[...] = _dot(sc, w_ref[...].astype(BF)) + b_ref[...]


def _ada_proj(c_all, w, b, tn=1024):
    nl, d, n = w.shape
    r = c_all.shape[0]
    return pl.pallas_call(
        _ada_kernel,
        grid=(nl, n // tn),
        in_specs=[
            pl.BlockSpec((r, d), lambda l, j: (0, 0)),
            pl.BlockSpec((None, d, tn), lambda l, j: (l, 0, j)),
            pl.BlockSpec((None, 1, tn), lambda l, j: (l, 0, j)),
        ],
        out_specs=pl.BlockSpec((None, r, tn), lambda l, j: (l, 0, j)),
        out_shape=jax.ShapeDtypeStruct((nl, r, n), F32),
        compiler_params=_params(2),
        name="ada_proj",
    )(c_all, w, b.reshape(nl, 1, n))


N_KV_IN, N_KV_OUT = 6, 4


def _ffn_kernel(x_ref, mod_ref, g_ref, win_ref, wout_ref, *rest, m0, g0, fc, n_cast, vt_block, with_kv):
    n_in = (N_KV_IN if with_kv else 0) + n_cast
    kv_in, cast_src = rest[:n_in - n_cast], rest[n_in - n_cast:n_in]
    o_ref = rest[n_in]
    kv_out = rest[n_in + 1:n_in + 1 + (N_KV_OUT if with_kv else 0)]
    cast_dst = rest[n_in + 1 + len(kv_out):]
    for src, dst in zip(cast_src, cast_dst):
        dst[...] = src[...].astype(BF)
    nb, ts, d = x_ref.shape
    x = x_ref[...]
    shift = mod_ref[:, m0:m0 + 1, :]
    gain_in = g_ref[g0:g0 + 1, :] * (1.0 + mod_ref[:, m0 + 1:m0 + 2, :])
    gain_out = g_ref[g0 + 1:g0 + 2, :] * (HALF * mod_ref[:, m0 + 2:m0 + 3, :])
    h = _rmsn(x) * gain_in + shift
    hb = h.reshape(nb * ts, d).astype(BF)
    acc = jnp.zeros((nb * ts, d), F32)
    for c in range(D_FF // fc):
        a = _dot(hb, win_ref[:, c * fc:(c + 1) * fc])
        b = _dot(hb, win_ref[:, D_FF + c * fc:D_FF + (c + 1) * fc])
        u = (_silu(a) * b).astype(BF)
        acc = acc + _dot(u, wout_ref[c * fc:(c + 1) * fc, :])
    out = x + _rmsn(acc).reshape(nb, ts, d) * gain_out
    o_ref[...] = out
    if with_kv:
        _kv_tile(out, *kv_in, *kv_out, vt_block=vt_block)


def _ffn(x, mod, mod_row0, g, w_in, w_out, wsel, *, layer, idx, nb, ts, casts=(), kv=None):
    b, s, d = x.shape
    nbt, nst = b // nb, s // ts
    mrow = mod_row0 // nb
    xspec = pl.BlockSpec((nb, ts, d), lambda i, j: (i, j, 0))
    kv_args, kv_specs, kv_out_specs, kv_shapes, vt_block = [], [], [], [], None
    if kv is not None:
        modkv, g_kv, w_kv, tables, vt_block = kv
        tspec = pl.BlockSpec((nb * ts, LANES), lambda i, j: (j, 0))
        kv_args = [modkv, g_kv.reshape(1, d), w_kv, *tables]
        kv_specs = [pl.BlockSpec((nb, 2, d), lambda i, j: (mrow + i, 0, 0)),
                    pl.BlockSpec((1, d), lambda i, j: (0, 0)),
                    _resident((d, 2 * d), lambda i, j: (0, 0)),
                    tspec, tspec, tspec]
        if vt_block is None:
            vb_spec, vb_shape = xspec, jax.ShapeDtypeStruct(x.shape, BF)
            k_spec, k_shape = xspec, jax.ShapeDtypeStruct(x.shape, F32)
            v_spec, v_shape = xspec, jax.ShapeDtypeStruct(x.shape, F32)
        else:
            assert nb == 1
            vb_spec = pl.BlockSpec((1, ts // vt_block, H_B, VT_ROWS, vt_block), lambda i, j: (i, j, 0, 0, 0))
            vb_shape = jax.ShapeDtypeStruct((b, s // vt_block, H_B, VT_ROWS, vt_block), BF)
            k_spec = pl.BlockSpec((1, 2 * H_B, DH_B, ts), lambda i, j: (i, 0, 0, j))
            k_shape = jax.ShapeDtypeStruct((b, 2 * H_B, DH_B, s), F32)
            v_spec = pl.BlockSpec((1, ts * H_B, HB_W), lambda i, j: (i, j, 0))
            v_shape = jax.ShapeDtypeStruct((b, s * H_B, HB_W), F32)
        kv_out_specs = [k_spec, v_spec, xspec, vb_spec]
        kv_shapes = [k_shape, v_shape, jax.ShapeDtypeStruct(x.shape, BF), vb_shape]
    cast_args, cast_specs, cast_out_specs, cast_shapes = [], [], [], []
    for w, row0, rows, step in casts:
        assert row0 % step == 0 and rows % step == 0 and rows // step <= nbt * nst
        first, last = row0 // step, rows // step - 1
        cast_args.append(w)
        cast_specs.append(pl.BlockSpec((step, w.shape[1]),
                                       lambda i, j, first=first, last=last: (first + jnp.minimum(i * nst + j, last), 0)))
        cast_out_specs.append(pl.BlockSpec((step, w.shape[1]),
                                           lambda i, j, last=last: (jnp.minimum(i * nst + j, last), 0)))
        cast_shapes.append(jax.ShapeDtypeStruct((rows, w.shape[1]), BF))
    kern = functools.partial(_ffn_kernel, m0=6 * idx, g0=4 * idx, fc=256, n_cast=len(casts),
                             vt_block=vt_block, with_kv=kv is not None)
    outs = pl.pallas_call(
        kern,
        grid=(nbt, nst),
        in_specs=[
            xspec,
            pl.BlockSpec((None, nb, N_MOD, d), lambda i, j: (layer, mrow + i, 0, 0)),
            pl.BlockSpec((None, 6, d), lambda i, j: (layer, 0, 0)),
            _resident((None, d, 2 * D_FF), lambda i, j: (wsel, 0, 0)),
            _resident((None, D_FF, d), lambda i, j: (wsel, 0, 0)),
        ] + kv_specs + cast_specs,
        out_specs=[xspec] + kv_out_specs + cast_out_specs,
        out_shape=[jax.ShapeDtypeStruct(x.shape, F32)] + kv_shapes + cast_shapes,
        compiler_params=_params(2),
        name="ffn_kv" if kv is not None else "ffn",
    )(x, mod, g, w_in, w_out, *kv_args, *cast_args)
    return outs if (casts or kv is not None) else outs[0]


def _decay_mask(lg, t):
    i = lax.broadcasted_iota(jnp.int32, (t, t), 0)
    j = lax.broadcasted_iota(jnp.int32, (t, t), 1)
    dist = jnp.abs(i - j).astype(F32)
    vis = (j >> CHUNK_SHIFT) <= (i >> CHUNK_SHIFT)
    return jnp.where(vis, jnp.exp(lg * dist), 0.0)


def _ret_kernel(*refs, nb, t, from_input_state):
    if from_input_state:
        (x_ref, mod_ref, g_ref, win_ref, ggn_ref, wout_ref, cos_ref, sin_ref, sin_st_ref,
         o_ref, so_ref, q_scr, k_scr, v_scr, og_scr, gat_scr, mask_scr, st_scr) = refs
    else:
        (x_ref, mod_ref, g_ref, win_ref, ggn_ref, wout_ref, cos_ref, sin_ref,
         o_ref, so_ref, q_scr, k_scr, v_scr, og_scr, gat_scr, mask_scr, st_scr) = refs
        sin_st_ref = None
    d = D_MODEL
    r = nb * t
    j = pl.program_id(1)

    @pl.when(j == 0)
    def _():
        for hd in range(H_A):
            mask_scr[hd] = _decay_mask(LOG_GAMMA[hd], t)
        if not from_input_state:
            st_scr[...] = jnp.zeros_like(st_scr)

    x = x_ref[...]
    shift = mod_ref[:, 3:4, :]
    scale = mod_ref[:, 4:5, :]
    gate = mod_ref[:, 5:6, :]
    h = _rmsn(x) * g_ref[2:3, :] * (1.0 + scale) + shift
    hb = h.reshape(r, d).astype(BF)

    cos = cos_ref[...]
    sin = sin_ref[...]
    qk = _dot(hb, win_ref[:, 0:2 * QK_A])
    for c in range(H_A):
        qc = qk[:, c * DK_A:(c + 1) * DK_A]
        q_scr[:, c * DK_A:(c + 1) * DK_A] = qc * cos + pltpu.roll(qc, DK_A // 2, 1) * sin
        kc = qk[:, QK_A + c * DK_A:QK_A + (c + 1) * DK_A]
        k_scr[:, c * DK_A:(c + 1) * DK_A] = (kc * cos + pltpu.roll(kc, DK_A // 2, 1) * sin) * (DK_A ** -0.5)
    v_scr[...] = _dot(hb, win_ref[:, 2 * QK_A:2 * QK_A + V_A]).astype(BF)
    gp = _dot(hb, win_ref[:, 2 * QK_A + V_A:2 * QK_A + 2 * V_A])

    row = lax.broadcasted_iota(jnp.int32, (t, DK_A), 0).astype(F32)
    for hd in range(H_A):
        lg = LOG_GAMMA[hd]
        qdec = jnp.exp(lg * row)
        kdec = jnp.exp(lg * (float(t) - row))
        cdec = math.exp(lg * t)
        mask = mask_scr[hd]

        def one_batch(n, carry, hd=hd, qdec=qdec, kdec=kdec, cdec=cdec, mask=mask):
            rows = pl.ds(n * t, t)
            q = q_scr[rows, hd * DK_A:(hd + 1) * DK_A]
            k = k_scr[rows, hd * DK_A:(hd + 1) * DK_A]
            v = v_scr[rows, hd * DV_A:(hd + 1) * DV_A]
            if from_input_state:
                st = sin_st_ref[n, hd]
            else:
                st = st_scr[hd]
            s = _dot_nt(q.astype(BF), k.astype(BF)) * mask
            o = _dot(s.astype(BF), v) + _dot((q * qdec).astype(BF), st.astype(BF))
            st_new = st * cdec + _dot_tn((k * kdec).astype(BF), v)
            og_scr[rows, hd * DV_A:(hd + 1) * DV_A] = o
            if from_input_state:
                so_ref[n, hd] = st_new
            else:
                st_scr[hd] = st_new
            return carry

        for n in range(nb):
            one_batch(n, 0)

    for hd in range(H_A):
        cols = slice(hd * DV_A, (hd + 1) * DV_A)
        on = _rmsn(og_scr[:, cols]) * ggn_ref[:, cols]
        gat_scr[:, cols] = (_silu(gp[:, cols]) * on).astype(BF)
    y = _dot(gat_scr[...], wout_ref[...])
    yn = _rmsn(y) * g_ref[3:4, :]
    o_ref[...] = x + gate * yn.reshape(nb, t, d)

    if not from_input_state:
        @pl.when(j == pl.num_programs(1) - 1)
        def _():
            so_ref[0] = st_scr[...]


def _ret_mixer(x, mod, mod_row0, g, w_in, g_gn, w_out, cos_t, sin_t, state_in, *, nb, t):
    b, s, d = x.shape
    nbt, nst = b // nb, s // t
    r = nb * t
    mrow = mod_row0 // nb
    from_input = state_in is not None
    kern = functools.partial(_ret_kernel, nb=nb, t=t, from_input_state=from_input)
    in_specs = [
        pl.BlockSpec((nb, t, d), lambda i, j: (i, j, 0)),
        pl.BlockSpec((nb, N_MOD, d), lambda i, j: (mrow + i, 0, 0)),
        pl.BlockSpec((6, d), lambda i, j: (0, 0)),
        _resident((d, 2 * QK_A + 2 * V_A), lambda i, j: (0, 0)),
        pl.BlockSpec((1, V_A), lambda i, j: (0, 0)),
        _resident((V_A, d), lambda i, j: (0, 0)),
        pl.BlockSpec((r, DK_A), lambda i, j: (j, 0)),
        pl.BlockSpec((r, DK_A), lambda i, j: (j, 0)),
    ]
    args = [x, mod, g, w_in, g_gn.reshape(1, V_A), w_out, cos_t, sin_t]
    if from_input:
        in_specs.append(pl.BlockSpec((nb, H_A, DK_A, DV_A), lambda i, j: (i, 0, 0, 0)))
        args.append(state_in)
    out, st = pl.pallas_call(
        kern,
        grid=(nbt, nst),
        in_specs=in_specs,
        out_specs=[
            pl.BlockSpec((nb, t, d), lambda i, j: (i, j, 0)),
            pl.BlockSpec((nb, H_A, DK_A, DV_A), lambda i, j: (i, 0, 0, 0)),
        ],
        out_shape=[
            jax.ShapeDtypeStruct(x.shape, F32),
            jax.ShapeDtypeStruct((b, H_A, DK_A, DV_A), F32),
        ],
        scratch_shapes=[
            pltpu.VMEM((r, QK_A), F32),
            pltpu.VMEM((r, QK_A), F32),
            pltpu.VMEM((r, V_A), BF),
            pltpu.VMEM((r, V_A), F32),
            pltpu.VMEM((r, V_A), BF),
            pltpu.VMEM((H_A, t, t), F32),
            pltpu.VMEM((H_A, DK_A, DV_A), F32),
        ],
        compiler_params=_params(2),
        name="ret_mixer",
    )(*args)
    return out, st


def _rope64(blk, cos, sa, sb):
    return blk * cos + pltpu.roll(blk, DH_B // 2, 1) * sa + pltpu.roll(blk, LANES - DH_B // 2, 1) * sb


def _kv_tile(x, mod_ref, g_ref, w_ref, cos_ref, sa_ref, sb_ref, k_ref, v_ref, kb_ref, vb_ref, *, vt_block):
    nb, ts, d = x.shape
    shift = mod_ref[:, 0:1, :]
    scale = mod_ref[:, 1:2, :]
    h = _rmsn(x) * g_ref[...] * (1.0 + scale) + shift
    hb = h.reshape(nb * ts, d).astype(BF)
    kv = _dot(hb, w_ref[...])
    cos = cos_ref[...]
    sa = sa_ref[...]
    sb = sb_ref[...]
    ks = [_rope64(kv[:, c * LANES:(c + 1) * LANES], cos, sa, sb) for c in range(d // LANES)]
    k2 = jnp.concatenate(ks, axis=1)
    k = k2.reshape(nb, ts, d)
    v = kv[:, d:2 * d].reshape(nb, ts, d)
    kb_ref[...] = k.astype(BF)
    if vt_block is None:
        k_ref[...] = k
        v_ref[...] = v
        vb_ref[...] = v.astype(BF)
    else:
        k_ref[0] = k2.T.reshape(2 * H_B, DH_B, ts)
        for hd in range(H_B):
            v_ref[0, pl.ds(hd, ts, stride=H_B), :] = kv[:, d + hd * HB_W:d + (hd + 1) * HB_W]
        vt = kv[:, d:2 * d].T
        ones = jnp.ones((VT_ROWS - HB_W, vt_block), BF)
        for c in range(ts // vt_block):
            for hd in range(H_B):
                vb_ref[0, c, hd, 0:HB_W, :] = vt[hd * HB_W:(hd + 1) * HB_W,
                                                 c * vt_block:(c + 1) * vt_block].astype(BF)
                vb_ref[0, c, hd, HB_W:VT_ROWS, :] = ones


def _lambda(lam_ref, lam_init):
    lp = lam_ref[...]
    l1 = jnp.sum(lp[0:1, :] * lp[1:2, :], axis=-1, keepdims=True)
    l2 = jnp.sum(lp[2:3, :] * lp[3:4, :], axis=-1, keepdims=True)
    return jnp.exp(l1) - jnp.exp(l2) + lam_init


def _project_q(hb, wq_ref, cos, sa, sb, q_scr, tq):
    q = _dot(hb, wq_ref[...])
    lane = lax.broadcasted_iota(jnp.int32, (tq, LANES), 1)
    first = lane < DH_B
    for c in range(H_B):
        rq = _rope64(q[:, c * LANES:(c + 1) * LANES], cos, sa, sb) * Q_SCALE
        q_scr[0:tq, c * LANES:(c + 1) * LANES] = jnp.where(first, rq, 0.0).astype(BF)
        q_scr[tq:2 * tq, c * LANES:(c + 1) * LANES] = jnp.where(first, 0.0, rq).astype(BF)


def _finish_head(acc, l, lam, gsub, lam_init, tq):
    o = acc[0:tq] / l[0:tq] - lam * (acc[tq:2 * tq] / l[tq:2 * tq])
    return _rmsn(o) * gsub * (1.0 - lam_init)


def _roll_rows(x, shift):
    n = x.shape[0]
    return jnp.concatenate([x[n - shift:], x[:n - shift]], axis=0)


def _diffp_kernel(x_ref, mod_ref, g_ref, wqt_ref, k_ref, vt_ref, cos_ref, sa_ref, sb_ref, lam_ref,
                  gsub_ref, wout_ref, o_ref, qt_scr, m_scr, acc_scr, oh_scr, s_scr, e_scr, b_scr, *, tq, lam_init):
    qi = pl.program_id(1)

    @pl.when(qi == 0)
    def _():
        kr = lax.broadcasted_iota(jnp.int32, (tq, HB_W), 0)
        kc = lax.broadcasted_iota(jnp.int32, (tq, HB_W), 1)
        e_scr[...] = jnp.where((kr >> CHUNK_SHIFT) == kc, 1.0, 0.0).astype(BF)
        ra = lax.broadcasted_iota(jnp.int32, (HB_W, 2 * tq), 0)
        ci = lax.broadcasted_iota(jnp.int32, (HB_W, 2 * tq), 1)
        qchunk = jnp.where(ci >= tq, ci - tq, ci) >> CHUNK_SHIFT
        hidden = jnp.where(ra < tq // CHUNK, jnp.where(ra > qchunk, NEG, 0.0), 0.0)
        b_scr[...] = hidden.astype(BF)

    x = x_ref[0]
    shift = mod_ref[0, 3:4, :]
    scale = mod_ref[0, 4:5, :]
    gate = mod_ref[0, 5:6, :]
    h = _rmsn(x) * g_ref[2:3, :] * (1.0 + scale) + shift
    ht = h.T.astype(BF)
    cos = cos_ref[...]
    sa = sa_ref[...]
    sb = sb_ref[...]
    first = lax.broadcasted_iota(jnp.int32, (HB_W, tq), 0) < DH_B
    half = H_B // 2
    for hd in range(H_B):
        if hd % half == 0:
            qt4 = _dot(wqt_ref[hd * HB_W:(hd + half) * HB_W, :], ht)
        blk = qt4[(hd % half) * HB_W:(hd % half + 1) * HB_W, :]
        rq = (blk * cos + _roll_rows(blk, DH_B // 2) * sa + _roll_rows(blk, HB_W - DH_B // 2) * sb) * Q_SCALE
        qt_scr[hd] = jnp.concatenate([jnp.where(first, rq, 0.0), jnp.where(first, 0.0, rq)], axis=1).astype(BF)

    def block(kb, diagonal):
        rows = pl.ds(pl.multiple_of(kb * tq, tq), tq)
        for hd in range(H_B):
            cols = slice(hd * HB_W, (hd + 1) * HB_W)
            if diagonal:
                k_aug = jnp.concatenate([k_ref[0, rows, cols], e_scr[...]], axis=1)
                q_aug = jnp.concatenate([qt_scr[hd], b_scr[...]], axis=0)
                s_scr[hd] = _dot(k_aug, q_aug)
            else:
                s_scr[hd] = _dot(k_ref[0, rows, cols], qt_scr[hd])
        for hd in range(H_B):
            s = s_scr[hd]
            if diagonal:
                m_new = jnp.max(s, axis=0, keepdims=True)
                acc_scr[hd] = _dot(vt_ref[0, kb, hd], jnp.exp2(s - m_new).astype(BF))
            else:
                m_prev = m_scr[hd]
                m_new = jnp.maximum(m_prev, jnp.max(s, axis=0, keepdims=True))
                p = jnp.exp2(s - m_new).astype(BF)
                acc_scr[hd] = jnp.exp2(m_prev - m_new) * acc_scr[hd] + _dot(vt_ref[0, kb, hd], p)
            m_scr[hd] = m_new

    block(qi, True)

    def body(kb, carry):
        block(kb, False)
        return carry

    lax.fori_loop(0, qi, body, 0)

    lam = _lambda(lam_ref, lam_init)
    for hd in range(H_B):
        acc = acc_scr[hd, 0:HB_W, :]
        inv_l = 1.0 / acc_scr[hd, HB_W:HB_W + 1, :]
        o = acc[:, 0:tq] * inv_l[:, 0:tq] - lam * (acc[:, tq:2 * tq] * inv_l[:, tq:2 * tq])
        on = o * lax.rsqrt(jnp.mean(o * o, axis=0, keepdims=True) + EPS)
        oh_scr[hd * HB_W:(hd + 1) * HB_W, :] = (on * gsub_ref[...] * (1.0 - lam_init)).astype(BF)

    y = _dot_tn(oh_scr[...], wout_ref[...])
    o_ref[0] = x + gate * (_rmsn(y) * g_ref[3:4, :])


def _diff_prompt(x, mod, mod_row0, g, w_qt, kb, vtb, cos_t, sa_t, sb_t, lam_p, g_sub, w_out, lam_init, *, tq):
    b, s, d = x.shape
    kern = functools.partial(_diffp_kernel, tq=tq, lam_init=lam_init)
    tspec = pl.BlockSpec((HB_W, tq), lambda i, j: (0, j))
    return pl.pallas_call(
        kern,
        grid=(b, s // tq),
        in_specs=[
            pl.BlockSpec((1, tq, d), lambda i, j: (i, j, 0)),
            pl.BlockSpec((1, N_MOD, d), lambda i, j: (mod_row0 + i, 0, 0)),
            pl.BlockSpec((6, d), lambda i, j: (0, 0)),
            _resident((d, d), lambda i, j: (0, 0)),
            pl.BlockSpec((1, s, d), lambda i, j: (i, 0, 0)),
            pl.BlockSpec((1, s // tq, H_B, VT_ROWS, tq), lambda i, j: (i, 0, 0, 0, 0)),
            tspec, tspec, tspec,
            pl.BlockSpec((4, DH_B), lambda i, j: (0, 0)),
            pl.BlockSpec((HB_W, 1), lambda i, j: (0, 0)),
            _resident((d, d), lambda i, j: (0, 0)),
        ],
        out_specs=pl.BlockSpec((1, tq, d), lambda i, j: (i, j, 0)),
        out_shape=jax.ShapeDtypeStruct(x.shape, F32),
        scratch_shapes=[
            pltpu.VMEM((H_B, HB_W, 2 * tq), BF),
            pltpu.VMEM((H_B, 1, 2 * tq), F32),
            pltpu.VMEM((H_B, VT_ROWS, 2 * tq), F32),
            pltpu.VMEM((d, tq), BF),
            pltpu.VMEM((H_B, tq, 2 * tq), F32),
            pltpu.VMEM((tq, HB_W), BF),
            pltpu.VMEM((HB_W, 2 * tq), BF),
        ],
        compiler_params=_params(2),
        name="diff_prompt",
    )(x, mod, g, w_qt, kb, vtb, cos_t, sa_t, sb_t, lam_p, g_sub.reshape(HB_W, 1), w_out)


def _diffs_step(kc, n_kc, x_ref, mod_ref, g_ref, wq_ref, kn_ref, vn_ref, kc_ref, vc_ref, cos_ref, sa_ref, sb_ref,
                lam_ref, gsub_ref, wout_ref, o_ref, q_scr, m_scr, acc_scr, oh_scr, s_scr, *, tq, lam_init):
    rows2 = 2 * tq

    def with_ones(v):
        return jnp.concatenate([v, jnp.ones(v.shape, BF)], axis=1)

    @pl.when(kc == 0)
    def _():
        x = x_ref[0]
        h = _rmsn(x) * g_ref[2:3, :] * (1.0 + mod_ref[0, 4:5, :]) + mod_ref[0, 3:4, :]
        _project_q(h.astype(BF), wq_ref, cos_ref[...], sa_ref[...], sb_ref[...], q_scr, tq)
        for hd in range(H_B):
            cols = slice(hd * HB_W, (hd + 1) * HB_W)
            s = _dot_nt(q_scr[:, cols], kn_ref[0, :, cols])
            m0 = jnp.max(s, axis=-1, keepdims=True)
            m_scr[hd] = jnp.broadcast_to(m0, (rows2, LANES))
            acc_scr[hd] = _dot(jnp.exp2(s - m0).astype(BF), with_ones(vn_ref[0, :, cols]))

    tkc = kc_ref.shape[3]
    for hd in range(H_B):
        s_scr[hd] = jnp.concatenate(
            [_dot(q_scr[t * tq:(t + 1) * tq, (2 * hd + t) * DH_B:(2 * hd + t + 1) * DH_B],
                  kc_ref[0, 2 * hd + t].astype(BF)) for t in range(2)], axis=0)
    for hd in range(H_B):
        s = s_scr[hd]
        v_h = with_ones(vc_ref[0, pl.ds(hd, tkc, stride=H_B), :].astype(BF))
        m_prev = m_scr[hd][:, 0:1]
        m_new = jnp.maximum(m_prev, jnp.max(s, axis=-1, keepdims=True))
        acc_scr[hd] = jnp.exp2(m_prev - m_new) * acc_scr[hd] + _dot(jnp.exp2(s - m_new).astype(BF), v_h)
        m_scr[hd] = jnp.broadcast_to(m_new, (rows2, LANES))

    @pl.when(kc == n_kc - 1)
    def _():
        lam = _lambda(lam_ref, lam_init)
        for hd in range(H_B):
            cols = slice(hd * HB_W, (hd + 1) * HB_W)
            oh_scr[:, cols] = _finish_head(acc_scr[hd, :, 0:HB_W], acc_scr[hd, :, HB_W:HB_W + 1], lam,
                                           gsub_ref[...], lam_init, tq).astype(BF)
        y = _dot(oh_scr[...], wout_ref[...])
        o_ref[0] = x_ref[0] + mod_ref[0, 5:6, :] * (_rmsn(y) * g_ref[3:4, :])


N_DIFFS_IN = 14
FFN_PHASES = 4


def _ffn_diffs_kernel(xp_ref, modp_ref, gp_ref, win_ref, wout_ref, *rest, m0, g0, fc, tq, lam_init):
    d_in = rest[:N_DIFFS_IN]
    op_ref, os_ref = rest[N_DIFFS_IN:N_DIFFS_IN + 2]
    h_scr, accf_scr = rest[N_DIFFS_IN + 2:N_DIFFS_IN + 4]
    d_scr = rest[N_DIFFS_IN + 4:]
    d = xp_ref.shape[-1]
    phase = lax.rem(pl.program_id(0), FFN_PHASES)
    n_chunks = D_FF // fc
    per = -(-n_chunks // FFN_PHASES)

    for p in range(FFN_PHASES):
        @pl.when(phase == p)
        def _(p=p):
            if p == 0:
                gain_in = gp_ref[g0:g0 + 1, :] * (1.0 + modp_ref[:, m0 + 1:m0 + 2, :])
                h = _rmsn(xp_ref[...]) * gain_in + modp_ref[:, m0:m0 + 1, :]
                h_scr[...] = h.reshape(-1, d).astype(BF)
            hb = h_scr[...]
            acc = None
            for c in range(p * per, min((p + 1) * per, n_chunks)):
                a = _dot(hb, win_ref[:, c * fc:(c + 1) * fc])
                b = _dot(hb, win_ref[:, D_FF + c * fc:D_FF + (c + 1) * fc])
                part = _dot((_silu(a) * b).astype(BF), wout_ref[c * fc:(c + 1) * fc, :])
                acc = part if acc is None else acc + part
            if p > 0:
                acc = accf_scr[...] + acc
            if p < FFN_PHASES - 1:
                accf_scr[...] = acc
            else:
                x = xp_ref[...]
                gain_out = gp_ref[g0 + 1:g0 + 2, :] * (HALF * modp_ref[:, m0 + 2:m0 + 3, :])
                op_ref[...] = x + _rmsn(acc).reshape(x.shape) * gain_out

    _diffs_step(phase, FFN_PHASES, *d_in, os_ref, *d_scr, tq=tq, lam_init=lam_init)


def _ffn_diff_sample(xp, mod, row_p, g, w_in, w_out, wsel, x, mod1, row_s, g1, w_q, kn, vn, cache_k, cache_v,
                     cos_t, sa_t, sb_t, lam_p, g_sub, w_out_b, lam_init, *, layer, idx, ts):
    bp, sp, d = xp.shape
    b, tq, _ = x.shape
    past = cache_k.shape[1]
    tkc = past // FFN_PHASES
    nst = sp // ts
    assert bp * nst == b
    cache_kt = jnp.transpose(cache_k, (0, 2, 3, 1))
    cache_vf = cache_v.reshape(b, past * H_B, HB_W)

    def tile(t):
        return t // FFN_PHASES

    xp_spec = pl.BlockSpec((1, ts, d), lambda t: (tile(t) // nst, tile(t) % nst, 0))
    tspec = pl.BlockSpec((tq, LANES), lambda t: (0, 0))
    xspec = pl.BlockSpec((1, tq, d), lambda t: (tile(t), 0, 0))
    kspec = pl.BlockSpec((1, 2 * H_B, DH_B, tkc), lambda t: (tile(t), 0, 0, t % FFN_PHASES))
    vspec = pl.BlockSpec((1, tkc * H_B, HB_W), lambda t: (tile(t), t % FFN_PHASES, 0))
    kern = functools.partial(_ffn_diffs_kernel, m0=6 * idx, g0=4 * idx, fc=256, tq=tq, lam_init=lam_init)
    return pl.pallas_call(
        kern,
        grid=(b * FFN_PHASES,),
        in_specs=[
            xp_spec,
            pl.BlockSpec((None, 1, N_MOD, d), lambda t: (layer, row_p + tile(t) // nst, 0, 0)),
            pl.BlockSpec((None, 6, d), lambda t: (layer, 0, 0)),
            _resident((None, d, 2 * D_FF), lambda t: (wsel, 0, 0)),
            _resident((None, D_FF, d), lambda t: (wsel, 0, 0)),
            xspec,
            pl.BlockSpec((1, N_MOD, d), lambda t: (row_s + tile(t), 0, 0)),
            pl.BlockSpec((6, d), lambda t: (0, 0)),
            _resident((d, d), lambda t: (0, 0)),
            xspec, xspec, kspec, vspec,
            tspec, tspec, tspec,
            pl.BlockSpec((4, DH_B), lambda t: (0, 0)),
            pl.BlockSpec((1, HB_W), lambda t: (0, 0)),
            _resident((d, d), lambda t: (0, 0)),
        ],
        out_specs=[xp_spec, xspec],
        out_shape=[jax.ShapeDtypeStruct(xp.shape, F32), jax.ShapeDtypeStruct(x.shape, F32)],
        scratch_shapes=[
            pltpu.VMEM((ts, d), BF),
            pltpu.VMEM((ts, d), F32),
            pltpu.VMEM((2 * tq, d), BF),
            pltpu.VMEM((H_B, 2 * tq, LANES), F32),
            pltpu.VMEM((H_B, 2 * tq, 2 * HB_W), F32),
            pltpu.VMEM((tq, d), BF),
            pltpu.VMEM((H_B, 2 * tq, tkc), F32),
        ],
        compiler_params=_params(1),
        name="ffn_diff_sample",
    )(xp, mod, g, w_in, w_out, x, mod1, g1, w_q, kn, vn, cache_kt, cache_vf, cos_t, sa_t, sb_t, lam_p,
      g_sub.reshape(1, HB_W), w_out_b)


def _rope_tables_a(pos):
    inv = jnp.power(ROPE_THETA, -jnp.arange(0, DK_A, 2, dtype=jnp.float32) / DK_A)
    ang = pos[:, None] * inv[None, :]
    cos, sin = jnp.cos(ang), jnp.sin(ang)
    return jnp.concatenate([cos, cos], axis=-1), jnp.concatenate([-sin, sin], axis=-1)


def _rope_tables_b(pos):
    inv = jnp.power(ROPE_THETA, -jnp.arange(0, DH_B, 2, dtype=jnp.float32) / DH_B)
    ang = pos[:, None] * inv[None, :]
    cos, sin = jnp.cos(ang), jnp.sin(ang)
    zero = jnp.zeros_like(sin)
    cos_t = jnp.concatenate([cos, cos, cos, cos], axis=-1)
    sa = jnp.concatenate([zero, sin, zero, sin], axis=-1)
    sb = jnp.concatenate([-sin, zero, -sin, zero], axis=-1)
    return cos_t, sa, sb


def kernel(x_prompt, x_sample, state_ret, cache_k, cache_v, c_prompt, c_sample, w_ada, b_ada, g_norm, w_ffn_in,
           w_ffn_out, w_in_a, g_gn_a, w_out_a, w_ada_kv, b_ada_kv, g_kv, w_kv, w_q_b, lam_b, g_subln_b, w_out_b):
    d = D_MODEL
    bp, sp, _ = x_prompt.shape
    bs, ss, _ = x_sample.shape

    n_rows = -(-(bs + bp) // 16) * 16
    c_all = jnp.concatenate([c_sample, c_prompt, jnp.zeros((n_rows - bs - bp, d), F32)], axis=0)
    mod = _ada_proj(c_all, w_ada, b_ada).reshape(2, n_rows, N_MOD, d)
    modkv = _ada_proj(c_all, w_ada_kv[None], b_ada_kv[None]).reshape(n_rows, 2, d)
    row_s, row_p = 0, bs

    wfi0 = w_ffn_in[0, 0].astype(BF)[None]
    wfo0 = w_ffn_out[0, 0].astype(BF)[None]
    wqb = w_q_b.astype(BF)
    wob = w_out_b.astype(BF)
    ts_p = 512
    nb_s = ts_p // ss
    nb_r = 4
    t_ret = 256
    tq = 256
    n_ffn = w_ffn_in.shape[0] * w_ffn_in.shape[1]
    n_steps = bp * sp // ts_p
    per_set = n_steps // n_ffn
    ffn_casts = (
        (w_ffn_in.reshape(n_ffn * d, 2 * D_FF), d, (n_ffn - 1) * d, d // per_set),
        (w_ffn_out.reshape(n_ffn * D_FF, d), D_FF, (n_ffn - 1) * D_FF, D_FF // per_set),
        (w_in_a.reshape(d, 2 * QK_A + 2 * V_A), 0, d, d // n_steps),
        (w_out_a.reshape(V_A, d), 0, V_A, V_A // n_steps),
        (w_kv, 0, d, d // n_steps),
    )

    pos_p = jnp.arange(sp, dtype=jnp.float32)
    pos_s = PAST_LEN + jnp.arange(ss, dtype=jnp.float32)

    cos_ap, sin_ap = _rope_tables_a(pos_p)
    cos_as, sin_as = [jnp.tile(a, (nb_r, 1)) for a in _rope_tables_a(pos_s)]
    tab_bp = _rope_tables_b(pos_p)
    tab_bs = _rope_tables_b(pos_s)
    tab_bs_tiled = [jnp.tile(a, (nb_s, 1)) for a in tab_bs]

    xp, wfi_rest, wfo_rest, wia, woa, wkv = _ffn(x_prompt, mod, row_p, g_norm, wfi0, wfo0, 0, layer=0, idx=0,
                                                 nb=1, ts=ts_p, casts=ffn_casts)
    wfi_rest = wfi_rest.reshape(n_ffn - 1, d, 2 * D_FF)
    wfo_rest = wfo_rest.reshape(n_ffn - 1, D_FF, d)

    def ffn_p(x, l, i, kv=None):
        return _ffn(x, mod, row_p, g_norm, wfi_rest, wfo_rest, 2 * l + i - 1, layer=l, idx=i, nb=1, ts=ts_p, kv=kv)

    def ffn_s(x, l, i, kv=None):
        if (l, i) == (0, 0):
            return _ffn(x, mod, row_s, g_norm, wfi0, wfo0, 0, layer=l, idx=i, nb=nb_s, ts=ss)
        return _ffn(x, mod, row_s, g_norm, wfi_rest, wfo_rest, 2 * l + i - 1, layer=l, idx=i, nb=nb_s, ts=ss,
                    kv=kv)

    xs = ffn_s(x_sample, 0, 0)
    xp, ret_p = _ret_mixer(xp, mod[0], row_p, g_norm[0], wia, g_gn_a[0], woa, cos_ap, sin_ap, None,
                           nb=1, t=t_ret)
    xs, ret_s = _ret_mixer(xs, mod[0], row_s, g_norm[0], wia, g_gn_a[0], woa, cos_as, sin_as, state_ret[0],
                           nb=nb_r, t=ss)
    xp, k_p, v_p, kb_p, vtb_p = ffn_p(xp, 0, 1, kv=(modkv, g_kv, wkv, tab_bp, tq))
    xs, k_s, v_s, kb_s, vb_s = ffn_s(xs, 0, 1, kv=(modkv, g_kv, wkv, tab_bs_tiled, None))

    lam_init = 0.8 - 0.6 * math.exp(-0.3 * 1)
    xp = ffn_p(xp, 1, 0)
    xs = ffn_s(xs, 1, 0)
    xp = _diff_prompt(xp, mod[1], row_p, g_norm[1], w_q_b[0].T.astype(BF), kb_p, vtb_p,
                      *[a.T for a in tab_bp], lam_b[0], g_subln_b[0], wob[0],
                      lam_init, tq=tq)
    xp, xs = _ffn_diff_sample(xp, mod, row_p, g_norm, wfi_rest, wfo_rest, 2,
                              xs, mod[1], row_s, g_norm[1], wqb[0], kb_s, vb_s, cache_k, cache_v,
                              *tab_bs, lam_b[0], g_subln_b[0], wob[0], lam_init, layer=1, idx=1, ts=ts_p)
    xs = ffn_s(xs, 1, 1)

    return (xp, xs,
            ret_p[None], jnp.transpose(k_p, (0, 3, 1, 2)), v_p.reshape(bp, sp, H_B, 2 * DH_B),
            ret_s[None], k_s.reshape(bs, ss, 2 * H_B, DH_B), v_s.reshape(bs, ss, H_B, 2 * DH_B))
```

```python
import functools
import math

import jax
import jax.numpy as jnp
from jax import lax
from jax.experimental import pallas as pl
from jax.experimental.pallas import tpu as pltpu

D_MODEL = 1024
CHUNK = 64
CHUNK_SHIFT = 6
H_A = 8
DK_A = 128
DV_A = 256
QK_A = H_A * DK_A
V_A = H_A * DV_A
H_B = 8
DH_B = 64
HB_W = 2 * DH_B
D_FF = 2816
N_MOD = 9
ROPE_THETA = 10000.0
EPS = 1e-6
HALF = 0.5
PAST_LEN = 4096
NEG = -1e30
LOG2E = 1.4426950408889634
Q_SCALE = DH_B ** -0.5 * LOG2E
VT_ROWS = HB_W + 16

LANES = 128
VMEM_LIMIT = 56 * 1024 * 1024

BF = jnp.bfloat16
F32 = jnp.float32

LOG_GAMMA = [math.log1p(-(2.0 ** (-5 - h))) for h in range(H_A)]


def _dot(a, b):
    return jnp.dot(a, b, preferred_element_type=F32)


def _dot_nt(a, b):
    return lax.dot_general(a, b, (((1,), (1,)), ((), ())), preferred_element_type=F32)


def _dot_tn(a, b):
    return lax.dot_general(a, b, (((0,), (0,)), ((), ())), preferred_element_type=F32)


def _rmsn(x):
    return x * lax.rsqrt(jnp.mean(x * x, axis=-1, keepdims=True) + EPS)


def _silu(x):
    return x * jax.nn.sigmoid(x)


def _resident(shape, index_map):
    return pl.BlockSpec(shape, index_map, pipeline_mode=pl.Buffered(1))


def _params(n_axes):
    return pltpu.CompilerParams(dimension_semantics=("arbitrary",) * n_axes,
                                vmem_limit_bytes=VMEM_LIMIT)


def _ada_kernel(c_ref, w_ref, b_ref, o_ref):
    sc = _silu(c_ref[...]).astype(BF)
    o_ref[...] = _dot(sc, w_ref[...].astype(BF)) + b_ref[...]


def _ada_proj(c_all, w, b, tn=1024):
    nl, d, n = w.shape
    r = c_all.shape[0]
    return pl.pallas_call(
        _ada_kernel,
        grid=(nl, n // tn),
        in_specs=[
            pl.BlockSpec((r, d), lambda l, j: (0, 0)),
            pl.BlockSpec((None, d, tn), lambda l, j: (l, 0, j)),
            pl.BlockSpec((None, 1, tn), lambda l, j: (l, 0, j)),
        ],
        out_specs=pl.BlockSpec((None, r, tn), lambda l, j: (l, 0, j)),
        out_shape=jax.ShapeDtypeStruct((nl, r, n), F32),
        compiler_params=_params(2),
        name="ada_proj",
    )(c_all, w, b.reshape(nl, 1, n))


N_KV_IN, N_KV_OUT = 6, 4


def _ffn_kernel(x_ref, mod_ref, g_ref, win_ref, wout_ref, *rest, m0, g0, fc, n_cast, vt_block, with_kv):
    n_in = (N_KV_IN if with_kv else 0) + n_cast
    kv_in, cast_src = rest[:n_in - n_cast], rest[n_in - n_cast:n_in]
    o_ref = rest[n_in]
    kv_out = rest[n_in + 1:n_in + 1 + (N_KV_OUT if with_kv else 0)]
    cast_dst = rest[n_in + 1 + len(kv_out):]
    for src, dst in zip(cast_src, cast_dst):
        dst[...] = src[...].astype(BF)
    nb, ts, d = x_ref.shape
    x = x_ref[...]
    shift = mod_ref[:, m0:m0 + 1, :]
    gain_in = g_ref[g0:g0 + 1, :] * (1.0 + mod_ref[:, m0 + 1:m0 + 2, :])
    gain_out = g_ref[g0 + 1:g0 + 2, :] * (HALF * mod_ref[:, m0 + 2:m0 + 3, :])
    h = _rmsn(x) * gain_in + shift
    hb = h.reshape(nb * ts, d).astype(BF)
    acc = jnp.zeros((nb * ts, d), F32)
    for c in range(D_FF // fc):
        a = _dot(hb, win_ref[:, c * fc:(c + 1) * fc])
        b = _dot(hb, win_ref[:, D_FF + c * fc:D_FF + (c + 1) * fc])
        u = (_silu(a) * b).astype(BF)
        acc = acc + _dot(u, wout_ref[c * fc:(c + 1) * fc, :])
    out = x + _rmsn(acc).reshape(nb, ts, d) * gain_out
    o_ref[...] = out
    if with_kv:
        _kv_tile(out, *kv_in, *kv_out, vt_block=vt_block)


def _ffn(x, mod, mod_row0, g, w_in, w_out, wsel, *, layer, idx, nb, ts, casts=(), kv=None):
    b, s, d = x.shape
    nbt, nst = b // nb, s // ts
    mrow = mod_row0 // nb
    xspec = pl.BlockSpec((nb, ts, d), lambda i, j: (i, j, 0))
    kv_args, kv_specs, kv_out_specs, kv_shapes, vt_block = [], [], [], [], None
    if kv is not None:
        modkv, g_kv, w_kv, tables, vt_block = kv
        tspec = pl.BlockSpec((nb * ts, LANES), lambda i, j: (j, 0))
        kv_args = [modkv, g_kv.reshape(1, d), w_kv, *tables]
        kv_specs = [pl.BlockSpec((nb, 2, d), lambda i, j: (mrow + i, 0, 0)),
                    pl.BlockSpec((1, d), lambda i, j: (0, 0)),
                    _resident((d, 2 * d), lambda i, j: (0, 0)),
                    tspec, tspec, tspec]
        if vt_block is None:
            vb_spec, vb_shape = xspec, jax.ShapeDtypeStruct(x.shape, BF)
            k_spec, k_shape = xspec, jax.ShapeDtypeStruct(x.shape, F32)
            v_spec, v_shape = xspec, jax.ShapeDtypeStruct(x.shape, F32)
        else:
            assert nb == 1
            vb_spec = pl.BlockSpec((1, ts // vt_block, H_B, VT_ROWS, vt_block), lambda i, j: (i, j, 0, 0, 0))
            vb_shape = jax.ShapeDtypeStruct((b, s // vt_block, H_B, VT_ROWS, vt_block), BF)
            k_spec = pl.BlockSpec((1, 2 * H_B, DH_B, ts), lambda i, j: (i, 0, 0, j))
            k_shape = jax.ShapeDtypeStruct((b, 2 * H_B, DH_B, s), F32)
            v_spec = pl.BlockSpec((1, ts * H_B, HB_W), lambda i, j: (i, j, 0))
            v_shape = jax.ShapeDtypeStruct((b, s * H_B, HB_W), F32)
        kv_out_specs = [k_spec, v_spec, xspec, vb_spec]
        kv_shapes = [k_shape, v_shape, jax.ShapeDtypeStruct(x.shape, BF), vb_shape]
    cast_args, cast_specs, cast_out_specs, cast_shapes = [], [], [], []
    for w, row0, rows, step in casts:
        assert row0 % step == 0 and rows % step == 0 and rows // step <= nbt * nst
        first, last = row0 // step, rows // step - 1
        cast_args.append(w)
        cast_specs.append(pl.BlockSpec((step, w.shape[1]),
                                       lambda i, j, first=first, last=last: (first + jnp.minimum(i * nst + j, last), 0)))
        cast_out_specs.append(pl.BlockSpec((step, w.shape[1]),
                                           lambda i, j, last=last: (jnp.minimum(i * nst + j, last), 0)))
        cast_shapes.append(jax.ShapeDtypeStruct((rows, w.shape[1]), BF))
    kern = functools.partial(_ffn_kernel, m0=6 * idx, g0=4 * idx, fc=256, n_cast=len(casts),
                             vt_block=vt_block, with_kv=kv is not None)
    outs = pl.pallas_call(
        kern,
        grid=(nbt, nst),
        in_specs=[
            xspec,
            pl.BlockSpec((None, nb, N_MOD, d), lambda i, j: (layer, mrow + i, 0, 0)),
            pl.BlockSpec((None, 6, d), lambda i, j: (layer, 0, 0)),
            _resident((None, d, 2 * D_FF), lambda i, j: (wsel, 0, 0)),
            _resident((None, D_FF, d), lambda i, j: (wsel, 0, 0)),
        ] + kv_specs + cast_specs,
        out_specs=[xspec] + kv_out_specs + cast_out_specs,
        out_shape=[jax.ShapeDtypeStruct(x.shape, F32)] + kv_shapes + cast_shapes,
        compiler_params=_params(2),
        name="ffn_kv" if kv is not None else "ffn",
    )(x, mod, g, w_in, w_out, *kv_args, *cast_args)
    return outs if (casts or kv is not None) else outs[0]


def _decay_mask(lg, t, nb=1):
    i = lax.broadcasted_iota(jnp.int32, (nb * t, nb * t), 0)
    j = lax.broadcasted_iota(jnp.int32, (nb * t, nb * t), 1)
    dist = jnp.abs(i - j).astype(F32)
    if nb == 1:
        vis = (j >> CHUNK_SHIFT) <= (i >> CHUNK_SHIFT)
    else:
        assert t <= CHUNK and t & (t - 1) == 0
        sh = t.bit_length() - 1
        vis = (i >> sh) == (j >> sh)
    return jnp.where(vis, jnp.exp(lg * dist), 0.0)


def _ret_kernel(*refs, nb, t, from_input_state):
    if from_input_state:
        (x_ref, mod_ref, g_ref, win_ref, ggn_ref, wout_ref, cos_ref, sin_ref, sin_st_ref,
         o_ref, so_ref, q_scr, k_scr, v_scr, og_scr, gat_scr, mask_scr, st_scr) = refs
    else:
        (x_ref, mod_ref, g_ref, win_ref, ggn_ref, wout_ref, cos_ref, sin_ref,
         o_ref, so_ref, q_scr, k_scr, v_scr, og_scr, gat_scr, mask_scr, st_scr) = refs
        sin_st_ref = None
    d = D_MODEL
    r = nb * t
    j = pl.program_id(1)

    @pl.when(j == 0)
    def _():
        for hd in range(H_A):
            mask_scr[hd] = _decay_mask(LOG_GAMMA[hd], t, nb)
        if not from_input_state:
            st_scr[...] = jnp.zeros_like(st_scr)

    x = x_ref[...]
    shift = mod_ref[:, 3:4, :]
    scale = mod_ref[:, 4:5, :]
    gate = mod_ref[:, 5:6, :]
    h = _rmsn(x) * g_ref[2:3, :] * (1.0 + scale) + shift
    hb = h.reshape(r, d).astype(BF)

    cos = cos_ref[...]
    sin = sin_ref[...]
    qk = _dot(hb, win_ref[:, 0:2 * QK_A])
    for c in range(H_A):
        qc = qk[:, c * DK_A:(c + 1) * DK_A]
        q_scr[:, c * DK_A:(c + 1) * DK_A] = qc * cos + pltpu.roll(qc, DK_A // 2, 1) * sin
        kc = qk[:, QK_A + c * DK_A:QK_A + (c + 1) * DK_A]
        k_scr[:, c * DK_A:(c + 1) * DK_A] = (kc * cos + pltpu.roll(kc, DK_A // 2, 1) * sin) * (DK_A ** -0.5)
    v_scr[...] = _dot(hb, win_ref[:, 2 * QK_A:2 * QK_A + V_A]).astype(BF)
    gp = _dot(hb, win_ref[:, 2 * QK_A + V_A:2 * QK_A + 2 * V_A])

    row = lax.broadcasted_iota(jnp.int32, (t, DK_A), 0).astype(F32)
    for hd in range(H_A):
        lg = LOG_GAMMA[hd]
        qdec = jnp.exp(lg * row)
        kdec = jnp.exp(lg * (float(t) - row))
        cdec = math.exp(lg * t)
        mask = mask_scr[hd]

        qa = q_scr[:, hd * DK_A:(hd + 1) * DK_A]
        ka = k_scr[:, hd * DK_A:(hd + 1) * DK_A]
        va = v_scr[:, hd * DV_A:(hd + 1) * DV_A]
        s = _dot_nt(qa.astype(BF), ka.astype(BF)) * mask
        o_in = _dot(s.astype(BF), va)
        for n in range(nb):
            rows = slice(n * t, (n + 1) * t)
            st = sin_st_ref[n, hd] if from_input_state else st_scr[hd]
            o = o_in[rows] + _dot((qa[rows] * qdec).astype(BF), st.astype(BF))
            st_new = st * cdec + _dot_tn((ka[rows] * kdec).astype(BF), va[rows])
            og_scr[rows, hd * DV_A:(hd + 1) * DV_A] = o
            if from_input_state:
                so_ref[n, hd] = st_new
            else:
                st_scr[hd] = st_new

    for hd in range(H_A):
        cols = slice(hd * DV_A, (hd + 1) * DV_A)
        on = _rmsn(og_scr[:, cols]) * ggn_ref[:, cols]
        gat_scr[:, cols] = (_silu(gp[:, cols]) * on).astype(BF)
    y = _dot(gat_scr[...], wout_ref[...])
    yn = _rmsn(y) * g_ref[3:4, :]
    o_ref[...] = x + gate * yn.reshape(nb, t, d)

    if not from_input_state:
        @pl.when(j == pl.num_programs(1) - 1)
        def _():
            so_ref[0] = st_scr[...]


def _ret_mixer(x, mod, mod_row0, g, w_in, g_gn, w_out, cos_t, sin_t, state_in, *, nb, t):
    b, s, d = x.shape
    nbt, nst = b // nb, s // t
    r = nb * t
    mrow = mod_row0 // nb
    from_input = state_in is not None
    kern = functools.partial(_ret_kernel, nb=nb, t=t, from_input_state=from_input)
    in_specs = [
        pl.BlockSpec((nb, t, d), lambda i, j: (i, j, 0)),
        pl.BlockSpec((nb, N_MOD, d), lambda i, j: (mrow + i, 0, 0)),
        pl.BlockSpec((6, d), lambda i, j: (0, 0)),
        _resident((d, 2 * QK_A + 2 * V_A), lambda i, j: (0, 0)),
        pl.BlockSpec((1, V_A), lambda i, j: (0, 0)),
        _resident((V_A, d), lambda i, j: (0, 0)),
        pl.BlockSpec((r, DK_A), lambda i, j: (j, 0)),
        pl.BlockSpec((r, DK_A), lambda i, j: (j, 0)),
    ]
    args = [x, mod, g, w_in, g_gn.reshape(1, V_A), w_out, cos_t, sin_t]
    if from_input:
        in_specs.append(pl.BlockSpec((nb, H_A, DK_A, DV_A), lambda i, j: (i, 0, 0, 0)))
        args.append(state_in)
    out, st = pl.pallas_call(
        kern,
        grid=(nbt, nst),
        in_specs=in_specs,
        out_specs=[
            pl.BlockSpec((nb, t, d), lambda i, j: (i, j, 0)),
            pl.BlockSpec((nb, H_A, DK_A, DV_A), lambda i, j: (i, 0, 0, 0)),
        ],
        out_shape=[
            jax.ShapeDtypeStruct(x.shape, F32),
            jax.ShapeDtypeStruct((b, H_A, DK_A, DV_A), F32),
        ],
        scratch_shapes=[
            pltpu.VMEM((r, QK_A), F32),
            pltpu.VMEM((r, QK_A), F32),
            pltpu.VMEM((r, V_A), BF),
            pltpu.VMEM((r, V_A), F32),
            pltpu.VMEM((r, V_A), BF),
            pltpu.VMEM((H_A, r, r), F32),
            pltpu.VMEM((H_A, DK_A, DV_A), F32),
        ],
        compiler_params=_params(2),
        name="ret_mixer",
    )(*args)
    return out, st


def _rope64(blk, cos, sa, sb):
    return blk * cos + pltpu.roll(blk, DH_B // 2, 1) * sa + pltpu.roll(blk, LANES - DH_B // 2, 1) * sb


def _kv_tile(x, mod_ref, g_ref, w_ref, cos_ref, sa_ref, sb_ref, k_ref, v_ref, kb_ref, vb_ref, *, vt_block):
    nb, ts, d = x.shape
    shift = mod_ref[:, 0:1, :]
    scale = mod_ref[:, 1:2, :]
    h = _rmsn(x) * g_ref[...] * (1.0 + scale) + shift
    hb = h.reshape(nb * ts, d).astype(BF)
    kv = _dot(hb, w_ref[...])
    cos = cos_ref[...]
    sa = sa_ref[...]
    sb = sb_ref[...]
    ks = [_rope64(kv[:, c * LANES:(c + 1) * LANES], cos, sa, sb) for c in range(d // LANES)]
    k2 = jnp.concatenate(ks, axis=1)
    k = k2.reshape(nb, ts, d)
    v = kv[:, d:2 * d].reshape(nb, ts, d)
    kb_ref[...] = k.astype(BF)
    if vt_block is None:
        k_ref[...] = k
        v_ref[...] = v
        vb_ref[...] = v.astype(BF)
    else:
        k_ref[0] = k2.T.reshape(2 * H_B, DH_B, ts)
        for hd in range(H_B):
            v_ref[0, pl.ds(hd, ts, stride=H_B), :] = kv[:, d + hd * HB_W:d + (hd + 1) * HB_W]
        vt = kv[:, d:2 * d].T
        ones = jnp.ones((VT_ROWS - HB_W, vt_block), BF)
        for c in range(ts // vt_block):
            for hd in range(H_B):
                vb_ref[0, c, hd, 0:HB_W, :] = vt[hd * HB_W:(hd + 1) * HB_W,
                                                 c * vt_block:(c + 1) * vt_block].astype(BF)
                vb_ref[0, c, hd, HB_W:VT_ROWS, :] = ones


def _lambda(lam_ref, lam_init):
    lp = lam_ref[...]
    l1 = jnp.sum(lp[0:1, :] * lp[1:2, :], axis=-1, keepdims=True)
    l2 = jnp.sum(lp[2:3, :] * lp[3:4, :], axis=-1, keepdims=True)
    return jnp.exp(l1) - jnp.exp(l2) + lam_init


def _project_q(hb, wq_ref, cos, sa, sb, q_scr, tq):
    q = _dot(hb, wq_ref[...])
    lane = lax.broadcasted_iota(jnp.int32, (tq, LANES), 1)
    first = lane < DH_B
    for c in range(H_B):
        rq = _rope64(q[:, c * LANES:(c + 1) * LANES], cos, sa, sb) * Q_SCALE
        q_scr[0:tq, c * LANES:(c + 1) * LANES] = jnp.where(first, rq, 0.0).astype(BF)
        q_scr[tq:2 * tq, c * LANES:(c + 1) * LANES] = jnp.where(first, 0.0, rq).astype(BF)


def _finish_head(acc, l, lam, gsub, lam_init, tq):
    o = acc[0:tq] / l[0:tq] - lam * (acc[tq:2 * tq] / l[tq:2 * tq])
    return _rmsn(o) * gsub * (1.0 - lam_init)


def _roll_rows(x, shift):
    n = x.shape[0]
    return jnp.concatenate([x[n - shift:], x[:n - shift]], axis=0)


def _diffp_kernel(x_ref, mod_ref, g_ref, wqt_ref, k_ref, vt_ref, cos_ref, sa_ref, sb_ref, lam_ref,
                  gsub_ref, wout_ref, o_ref, qt_scr, m_scr, acc_scr, oh_scr, s_scr, e_scr, b_scr, *, tq, lam_init):
    qi = pl.program_id(1)

    @pl.when(qi == 0)
    def _():
        kr = lax.broadcasted_iota(jnp.int32, (tq, HB_W), 0)
        kc = lax.broadcasted_iota(jnp.int32, (tq, HB_W), 1)
        e_scr[...] = jnp.where((kr >> CHUNK_SHIFT) == kc, 1.0, 0.0).astype(BF)
        ra = lax.broadcasted_iota(jnp.int32, (HB_W, 2 * tq), 0)
        ci = lax.broadcasted_iota(jnp.int32, (HB_W, 2 * tq), 1)
        qchunk = jnp.where(ci >= tq, ci - tq, ci) >> CHUNK_SHIFT
        hidden = jnp.where(ra < tq // CHUNK, jnp.where(ra > qchunk, NEG, 0.0), 0.0)
        b_scr[...] = hidden.astype(BF)

    x = x_ref[0]
    shift = mod_ref[0, 3:4, :]
    scale = mod_ref[0, 4:5, :]
    gate = mod_ref[0, 5:6, :]
    h = _rmsn(x) * g_ref[2:3, :] * (1.0 + scale) + shift
    ht = h.T.astype(BF)
    cos = cos_ref[...]
    sa = sa_ref[...]
    sb = sb_ref[...]
    first = lax.broadcasted_iota(jnp.int32, (HB_W, tq), 0) < DH_B
    half = H_B // 2
    for hd in range(H_B):
        if hd % half == 0:
            qt4 = _dot(wqt_ref[hd * HB_W:(hd + half) * HB_W, :], ht)
        blk = qt4[(hd % half) * HB_W:(hd % half + 1) * HB_W, :]
        rq = (blk * cos + _roll_rows(blk, DH_B // 2) * sa + _roll_rows(blk, HB_W - DH_B // 2) * sb) * Q_SCALE
        qt_scr[hd] = jnp.concatenate([jnp.where(first, rq, 0.0), jnp.where(first, 0.0, rq)], axis=1).astype(BF)

    def block(kb, diagonal):
        rows = pl.ds(pl.multiple_of(kb * tq, tq), tq)
        for hd in range(H_B):
            cols = slice(hd * HB_W, (hd + 1) * HB_W)
            if diagonal:
                k_aug = jnp.concatenate([k_ref[0, rows, cols], e_scr[...]], axis=1)
                q_aug = jnp.concatenate([qt_scr[hd], b_scr[...]], axis=0)
                s_scr[hd] = _dot(k_aug, q_aug)
            else:
                s_scr[hd] = _dot(k_ref[0, rows, cols], qt_scr[hd])
        for hd in range(H_B):
            s = s_scr[hd]
            if diagonal:
                m_new = jnp.max(s, axis=0, keepdims=True)
                acc_scr[hd] = _dot(vt_ref[0, kb, hd], jnp.exp2(s - m_new).astype(BF))
            else:
                m_prev = m_scr[hd]
                m_new = jnp.maximum(m_prev, jnp.max(s, axis=0, keepdims=True))
                p = jnp.exp2(s - m_new).astype(BF)
                acc_scr[hd] = jnp.exp2(m_prev - m_new) * acc_scr[hd] + _dot(vt_ref[0, kb, hd], p)
            m_scr[hd] = m_new

    block(qi, True)

    def body(kb, carry):
        block(kb, False)
        return carry

    lax.fori_loop(0, qi, body, 0)

    lam = _lambda(lam_ref, lam_init)
    for hd in range(H_B):
        acc = acc_scr[hd, 0:HB_W, :]
        inv_l = 1.0 / acc_scr[hd, HB_W:HB_W + 1, :]
        o = acc[:, 0:tq] * inv_l[:, 0:tq] - lam * (acc[:, tq:2 * tq] * inv_l[:, tq:2 * tq])
        on = o * lax.rsqrt(jnp.mean(o * o, axis=0, keepdims=True) + EPS)
        oh_scr[hd * HB_W:(hd + 1) * HB_W, :] = (on * gsub_ref[...] * (1.0 - lam_init)).astype(BF)

    y = _dot_tn(oh_scr[...], wout_ref[...])
    o_ref[0] = x + gate * (_rmsn(y) * g_ref[3:4, :])


def _diff_prompt(x, mod, mod_row0, g, w_qt, kb, vtb, cos_t, sa_t, sb_t, lam_p, g_sub, w_out, lam_init, *, tq):
    b, s, d = x.shape
    kern = functools.partial(_diffp_kernel, tq=tq, lam_init=lam_init)
    tspec = pl.BlockSpec((HB_W, tq), lambda i, j: (0, j))
    return pl.pallas_call(
        kern,
        grid=(b, s // tq),
        in_specs=[
            pl.BlockSpec((1, tq, d), lambda i, j: (i, j, 0)),
            pl.BlockSpec((1, N_MOD, d), lambda i, j: (mod_row0 + i, 0, 0)),
            pl.BlockSpec((6, d), lambda i, j: (0, 0)),
            _resident((d, d), lambda i, j: (0, 0)),
            pl.BlockSpec((1, s, d), lambda i, j: (i, 0, 0)),
            pl.BlockSpec((1, s // tq, H_B, VT_ROWS, tq), lambda i, j: (i, 0, 0, 0, 0)),
            tspec, tspec, tspec,
            pl.BlockSpec((4, DH_B), lambda i, j: (0, 0)),
            pl.BlockSpec((HB_W, 1), lambda i, j: (0, 0)),
            _resident((d, d), lambda i, j: (0, 0)),
        ],
        out_specs=pl.BlockSpec((1, tq, d), lambda i, j: (i, j, 0)),
        out_shape=jax.ShapeDtypeStruct(x.shape, F32),
        scratch_shapes=[
            pltpu.VMEM((H_B, HB_W, 2 * tq), BF),
            pltpu.VMEM((H_B, 1, 2 * tq), F32),
            pltpu.VMEM((H_B, VT_ROWS, 2 * tq), F32),
            pltpu.VMEM((d, tq), BF),
            pltpu.VMEM((H_B, tq, 2 * tq), F32),
            pltpu.VMEM((tq, HB_W), BF),
            pltpu.VMEM((HB_W, 2 * tq), BF),
        ],
        compiler_params=_params(2),
        name="diff_prompt",
    )(x, mod, g, w_qt, kb, vtb, cos_t, sa_t, sb_t, lam_p, g_sub.reshape(HB_W, 1), w_out)


def _diffs_step(kc, n_kc, x_ref, mod_ref, g_ref, wq_ref, kn_ref, vn_ref, kc_ref, vc_ref, cos_ref, sa_ref, sb_ref,
                lam_ref, gsub_ref, wout_ref, o_ref, q_scr, m_scr, acc_scr, oh_scr, s_scr, *, tq, lam_init):
    rows2 = 2 * tq

    def with_ones(v):
        return jnp.concatenate([v, jnp.ones(v.shape, BF)], axis=1)

    @pl.when(kc == 0)
    def _():
        x = x_ref[0]
        h = _rmsn(x) * g_ref[2:3, :] * (1.0 + mod_ref[0, 4:5, :]) + mod_ref[0, 3:4, :]
        _project_q(h.astype(BF), wq_ref, cos_ref[...], sa_ref[...], sb_ref[...], q_scr, tq)
        for hd in range(H_B):
            cols = slice(hd * HB_W, (hd + 1) * HB_W)
            s = _dot_nt(q_scr[:, cols], kn_ref[0, :, cols])
            m0 = jnp.max(s, axis=-1, keepdims=True)
            m_scr[hd] = jnp.broadcast_to(m0, (rows2, LANES))
            acc_scr[hd] = _dot(jnp.exp2(s - m0).astype(BF), with_ones(vn_ref[0, :, cols]))

    tkc = kc_ref.shape[3]
    for hd in range(H_B):
        s_scr[hd] = jnp.concatenate(
            [_dot(q_scr[t * tq:(t + 1) * tq, (2 * hd + t) * DH_B:(2 * hd + t + 1) * DH_B],
                  kc_ref[0, 2 * hd + t].astype(BF)) for t in range(2)], axis=0)
    for hd in range(H_B):
        s = s_scr[hd]
        v_h = with_ones(vc_ref[0, pl.ds(hd, tkc, stride=H_B), :].astype(BF))
        m_prev = m_scr[hd][:, 0:1]
        m_new = jnp.maximum(m_prev, jnp.max(s, axis=-1, keepdims=True))
        acc_scr[hd] = jnp.exp2(m_prev - m_new) * acc_scr[hd] + _dot(jnp.exp2(s - m_new).astype(BF), v_h)
        m_scr[hd] = jnp.broadcast_to(m_new, (rows2, LANES))

    @pl.when(kc == n_kc - 1)
    def _():
        lam = _lambda(lam_ref, lam_init)
        for hd in range(H_B):
            cols = slice(hd * HB_W, (hd + 1) * HB_W)
            oh_scr[:, cols] = _finish_head(acc_scr[hd, :, 0:HB_W], acc_scr[hd, :, HB_W:HB_W + 1], lam,
                                           gsub_ref[...], lam_init, tq).astype(BF)
        y = _dot(oh_scr[...], wout_ref[...])
        o_ref[0] = x_ref[0] + mod_ref[0, 5:6, :] * (_rmsn(y) * g_ref[3:4, :])


N_DIFFS_IN = 14
FFN_PHASES = 4


def _ffn_diffs_kernel(xp_ref, modp_ref, gp_ref, win_ref, wout_ref, *rest, m0, g0, fc, tq, lam_init):
    d_in = rest[:N_DIFFS_IN]
    op_ref, os_ref = rest[N_DIFFS_IN:N_DIFFS_IN + 2]
    h_scr, accf_scr = rest[N_DIFFS_IN + 2:N_DIFFS_IN + 4]
    d_scr = rest[N_DIFFS_IN + 4:]
    d = xp_ref.shape[-1]
    phase = lax.rem(pl.program_id(0), FFN_PHASES)
    n_chunks = D_FF // fc
    per = -(-n_chunks // FFN_PHASES)

    for p in range(FFN_PHASES):
        @pl.when(phase == p)
        def _(p=p):
            if p == 0:
                gain_in = gp_ref[g0:g0 + 1, :] * (1.0 + modp_ref[:, m0 + 1:m0 + 2, :])
                h = _rmsn(xp_ref[...]) * gain_in + modp_ref[:, m0:m0 + 1, :]
                h_scr[...] = h.reshape(-1, d).astype(BF)
            hb = h_scr[...]
            acc = None
            for c in range(p * per, min((p + 1) * per, n_chunks)):
                a = _dot(hb, win_ref[:, c * fc:(c + 1) * fc])
                b = _dot(hb, win_ref[:, D_FF + c * fc:D_FF + (c + 1) * fc])
                part = _dot((_silu(a) * b).astype(BF), wout_ref[c * fc:(c + 1) * fc, :])
                acc = part if acc is None else acc + part
            if p > 0:
                acc = accf_scr[...] + acc
            if p < FFN_PHASES - 1:
                accf_scr[...] = acc
            else:
                x = xp_ref[...]
                gain_out = gp_ref[g0 + 1:g0 + 2, :] * (HALF * modp_ref[:, m0 + 2:m0 + 3, :])
                op_ref[...] = x + _rmsn(acc).reshape(x.shape) * gain_out

    _diffs_step(phase, FFN_PHASES, *d_in, os_ref, *d_scr, tq=tq, lam_init=lam_init)


def _ffn_diff_sample(xp, mod, row_p, g, w_in, w_out, wsel, x, mod1, row_s, g1, w_q, kn, vn, cache_k, cache_v,
                     cos_t, sa_t, sb_t, lam_p, g_sub, w_out_b, lam_init, *, layer, idx, ts):
    bp, sp, d = xp.shape
    b, tq, _ = x.shape
    past = cache_k.shape[1]
    tkc = past // FFN_PHASES
    nst = sp // ts
    assert bp * nst == b
    cache_kt = jnp.transpose(cache_k, (0, 2, 3, 1))
    cache_vf = cache_v.reshape(b, past * H_B, HB_W)

    def tile(t):
        return t // FFN_PHASES

    xp_spec = pl.BlockSpec((1, ts, d), lambda t: (tile(t) // nst, tile(t) % nst, 0))
    tspec = pl.BlockSpec((tq, LANES), lambda t: (0, 0))
    xspec = pl.BlockSpec((1, tq, d), lambda t: (tile(t), 0, 0))
    kspec = pl.BlockSpec((1, 2 * H_B, DH_B, tkc), lambda t: (tile(t), 0, 0, t % FFN_PHASES))
    vspec = pl.BlockSpec((1, tkc * H_B, HB_W), lambda t: (tile(t), t % FFN_PHASES, 0))
    kern = functools.partial(_ffn_diffs_kernel, m0=6 * idx, g0=4 * idx, fc=256, tq=tq, lam_init=lam_init)
    return pl.pallas_call(
        kern,
        grid=(b * FFN_PHASES,),
        in_specs=[
            xp_spec,
            pl.BlockSpec((None, 1, N_MOD, d), lambda t: (layer, row_p + tile(t) // nst, 0, 0)),
            pl.BlockSpec((None, 6, d), lambda t: (layer, 0, 0)),
            _resident((None, d, 2 * D_FF), lambda t: (wsel, 0, 0)),
            _resident((None, D_FF, d), lambda t: (wsel, 0, 0)),
            xspec,
            pl.BlockSpec((1, N_MOD, d), lambda t: (row_s + tile(t), 0, 0)),
            pl.BlockSpec((6, d), lambda t: (0, 0)),
            _resident((d, d), lambda t: (0, 0)),
            xspec, xspec, kspec, vspec,
            tspec, tspec, tspec,
            pl.BlockSpec((4, DH_B), lambda t: (0, 0)),
            pl.BlockSpec((1, HB_W), lambda t: (0, 0)),
            _resident((d, d), lambda t: (0, 0)),
        ],
        out_specs=[xp_spec, xspec],
        out_shape=[jax.ShapeDtypeStruct(xp.shape, F32), jax.ShapeDtypeStruct(x.shape, F32)],
        scratch_shapes=[
            pltpu.VMEM((ts, d), BF),
            pltpu.VMEM((ts, d), F32),
            pltpu.VMEM((2 * tq, d), BF),
            pltpu.VMEM((H_B, 2 * tq, LANES), F32),
            pltpu.VMEM((H_B, 2 * tq, 2 * HB_W), F32),
            pltpu.VMEM((tq, d), BF),
            pltpu.VMEM((H_B, 2 * tq, tkc), F32),
        ],
        compiler_params=_params(1),
        name="ffn_diff_sample",
    )(xp, mod, g, w_in, w_out, x, mod1, g1, w_q, kn, vn, cache_kt, cache_vf, cos_t, sa_t, sb_t, lam_p,
      g_sub.reshape(1, HB_W), w_out_b)


def _rope_tables_a(pos):
    inv = jnp.power(ROPE_THETA, -jnp.arange(0, DK_A, 2, dtype=jnp.float32) / DK_A)
    ang = pos[:, None] * inv[None, :]
    cos, sin = jnp.cos(ang), jnp.sin(ang)
    return jnp.concatenate([cos, cos], axis=-1), jnp.concatenate([-sin, sin], axis=-1)


def _rope_tables_b(pos):
    inv = jnp.power(ROPE_THETA, -jnp.arange(0, DH_B, 2, dtype=jnp.float32) / DH_B)
    ang = pos[:, None] * inv[None, :]
    cos, sin = jnp.cos(ang), jnp.sin(ang)
    zero = jnp.zeros_like(sin)
    cos_t = jnp.concatenate([cos, cos, cos, cos], axis=-1)
    sa = jnp.concatenate([zero, sin, zero, sin], axis=-1)
    sb = jnp.concatenate([-sin, zero, -sin, zero], axis=-1)
    return cos_t, sa, sb


def kernel(x_prompt, x_sample, state_ret, cache_k, cache_v, c_prompt, c_sample, w_ada, b_ada, g_norm, w_ffn_in,
           w_ffn_out, w_in_a, g_gn_a, w_out_a, w_ada_kv, b_ada_kv, g_kv, w_kv, w_q_b, lam_b, g_subln_b, w_out_b):
    d = D_MODEL
    bp, sp, _ = x_prompt.shape
    bs, ss, _ = x_sample.shape

    n_rows = -(-(bs + bp) // 16) * 16
    c_all = jnp.concatenate([c_sample, c_prompt, jnp.zeros((n_rows - bs - bp, d), F32)], axis=0)
    mod = _ada_proj(c_all, w_ada, b_ada).reshape(2, n_rows, N_MOD, d)
    modkv = _ada_proj(c_all, w_ada_kv[None], b_ada_kv[None]).reshape(n_rows, 2, d)
    row_s, row_p = 0, bs

    wfi0 = w_ffn_in[0, 0].astype(BF)[None]
    wfo0 = w_ffn_out[0, 0].astype(BF)[None]
    wqb = w_q_b.astype(BF)
    wob = w_out_b.astype(BF)
    ts_p = 512
    nb_s = ts_p // ss
    nb_r = 4
    t_ret = 256
    tq = 256
    n_ffn = w_ffn_in.shape[0] * w_ffn_in.shape[1]
    n_steps = bp * sp // ts_p
    per_set = n_steps // n_ffn
    ffn_casts = (
        (w_ffn_in.reshape(n_ffn * d, 2 * D_FF), d, (n_ffn - 1) * d, d // per_set),
        (w_ffn_out.reshape(n_ffn * D_FF, d), D_FF, (n_ffn - 1) * D_FF, D_FF // per_set),
        (w_in_a.reshape(d, 2 * QK_A + 2 * V_A), 0, d, d // n_steps),
        (w_out_a.reshape(V_A, d), 0, V_A, V_A // n_steps),
        (w_kv, 0, d, d // n_steps),
    )

    pos_p = jnp.arange(sp, dtype=jnp.float32)
    pos_s = PAST_LEN + jnp.arange(ss, dtype=jnp.float32)

    cos_ap, sin_ap = _rope_tables_a(pos_p)
    cos_as, sin_as = [jnp.tile(a, (nb_r, 1)) for a in _rope_tables_a(pos_s)]
    tab_bp = _rope_tables_b(pos_p)
    tab_bs = _rope_tables_b(pos_s)
    tab_bs_tiled = [jnp.tile(a, (nb_s, 1)) for a in tab_bs]

    xp, wfi_rest, wfo_rest, wia, woa, wkv = _ffn(x_prompt, mod, row_p, g_norm, wfi0, wfo0, 0, layer=0, idx=0,
                                                 nb=1, ts=ts_p, casts=ffn_casts)
    wfi_rest = wfi_rest.reshape(n_ffn - 1, d, 2 * D_FF)
    wfo_rest = wfo_rest.reshape(n_ffn - 1, D_FF, d)

    def ffn_p(x, l, i, kv=None):
        return _ffn(x, mod, row_p, g_norm, wfi_rest, wfo_rest, 2 * l + i - 1, layer=l, idx=i, nb=1, ts=ts_p, kv=kv)

    def ffn_s(x, l, i, kv=None):
        if (l, i) == (0, 0):
            return _ffn(x, mod, row_s, g_norm, wfi0, wfo0, 0, layer=l, idx=i, nb=nb_s, ts=ss)
        return _ffn(x, mod, row_s, g_norm, wfi_rest, wfo_rest, 2 * l + i - 1, layer=l, idx=i, nb=nb_s, ts=ss,
                    kv=kv)

    xs = ffn_s(x_sample, 0, 0)
    xp, ret_p = _ret_mixer(xp, mod[0], row_p, g_norm[0], wia, g_gn_a[0], woa, cos_ap, sin_ap, None,
                           nb=1, t=t_ret)
    xs, ret_s = _ret_mixer(xs, mod[0], row_s, g_norm[0], wia, g_gn_a[0], woa, cos_as, sin_as, state_ret[0],
                           nb=nb_r, t=ss)
    xp, k_p, v_p, kb_p, vtb_p = ffn_p(xp, 0, 1, kv=(modkv, g_kv, wkv, tab_bp, tq))
    xs, k_s, v_s, kb_s, vb_s = ffn_s(xs, 0, 1, kv=(modkv, g_kv, wkv, tab_bs_tiled, None))

    lam_init = 0.8 - 0.6 * math.exp(-0.3 * 1)
    xp = ffn_p(xp, 1, 0)
    xs = ffn_s(xs, 1, 0)
    xp = _diff_prompt(xp, mod[1], row_p, g_norm[1], w_q_b[0].T.astype(BF), kb_p, vtb_p,
                      *[a.T for a in tab_bp], lam_b[0], g_subln_b[0], wob[0],
                      lam_init, tq=tq)
    xp, xs = _ffn_diff_sample(xp, mod, row_p, g_norm, wfi_rest, wfo_rest, 2,
                              xs, mod[1], row_s, g_norm[1], wqb[0], kb_s, vb_s, cache_k, cache_v,
                              *tab_bs, lam_b[0], g_subln_b[0], wob[0], lam_init, layer=1, idx=1, ts=ts_p)
    xs = ffn_s(xs, 1, 1)

    return (xp, xs,
            ret_p[None], jnp.transpose(k_p, (0, 3, 1, 2)), v_p.reshape(bp, sp, H_B, 2 * DH_B),
            ret_s[None], k_s.reshape(bs, ss, 2 * H_B, DH_B), v_s.reshape(bs, ss, H_B, 2 * DH_B))
```
